```python
import jax
import jax.numpy as jnp
from jax import lax
import numpy as np

D_MODEL = 1024
BATCH = 4
SEQ = 8192
DEPTH = 4
DEC_BATCH = 2
DEC_SEQ = 8192
PAST_LEN = 128

N_MIXERS = 3
HEAD_DIM = 64
Q_BLOCK = 128
ROPE_THETA = 10000.0
GRID_W = 64

A_HEADS = 16
A_KV_HEADS = 4
A_WINDOW = 128
A_COLS = (A_HEADS + 2 * A_KV_HEADS) * HEAD_DIM

B_HEADS = 16
B_KV_HEADS = 4
B_COLS = (B_HEADS + 2 * B_KV_HEADS) * HEAD_DIM

C_PATTERNS = ((128, 1), (512, 4), (2048, 16))
C_HEADS = 8
C_KV_HEADS = 2
C_GROUP_COLS = (C_HEADS + 2 * C_KV_HEADS) * HEAD_DIM
C_COLS = len(C_PATTERNS) * C_GROUP_COLS

N_MEM = 256
MEM_HEADS = 4
MEM_HEAD_DIM = D_MODEL // MEM_HEADS

N_EXPERTS = 32
N_EXPERT_GROUPS = 8
EXPERTS_PER_GROUP = N_EXPERTS // N_EXPERT_GROUPS
TOP_K = 2
D_EXPERT = D_MODEL // 4
EXPERT_BLOCK = 128

LN_EPS = 1e-5
QK_NORM_EPS = 1e-6
DEEPNORM_ALPHA = (2 * DEPTH) ** 0.25
DEEPNORM_BETA = (8 * DEPTH) ** -0.25

kernel_name = 'hybrid_bidir_encoder_interleaved_moe'


def layer_norm(x, g, b):
    xf = x.astype(jnp.float32)
    mu = jnp.mean(xf, axis=-1, keepdims=True)
    var = jnp.mean(jnp.square(xf - mu), axis=-1, keepdims=True)
    y = (xf - mu) * lax.rsqrt(var + LN_EPS) * g.astype(jnp.float32) + b.astype(jnp.float32)
    return y.astype(x.dtype)


def rms_norm(x, g):
    xf = x.astype(jnp.float32)
    y = xf * lax.rsqrt(jnp.mean(jnp.square(xf), axis=-1, keepdims=True) + QK_NORM_EPS) * g.astype(jnp.float32)
    return y.astype(x.dtype)


def rope_tables(pos, dim):
    inv_freq = ROPE_THETA ** (-jnp.arange(0, dim, 2, dtype=jnp.float32) / dim)
    ang = pos.astype(jnp.float32)[:, None] * inv_freq[None, :]
    return jnp.cos(ang), jnp.sin(ang)


def apply_rope(x, cs):
    cos, sin = cs
    c = cos[None, :, None, :]
    s = sin[None, :, None, :]
    x1, x2 = jnp.split(x.astype(jnp.float32), 2, axis=-1)
    return jnp.concatenate([x1 * c - x2 * s, x2 * c + x1 * s], axis=-1).astype(x.dtype)


def apply_axial_rope(x, row_cs, col_cs):
    half = x.shape[-1] // 2
    return jnp.concatenate([apply_rope(x[..., :half], row_cs), apply_rope(x[..., half:], col_cs)], axis=-1)


def position_tables(seq):
    rows = seq // GRID_W
    t = jnp.arange(seq, dtype=jnp.int32)
    row = jnp.repeat(jnp.arange(rows, dtype=jnp.int32), GRID_W)
    col = jnp.tile(jnp.arange(GRID_W, dtype=jnp.int32), rows)
    return rope_tables(t, HEAD_DIM), rope_tables(row, HEAD_DIM // 2), rope_tables(col, HEAD_DIM // 2)


def split_heads(qkv, n_heads, n_kv):
    b, s, _ = qkv.shape
    q = qkv[..., :n_heads * HEAD_DIM].reshape(b, s, n_heads, HEAD_DIM)
    k = qkv[..., n_heads * HEAD_DIM:(n_heads + n_kv) * HEAD_DIM].reshape(b, s, n_kv, HEAD_DIM)
    v = qkv[..., (n_heads + n_kv) * HEAD_DIM:].reshape(b, s, n_kv, HEAD_DIM)
    return q, k, v


def window_sink_attention(q, k, v, sink):
    b, s, h, dh = q.shape
    kvh = k.shape[2]
    g = h // kvh
    span = Q_BLOCK + 2 * A_WINDOW
    pad = ((0, 0), (A_WINDOW, A_WINDOW), (0, 0), (0, 0))
    kp = jnp.pad(k, pad)
    vp = jnp.pad(v, pad)
    sink_l = sink.astype(jnp.float32).reshape(1, kvh, g, 1, 1)
    scale = dh ** -0.5

    def one_block(j):
        q0 = j * Q_BLOCK
        qj = lax.dynamic_slice_in_dim(q, q0, Q_BLOCK, axis=1).reshape(b, Q_BLOCK, kvh, g, dh)
        kj = lax.dynamic_slice_in_dim(kp, q0, span, axis=1)
        vj = lax.dynamic_slice_in_dim(vp, q0, span, axis=1)
        qpos = q0 + jnp.arange(Q_BLOCK, dtype=jnp.int32)
        kpos = q0 - A_WINDOW + jnp.arange(span, dtype=jnp.int32)
        ok = ((jnp.abs(kpos[None, :] - qpos[:, None]) <= A_WINDOW)
              & (kpos >= 0)[None, :] & (kpos < s)[None, :])
        sc = jnp.einsum('bqkgd,bnkd->bkgqn', qj, kj, preferred_element_type=jnp.float32) * scale
        sc = jnp.where(ok, sc, -jnp.inf)
        m = jnp.maximum(jnp.max(sc, axis=-1, keepdims=True), sink_l)
        p = jnp.exp(sc - m)
        p = p / (jnp.sum(p, axis=-1, keepdims=True) + jnp.exp(sink_l - m))
        o = jnp.einsum('bkgqn,bnkd->bqkgd', p.astype(v.dtype), vj)
        return o.reshape(b, Q_BLOCK, h, dh)

    out = lax.map(one_block, jnp.arange(s // Q_BLOCK, dtype=jnp.int32))
    return out.transpose(1, 0, 2, 3, 4).reshape(b, s, h, dh)


def dense_block_attention(q, k, v):
    b, s, h, dh = q.shape
    kvh = k.shape[2]
    g = h // kvh
    scale = dh ** -0.5
    qb = q.reshape(b, s // Q_BLOCK, Q_BLOCK, kvh, g, dh).transpose(1, 0, 2, 3, 4, 5)

    def one_block(qj):
        sc = jnp.einsum('bqkgd,bnkd->bkgqn', qj, k, preferred_element_type=jnp.float32) * scale
        p = jax.nn.softmax(sc, axis=-1)
        return jnp.einsum('bkgqn,bnkd->bqkgd', p.astype(v.dtype), v)

    out = lax.map(one_block, qb)
    return out.transpose(1, 0, 2, 3, 4, 5).reshape(b, s, h, dh)


def dilated_attention(q, k, v, window, dilation):
    b, s, h, dh = q.shape
    kvh = k.shape[2]
    g = h // kvh
    n_side = (window // 2) // dilation
    offsets = jnp.arange(-n_side, n_side + 1, dtype=jnp.int32) * dilation
    scale = dh ** -0.5

    def one_block(j):
        q0 = j * Q_BLOCK
        qj = lax.dynamic_slice_in_dim(q, q0, Q_BLOCK, axis=1).reshape(b, Q_BLOCK, kvh, g, dh)
        kpos = (q0 + jnp.arange(Q_BLOCK, dtype=jnp.int32))[:, None] + offsets[None, :]
        ok = (kpos >= 0) & (kpos < s)
        idx = jnp.clip(kpos, 0, s - 1)
        kj = jnp.take(k, idx, axis=1)
        vj = jnp.take(v, idx, axis=1)
        sc = jnp.einsum('bqkgd,bqnkd->bkgqn', qj, kj, preferred_element_type=jnp.float32) * scale
        sc = jnp.where(ok, sc, -jnp.inf)
        m = jnp.max(sc, axis=-1, keepdims=True)
        p = jnp.exp(sc - m)
        l = jnp.sum(p, axis=-1, keepdims=True)
        o = jnp.einsum('bkgqn,bqnkd->bqkgd', (p / l).astype(v.dtype), vj)
        lse = (m + jnp.log(l))[..., 0].transpose(0, 3, 1, 2).reshape(b, Q_BLOCK, h)
        return o.reshape(b, Q_BLOCK, h, dh), lse

    o, lse = lax.map(one_block, jnp.arange(s // Q_BLOCK, dtype=jnp.int32))
    return (o.transpose(1, 0, 2, 3, 4).reshape(b, s, h, dh),
            lse.transpose(1, 0, 2, 3).reshape(b, s, h))


def mixer_a(x, w_in, sink, w_out, rope_1d):
    q, k, v = split_heads(x @ w_in, A_HEADS, A_KV_HEADS)
    o = window_sink_attention(apply_rope(q, rope_1d), apply_rope(k, rope_1d), v, sink)
    return o.reshape(x.shape[0], x.shape[1], A_HEADS * HEAD_DIM) @ w_out


def mixer_b(x, w_in, q_norm, k_norm, w_out, rope_row, rope_col):
    q, k, v = split_heads(x @ w_in, B_HEADS, B_KV_HEADS)
    q = apply_axial_rope(rms_norm(q, q_norm), rope_row, rope_col)
    k = apply_axial_rope(rms_norm(k, k_norm), rope_row, rope_col)
    o = dense_block_attention(q, k, v)
    return o.reshape(x.shape[0], x.shape[1], B_HEADS * HEAD_DIM) @ w_out


def mixer_c(x, w_in, w_out, rope_1d):
    qkv = x @ w_in
    outs, lses = [], []
    for gi, (window, dilation) in enumerate(C_PATTERNS):
        cols = qkv[..., gi * C_GROUP_COLS:(gi + 1) * C_GROUP_COLS]
        q, k, v = split_heads(cols, C_HEADS, C_KV_HEADS)
        o, lse = dilated_attention(apply_rope(q, rope_1d), apply_rope(k, rope_1d), v, window, dilation)
        outs.append(o)
        lses.append(lse)
    wts = jax.nn.softmax(jnp.stack(lses, axis=0), axis=0)
    o = jnp.einsum('pbsh,pbshd->bshd', wts, jnp.stack(outs, axis=0).astype(jnp.float32)).astype(x.dtype)
    return o.reshape(x.shape[0], x.shape[1], C_HEADS * HEAD_DIM) @ w_out


def memory_attention(x, mem, w_q, w_kv, w_out):
    b, s, _ = x.shape
    m = mem.shape[1]
    q = (x @ w_q).reshape(b, s, MEM_HEADS, MEM_HEAD_DIM)
    kv = (mem @ w_kv).reshape(b, m, 2, MEM_HEADS, MEM_HEAD_DIM)
    k, v = kv[:, :, 0], kv[:, :, 1]
    sc = jnp.einsum('bshd,bmhd->bhsm', q, k, preferred_element_type=jnp.float32) * (MEM_HEAD_DIM ** -0.5)
    p = jax.nn.softmax(sc, axis=-1)
    o = jnp.einsum('bhsm,bmhd->bshd', p.astype(v.dtype), v)
    return o.reshape(b, s, MEM_HEADS * MEM_HEAD_DIM) @ w_out


def moe(x, router_w, router_bias, w_gu, w_down):
    b, s, d = x.shape
    n = b * s
    x2 = x.reshape(n, d)
    logits = jnp.einsum('nd,de->ne', x2, router_w, preferred_element_type=jnp.float32)
    scores = jax.nn.sigmoid(logits)
    biased = (scores + router_bias.astype(jnp.float32)).reshape(n, N_EXPERT_GROUPS, EXPERTS_PER_GROUP)
    group_score = jnp.sum(lax.top_k(biased, TOP_K)[0], axis=-1)
    group = jnp.argmax(group_score, axis=-1).astype(jnp.int32)
    in_group = jnp.take_along_axis(biased, group[:, None, None], axis=1)[:, 0]
    local = lax.top_k(in_group, TOP_K)[1].astype(jnp.int32)
    expert = group[:, None] * EXPERTS_PER_GROUP + local
    gate = jnp.take_along_axis(scores, expert, axis=1)
    gate = gate / jnp.sum(gate, axis=-1, keepdims=True)

    n_assign = n * TOP_K
    n_blocks = -(-n_assign // EXPERT_BLOCK) + N_EXPERTS
    flat_e = expert.reshape(n_assign)
    flat_tok = jnp.repeat(jnp.arange(n, dtype=jnp.int32), TOP_K)
    order = jnp.argsort(flat_e)
    se = flat_e[order]
    st = flat_tok[order]
    sw = gate.reshape(n_assign)[order]
    counts = jnp.bincount(flat_e, length=N_EXPERTS).astype(jnp.int32)
    padded = ((counts + EXPERT_BLOCK - 1) // EXPERT_BLOCK) * EXPERT_BLOCK
    pad_end = jnp.cumsum(padded)
    pad_start = pad_end - padded
    start = jnp.cumsum(counts) - counts
    dest = pad_start[se] + jnp.arange(n_assign, dtype=jnp.int32) - start[se]
    slot_tok = jnp.full((n_blocks * EXPERT_BLOCK,), n, jnp.int32).at[dest].set(st)
    block_e = jnp.minimum(
        jnp.searchsorted(pad_end, jnp.arange(n_blocks, dtype=jnp.int32) * EXPERT_BLOCK, side='right'),
        N_EXPERTS - 1).astype(jnp.int32)
    xpad = jnp.concatenate([x2, jnp.zeros((1, d), x2.dtype)], axis=0)
    xb = xpad[slot_tok].reshape(n_blocks, EXPERT_BLOCK, d)

    def expert_block(args):
        xi, e = args
        gu = xi @ w_gu[e]
        g_, u_ = jnp.split(gu, 2, axis=-1)
        return (jax.nn.silu(g_) * u_) @ w_down[e]

    yb = lax.map(expert_block, (xb, block_e)).reshape(n_blocks * EXPERT_BLOCK, d)
    y = jnp.zeros((n, d), x.dtype).at[st].add(yb[dest] * sw[:, None].astype(x.dtype))
    return y.reshape(b, s, d)


def setup_inputs(seed: int = 0) -> dict:
    key = jax.random.key(seed)
    ks = jax.random.split(key, 22)
    f32 = jnp.float32

    def nrm(k, shape, scale):
        return jax.random.normal(k, shape, f32) * scale

    n_a = len(range(0, DEPTH, N_MIXERS))
    n_b = len(range(1, DEPTH, N_MIXERS))
    n_c = len(range(2, DEPTH, N_MIXERS))
    din = D_MODEL ** -0.5
    a_w = A_HEADS * HEAD_DIM
    b_w = B_HEADS * HEAD_DIM
    c_w = C_HEADS * HEAD_DIM
    m_w = MEM_HEADS * MEM_HEAD_DIM
    return {
        'x_prompt': nrm(ks[0], (BATCH, SEQ, D_MODEL), 1.0),
        'x_sample': nrm(ks[1], (DEC_BATCH, DEC_SEQ, D_MODEL), 1.0),
        'mem_prompt': nrm(ks[2], (BATCH, N_MEM, D_MODEL), 1.0),
        'mem_sample': nrm(ks[3], (DEC_BATCH, N_MEM, D_MODEL), 1.0),
        'a_w_in': nrm(ks[4], (n_a, D_MODEL, A_COLS), din),
        'a_sink': nrm(ks[5], (n_a, A_HEADS), 0.5),
        'a_w_out': nrm(ks[6], (n_a, a_w, D_MODEL), DEEPNORM_BETA * a_w ** -0.5),
        'b_w_in': nrm(ks[7], (n_b, D_MODEL, B_COLS), din),
        'b_q_norm': 1.0 + nrm(ks[8], (n_b, HEAD_DIM), 0.02),
        'b_k_norm': 1.0 + nrm(ks[9], (n_b, HEAD_DIM), 0.02),
        'b_w_out': nrm(ks[10], (n_b, b_w, D_MODEL), DEEPNORM_BETA * b_w ** -0.5),
        'c_w_in': nrm(ks[11], (n_c, D_MODEL, C_COLS), din),
        'c_w_out': nrm(ks[12], (n_c, c_w, D_MODEL), DEEPNORM_BETA * c_w ** -0.5),
        'm_w_q': nrm(ks[13], (DEPTH, D_MODEL, m_w), din),
        'm_w_kv': nrm(ks[14], (DEPTH, D_MODEL, 2 * m_w), din),
        'm_w_out': nrm(ks[15], (DEPTH, m_w, D_MODEL), DEEPNORM_BETA * m_w ** -0.5),
        'ln_g': 1.0 + nrm(ks[16], (DEPTH, 3, D_MODEL), 0.02),
        'ln_b': nrm(ks[17], (DEPTH, 3, D_MODEL), 0.02),
        'router_w': nrm(ks[18], (D_MODEL, N_EXPERTS), din),
        'router_bias': nrm(ks[19], (N_EXPERTS,), 0.01),
        'e_w_gu': nrm(ks[20], (DEPTH, N_EXPERTS, D_MODEL, 2 * D_EXPERT), din),
        'e_w_down': nrm(ks[21], (DEPTH, N_EXPERTS, D_EXPERT, D_MODEL), DEEPNORM_BETA * D_EXPERT ** -0.5),
    }


def reference(x_prompt, x_sample, mem_prompt, mem_sample, a_w_in, a_sink, a_w_out,
              b_w_in, b_q_norm, b_k_norm, b_w_out, c_w_in, c_w_out,
              m_w_q, m_w_kv, m_w_out, ln_g, ln_b, router_w, router_bias, e_w_gu, e_w_down):
    def trunk(x, mem):
        rope_1d, rope_row, rope_col = position_tables(x.shape[1])
        for i in range(DEPTH):
            kind, j = i % N_MIXERS, i // N_MIXERS
            if kind == 0:
                h = mixer_a(x, a_w_in[j], a_sink[j], a_w_out[j], rope_1d)
            elif kind == 1:
                h = mixer_b(x, b_w_in[j], b_q_norm[j], b_k_norm[j], b_w_out[j], rope_row, rope_col)
            else:
                h = mixer_c(x, c_w_in[j], c_w_out[j], rope_1d)
            x = layer_norm(DEEPNORM_ALPHA * x + h, ln_g[i, 0], ln_b[i, 0])
            h = memory_attention(x, mem, m_w_q[i], m_w_kv[i], m_w_out[i])
            x = layer_norm(DEEPNORM_ALPHA * x + h, ln_g[i, 1], ln_b[i, 1])
            h = moe(x, router_w, router_bias, e_w_gu[i], e_w_down[i])
            x = layer_norm(DEEPNORM_ALPHA * x + h, ln_g[i, 2], ln_b[i, 2])
        return x

    y_prompt = trunk(x_prompt, mem_prompt)
    y_sample = trunk(x_sample, mem_sample)
    return (y_prompt, y_sample)
```

```python
import functools

import jax
import jax.numpy as jnp
import numpy as np
from jax import lax
from jax.experimental import pallas as pl
from jax.experimental.pallas import tpu as pltpu

F32 = jnp.float32
BF16 = jnp.bfloat16

D_MODEL = 1024
DEPTH = 4
N_MIXERS = 3
HEAD_DIM = 64
ROPE_THETA = 10000.0
GRID_W = 64

A_HEADS, A_KV_HEADS, A_WINDOW = 16, 4, 128
B_HEADS, B_KV_HEADS = 16, 4
C_PATTERNS = ((128, 1), (512, 4), (2048, 16))
C_HEADS, C_KV_HEADS = 8, 2

MEM_HEADS = 4
MEM_HEAD_DIM = D_MODEL // MEM_HEADS

N_EXPERTS = 32
N_EXPERT_GROUPS = 8
EXPERTS_PER_GROUP = 4
D_EXPERT = D_MODEL // 4
N_PAIRS = 6
N_CLASSES = N_EXPERT_GROUPS * N_PAIRS
MOE_BLOCK = 128

LN_EPS = 1e-5
QK_NORM_EPS = 1e-6
DEEPNORM_ALPHA = (2 * DEPTH) ** 0.25

LANES = 128
TOKEN_TILE = 512
VMEM_LIMIT_BYTES = 48 * 1024 * 1024

NEG_INF = float("-inf")


def _params(*semantics):
    return pltpu.CompilerParams(dimension_semantics=semantics, vmem_limit_bytes=VMEM_LIMIT_BYTES)


def _layer_norm(z, g, b):
    mu = jnp.mean(z, axis=-1, keepdims=True)
    zc = z - mu
    var = jnp.mean(zc * zc, axis=-1, keepdims=True)
    return zc * lax.rsqrt(var + LN_EPS) * g + b


def _dot(a, b):
    return jnp.dot(a, b, preferred_element_type=F32)


def _dot_nt(a, b):
    return lax.dot_general(a, b, (((1,), (1,)), ((), ())), preferred_element_type=F32)


def _qkv_kernel(x_ref, w_ref, cos_ref, sin_ref, gain_ref, ones_ref, q_ref, k_ref, v_ref, *,
                nq, nk, half, qk_norm, q_scale):
    acc = _dot(x_ref[...].astype(BF16), w_ref[...])
    cos = cos_ref[...]
    sin = sin_ref[...]
    lane = lax.broadcasted_iota(jnp.int32, cos.shape, 1)
    first = (lane % (2 * half)) < half

    def rotate(a, gain, scale):
        if qk_norm:
            a2 = a * a
            hi = a2.astype(BF16)
            lo = (a2 - hi.astype(F32)).astype(BF16)
            ms = (_dot(hi, ones_ref[...]) + _dot(lo, ones_ref[...])) * (1.0 / HEAD_DIM)
            a = a * lax.rsqrt(ms + QK_NORM_EPS) * gain
        partner = jnp.where(first, pltpu.roll(a, LANES - half, 1), pltpu.roll(a, half, 1))
        r = a * cos + partner * sin
        if scale != 1.0:
            r = r * scale
        return r.astype(BF16)

    gq = gain_ref[0:1, :]
    gk = gain_ref[1:2, :]
    for c in range(nq // LANES):
        q_ref[:, c * LANES:(c + 1) * LANES] = rotate(acc[:, c * LANES:(c + 1) * LANES], gq, q_scale)
    for c in range(nk // LANES):
        k_ref[:, c * LANES:(c + 1) * LANES] = rotate(acc[:, nq + c * LANES:nq + (c + 1) * LANES], gk, 1.0)
    v_ref[...] = acc[:, nq + nk:].astype(BF16)


def _qkv_project(x2, w, cos, sin, gains, ones_bd, *, nq, nk, nv, half, qk_norm, q_scale, seq):
    n, d = x2.shape
    tm = TOKEN_TILE
    tiles_per_seq = seq // tm
    kern = functools.partial(_qkv_kernel, nq=nq, nk=nk, half=half, qk_norm=qk_norm, q_scale=q_scale)
    return pl.pallas_call(
        kern,
        grid=(n // tm,),
        in_specs=[
            pl.BlockSpec((tm, d), lambda i: (i, 0)),
            pl.BlockSpec((d, nq + nk + nv), lambda i: (0, 0)),
            pl.BlockSpec((tm, LANES), lambda i: (i % tiles_per_seq, 0)),
            pl.BlockSpec((tm, LANES), lambda i: (i % tiles_per_seq, 0)),
            pl.BlockSpec((2, LANES), lambda i: (0, 0)),
            pl.BlockSpec((LANES, LANES), lambda i: (0, 0)),
        ],
        out_specs=[
            pl.BlockSpec((tm, nq), lambda i: (i, 0)),
            pl.BlockSpec((tm, nk), lambda i: (i, 0)),
            pl.BlockSpec((tm, nv), lambda i: (i, 0)),
        ],
        out_shape=[
            jax.ShapeDtypeStruct((n, nq), BF16),
            jax.ShapeDtypeStruct((n, nk), BF16),
            jax.ShapeDtypeStruct((n, nv), BF16),
        ],
        compiler_params=_params("parallel"),
        name="qkv_project",
    )(x2, w, cos, sin, gains, ones_bd)


def _banded_kernel(sink_ref, q_ref, kp_ref, kc_ref, kn_ref, vp_ref, vc_ref, vn_ref, *out_refs,
                   tq, w, seq, n_q_heads, n_kv_heads, with_lse):
    o_ref = out_refs[0]
    j = pl.program_id(1)
    g_per = n_q_heads // n_kv_heads
    span = tq + 2 * w
    row = lax.broadcasted_iota(jnp.int32, (tq, span), 0)
    col = lax.broadcasted_iota(jnp.int32, (tq, span), 1)
    rel = col - w - row
    kpos = j * tq - w + col
    ok = (jnp.abs(rel) <= w) & (kpos >= 0) & (kpos < seq)

    outs, lses = [], []
    for kh in range(n_kv_heads):
        cs = slice(kh * HEAD_DIM, (kh + 1) * HEAD_DIM)
        kcat = jnp.concatenate([kp_ref[0, :, cs], kc_ref[0, :, cs], kn_ref[0, :, cs]], axis=0)
        vcat = jnp.concatenate([vp_ref[0, :, cs], vc_ref[0, :, cs], vn_ref[0, :, cs]], axis=0)
        for g in range(g_per):
            h = kh * g_per + g
            qh = q_ref[0, :, h * HEAD_DIM:(h + 1) * HEAD_DIM]
            s = jnp.where(ok, _dot_nt(qh, kcat), NEG_INF)
            sink = sink_ref[h]
            m = jnp.maximum(jnp.max(s, axis=-1, keepdims=True), sink)
            p = jnp.exp(s - m)
            denom = jnp.sum(p, axis=-1, keepdims=True) + jnp.exp(sink - m)
            o = _dot(p.astype(BF16), vcat) / denom
            outs.append(o)
            if with_lse:
                lses.append(jnp.broadcast_to(m + jnp.log(denom), (tq, HEAD_DIM)))
    o_ref[0] = jnp.concatenate(outs, axis=-1).astype(o_ref.dtype)
    if with_lse:
        out_refs[1][0] = jnp.concatenate(lses, axis=-1)


def _banded_attention(q, k, v, sink, *, batch, seq, dilation, group, n_groups, n_q_heads, n_kv_heads,
                      window, tq, with_lse):
    d = dilation
    sd = seq // d
    qw = n_q_heads * HEAD_DIM
    kw = n_kv_heads * HEAD_DIM
    w = window
    r_blocks = tq // w
    nb_w = sd // w
    q3 = q.reshape(batch, sd, d * n_groups * qw)
    k3 = k.reshape(batch, sd, d * n_groups * kw)
    v3 = v.reshape(batch, sd, d * n_groups * kw)

    def q_map(bc, j):
        return (bc // d, j, (bc % d) * n_groups + group)

    def prev_map(bc, j):
        return (bc // d, jnp.maximum(j * r_blocks - 1, 0), (bc % d) * n_groups + group)

    def next_map(bc, j):
        return (bc // d, jnp.minimum((j + 1) * r_blocks, nb_w - 1), (bc % d) * n_groups + group)

    def o_map(bc, j):
        return (bc // d, j, bc % d)

    kern = functools.partial(_banded_kernel, tq=tq, w=w, seq=sd, n_q_heads=n_q_heads,
                             n_kv_heads=n_kv_heads, with_lse=with_lse)
    out_specs = [pl.BlockSpec((1, tq, qw), o_map)]
    out_shape = [jax.ShapeDtypeStruct((batch, sd, d * qw), BF16)]
    if with_lse:
        out_specs.append(pl.BlockSpec((1, tq, qw), o_map))
        out_shape.append(jax.ShapeDtypeStruct((batch, sd, d * qw), F32))
    res = pl.pallas_call(
        kern,
        grid=(batch * d, sd // tq),
        in_specs=[
            pl.BlockSpec(memory_space=pltpu.SMEM),
            pl.BlockSpec((1, tq, qw), q_map),
            pl.BlockSpec((1, w, kw), prev_map),
            pl.BlockSpec((1, tq, kw), q_map),
            pl.BlockSpec((1, w, kw), next_map),
            pl.BlockSpec((1, w, kw), prev_map),
            pl.BlockSpec((1, tq, kw), q_map),
            pl.BlockSpec((1, w, kw), next_map),
        ],
        out_specs=out_specs,
        out_shape=out_shape,
        compiler_params=_params("parallel", "parallel"),
        name=f"banded_attention_d{d}",
    )(sink, q3, k3, k3, k3, v3, v3, v3)
    return [r.reshape(batch * seq, qw) for r in res]


def _flash_kernel(q_ref, k_ref, v_ref, o_ref, *, tq, tk, seq, n_kv_heads, g_per):
    ones = jnp.ones((tk, HEAD_DIM), BF16)
    outs = [None] * (n_kv_heads * g_per)
    for kh in range(n_kv_heads):
        cs = slice(kh * HEAD_DIM, (kh + 1) * HEAD_DIM)
        qs = jnp.concatenate(
            [q_ref[0, :, (kh * g_per + g) * HEAD_DIM:(kh * g_per + g + 1) * HEAD_DIM] for g in range(g_per)],
            axis=0)

        def body(c, carry):
            m, acc = carry
            start = pl.multiple_of(c * tk, tk)
            kc = k_ref[0, pl.ds(start, tk), cs]
            vc = v_ref[0, pl.ds(start, tk), cs]
            vext = jnp.concatenate([vc, ones], axis=-1)
            s = _dot_nt(qs, kc)
            m_new = jnp.maximum(m, jnp.max(s, axis=-1, keepdims=True))
            alpha = jnp.exp(m - m_new)
            p = jnp.exp(s - m_new)
            acc = alpha * acc + _dot(p.astype(BF16), vext)
            return m_new, acc

        m0 = jnp.full((g_per * tq, 1), NEG_INF, F32)
        acc0 = jnp.zeros((g_per * tq, 2 * HEAD_DIM), F32)
        _, acc = lax.fori_loop(0, seq // tk, body, (m0, acc0))
        o = acc[:, :HEAD_DIM] / acc[:, HEAD_DIM:]
        for g in range(g_per):
            outs[kh * g_per + g] = o[g * tq:(g + 1) * tq]
    o_ref[0] = jnp.concatenate(outs, axis=-1).astype(o_ref.dtype)


def _flash_attention(q, k, v, *, batch, seq, n_q_heads, n_kv_heads, tq, tk):
    qw = n_q_heads * HEAD_DIM
    kw = n_kv_heads * HEAD_DIM
    kern = functools.partial(_flash_kernel, tq=tq, tk=tk, seq=seq, n_kv_heads=n_kv_heads,
                             g_per=n_q_heads // n_kv_heads)
    out = pl.pallas_call(
        kern,
        grid=(batch, seq // tq),
        in_specs=[
            pl.BlockSpec((1, tq, qw), lambda b, j: (b, j, 0)),
            pl.BlockSpec((1, seq, kw), lambda b, j: (b, 0, 0)),
            pl.BlockSpec((1, seq, kw), lambda b, j: (b, 0, 0)),
        ],
        out_specs=pl.BlockSpec((1, tq, qw), lambda b, j: (b, j, 0)),
        out_shape=jax.ShapeDtypeStruct((batch, seq, qw), BF16),
        compiler_params=_params("parallel", "arbitrary"),
        name="dense_attention",
    )(q.reshape(batch, seq, qw), k.reshape(batch, seq, kw), v.reshape(batch, seq, kw))
    return out.reshape(batch * seq, qw)


def _outproj_kernel(o_ref, w_ref, x_ref, g_ref, b_ref, y_ref):
    h = _dot(o_ref[...], w_ref[...])
    y_ref[...] = _layer_norm(DEEPNORM_ALPHA * x_ref[...] + h, g_ref[...], b_ref[...])


def _merge_outproj_kernel(o0_ref, o1_ref, o2_ref, l0_ref, l1_ref, l2_ref, w_ref, x_ref, g_ref, b_ref, y_ref):
    l0, l1, l2 = l0_ref[...], l1_ref[...], l2_ref[...]
    m = jnp.maximum(jnp.maximum(l0, l1), l2)
    e0, e1, e2 = jnp.exp(l0 - m), jnp.exp(l1 - m), jnp.exp(l2 - m)
    tot = e0 + e1 + e2
    o = (e0 * o0_ref[...].astype(F32) + e1 * o1_ref[...].astype(F32) + e2 * o2_ref[...].astype(F32)) / tot
    h = _dot(o.astype(BF16), w_ref[...])
    y_ref[...] = _layer_norm(DEEPNORM_ALPHA * x_ref[...] + h, g_ref[...], b_ref[...])


def _outproj_ln(os_, lses, w, x2, g, b):
    n, d = x2.shape
    tm = TOKEN_TILE
    kdim = w.shape[0]
    row = lambda i: (i, 0)
    fixed = lambda i: (0, 0)
    acts = list(os_) + list(lses)
    kern = _merge_outproj_kernel if lses else _outproj_kernel
    return pl.pallas_call(
        kern,
        grid=(n // tm,),
        in_specs=[pl.BlockSpec((tm, kdim), row) for _ in acts] + [
            pl.BlockSpec((kdim, d), fixed),
            pl.BlockSpec((tm, d), row),
            pl.BlockSpec((1, d), fixed),
            pl.BlockSpec((1, d), fixed),
        ],
        out_specs=pl.BlockSpec((tm, d), row),
        out_shape=jax.ShapeDtypeStruct((n, d), F32),
        compiler_params=_params("parallel"),
        name="outproj_ln",
    )(*acts, w, x2, g, b)


def _matmul_kernel(x_ref, w_ref, o_ref):
    o_ref[...] = _dot(x_ref[...].astype(BF16), w_ref[...]).astype(o_ref.dtype)


def _matmul(x2, w, out_dtype, tm):
    n, kdim = x2.shape
    cols = w.shape[1]
    return pl.pallas_call(
        _matmul_kernel,
        grid=(n // tm,),
        in_specs=[pl.BlockSpec((tm, kdim), lambda i: (i, 0)), pl.BlockSpec((kdim, cols), lambda i: (0, 0))],
        out_specs=pl.BlockSpec((tm, cols), lambda i: (i, 0)),
        out_shape=jax.ShapeDtypeStruct((n, cols), out_dtype),
        compiler_params=_params("parallel"),
        name="matmul",
    )(x2, w)


def _memattn_kernel(x_ref, wq_ref, kv_ref, wo_ref, g_ref, b_ref, y_ref):
    x = x_ref[0]
    q = (_dot(x.astype(BF16), wq_ref[...]) * (MEM_HEAD_DIM ** -0.5)).astype(BF16)
    outs = []
    for h in range(MEM_HEADS):
        cs = slice(h * MEM_HEAD_DIM, (h + 1) * MEM_HEAD_DIM)
        kh = kv_ref[0, :, cs]
        vh = kv_ref[0, :, D_MODEL + h * MEM_HEAD_DIM:D_MODEL + (h + 1) * MEM_HEAD_DIM]
        s = _dot_nt(q[:, cs], kh)
        m = jnp.max(s, axis=-1, keepdims=True)
        p = jnp.exp(s - m)
        denom = jnp.sum(p, axis=-1, keepdims=True)
        outs.append(_dot(p.astype(BF16), vh) / denom)
    o = jnp.concatenate(outs, axis=-1).astype(BF16)
    h_out = _dot(o, wo_ref[...])
    y_ref[0] = _layer_norm(DEEPNORM_ALPHA * x + h_out, g_ref[...], b_ref[...])


def _memory_attention(x3, kv, wq, wo, g, b):
    batch, seq, d = x3.shape
    n_mem = kv.shape[1]
    tm = TOKEN_TILE
    return pl.pallas_call(
        _memattn_kernel,
        grid=(batch, seq // tm),
        in_specs=[
            pl.BlockSpec((1, tm, d), lambda bi, i: (bi, i, 0)),
            pl.BlockSpec((d, d), lambda bi, i: (0, 0)),
            pl.BlockSpec((1, n_mem, 2 * d), lambda bi, i: (bi, 0, 0)),
            pl.BlockSpec((d, d), lambda bi, i: (0, 0)),
            pl.BlockSpec((1, d), lambda bi, i: (0, 0)),
            pl.BlockSpec((1, d), lambda bi, i: (0, 0)),
        ],
        out_specs=pl.BlockSpec((1, tm, d), lambda bi, i: (bi, i, 0)),
        out_shape=jax.ShapeDtypeStruct((batch, seq, d), F32),
        compiler_params=_params("parallel", "parallel"),
        name="memory_attention",
    )(x3, wq, kv, wo, g, b)


def _router_kernel(x_ref, wh_ref, wl_ref, bias_ref, cls_ref, gate_ref):
    x = x_ref[...]
    xh = x.astype(BF16)
    xl = (x - xh.astype(F32)).astype(BF16)
    logits = _dot(xh, wh_ref[...]) + _dot(xl, wh_ref[...]) + _dot(xh, wl_ref[...])
    lt = logits.T[:N_EXPERTS]
    scores = 1.0 / (1.0 + jnp.exp(-lt))
    biased = scores + bias_ref[...]
    ng = N_EXPERT_GROUPS
    a = [biased[l * ng:(l + 1) * ng] for l in range(EXPERTS_PER_GROUP)]
    sc = [scores[l * ng:(l + 1) * ng] for l in range(EXPERTS_PER_GROUP)]
    gs = None
    for i in range(EXPERTS_PER_GROUP):
        for k in range(i + 1, EXPERTS_PER_GROUP):
            pair = a[i] + a[k]
            gs = pair if gs is None else jnp.maximum(gs, pair)
    gidx = lax.broadcasted_iota(jnp.int32, gs.shape, 0).astype(F32)
    gmax = jnp.max(gs, axis=0, keepdims=True)
    grp_f = jnp.min(jnp.where(gs == gmax, gidx, float(ng)), axis=0, keepdims=True)
    sel = gidx == grp_f
    grp = grp_f.astype(jnp.int32)
    v = [jnp.sum(jnp.where(sel, a[l], 0.0), axis=0, keepdims=True) for l in range(EXPERTS_PER_GROUP)]
    sv = [jnp.sum(jnp.where(sel, sc[l], 0.0), axis=0, keepdims=True) for l in range(EXPERTS_PER_GROUP)]

    def first_argmax(vals, excluded):
        best = None
        for l, val in enumerate(vals):
            cand = val if excluded is None else jnp.where(excluded == l, NEG_INF, val)
            best = cand if best is None else jnp.maximum(best, cand)
        idx = jnp.full(best.shape, EXPERTS_PER_GROUP - 1, jnp.int32)
        for l in range(EXPERTS_PER_GROUP - 1, -1, -1):
            hit = vals[l] == best
            if excluded is not None:
                hit = hit & (excluded != l)
            idx = jnp.where(hit, l, idx)
        return idx

    i1 = first_argmax(v, None)
    i2 = first_argmax(v, i1)

    def pick(vals, idx):
        out = vals[0]
        for l in range(1, EXPERTS_PER_GROUP):
            out = jnp.where(idx == l, vals[l], out)
        return out

    s1 = pick(sv, i1)
    s2 = pick(sv, i2)
    tot = s1 + s2
    lo = jnp.minimum(i1, i2)
    hi = jnp.maximum(i1, i2)
    first_is_lo = i1 < i2
    g_lo = jnp.where(first_is_lo, s1, s2) / tot
    g_hi = jnp.where(first_is_lo, s2, s1) / tot
    pair_idx = jnp.right_shift(lo * (7 - lo), 1) + (hi - lo - 1)
    cls_ref[0] = grp * N_PAIRS + pair_idx
    gate_ref[0] = jnp.concatenate([g_lo, g_hi], axis=0)


def _router(x2, wh, wl, bias_t):
    n, d = x2.shape
    tm = TOKEN_TILE
    nt = n // tm
    cls, gates = pl.pallas_call(
        _router_kernel,
        grid=(nt,),
        in_specs=[
            pl.BlockSpec((tm, d), lambda i: (i, 0)),
            pl.BlockSpec((d, LANES), lambda i: (0, 0)),
            pl.BlockSpec((d, LANES), lambda i: (0, 0)),
            pl.BlockSpec((N_EXPERTS, tm), lambda i: (0, 0)),
        ],
        out_specs=[
            pl.BlockSpec((1, 1, tm), lambda i: (i, 0, 0)),
            pl.BlockSpec((1, 2, tm), lambda i: (i, 0, 0)),
        ],
        out_shape=[
            jax.ShapeDtypeStruct((nt, 1, tm), jnp.int32),
            jax.ShapeDtypeStruct((nt, 2, tm), F32),
        ],
        compiler_params=_params("parallel"),
        name="router",
    )(x2, wh, wl, bias_t)
    return cls.reshape(n), gates[:, 0, :].reshape(n), gates[:, 1, :].reshape(n)


def _moe_kernel(tok_ref, lo_ref, hi_ref, nvalid_ref, nblk_ref,
                x_hbm, gate_ref, gu_lo_ref, gu_hi_ref, dn_lo_ref, dn_hi_ref, g_ref, b_ref,
                y_hbm, xbuf, ybuf, gsem, ssem):
    blk = pl.program_id(0)
    n_used = nblk_ref[0]
    rows = MOE_BLOCK

    def gather_copy(block, i, slot):
        tok = jnp.maximum(tok_ref[block * rows + i], 0)
        return pltpu.make_async_copy(x_hbm.at[pl.ds(tok, 1)], xbuf.at[slot, pl.ds(i, 1)], gsem.at[slot])

    def scatter_copy(block, i, slot):
        tok = jnp.maximum(tok_ref[block * rows + i], 0)
        return pltpu.make_async_copy(ybuf.at[slot, pl.ds(i, 1)], y_hbm.at[pl.ds(tok, 1)], ssem.at[slot])

    def start_gather(block, slot):
        def body(i, c):
            gather_copy(block, i, slot).start()
            return c
        lax.fori_loop(0, rows, body, 0)

    def wait_gather(block, slot):
        def body(i, c):
            gather_copy(block, i, slot).wait()
            return c
        lax.fori_loop(0, rows, body, 0)

    def start_scatter(block, slot):
        def body(i, c):
            scatter_copy(block, i, slot).start()
            return c
        lax.fori_loop(0, nvalid_ref[block], body, 0)

    def wait_scatter(block, slot):
        def body(i, c):
            scatter_copy(block, i, slot).wait()
            return c
        lax.fori_loop(0, nvalid_ref[block], body, 0)

    @pl.when(blk < n_used)
    def _():
        slot = blk % 2

        @pl.when(blk == 0)
        def _():
            start_gather(0, 0)

        @pl.when(blk + 1 < n_used)
        def _():
            start_gather(blk + 1, 1 - slot)

        wait_gather(blk, slot)

        @pl.when(blk >= 2)
        def _():
            wait_scatter(blk - 2, slot)

        x = xbuf[slot]
        xb = x.astype(BF16)

        def expert(gu_ref, dn_ref):
            gu = _dot(xb, gu_ref[0])
            gate_act = gu[:, :D_EXPERT]
            act = gate_act / (1.0 + jnp.exp(-gate_act)) * gu[:, D_EXPERT:]
            return _dot(act.astype(BF16), dn_ref[0])

        r_i = lax.broadcasted_iota(jnp.int32, (rows, rows), 0)
        c_i = lax.broadcasted_iota(jnp.int32, (rows, rows), 1)
        diag = r_i == c_i
        gates = gate_ref[0]
        g_lo = jnp.sum(jnp.where(diag, gates[0:1, :], 0.0), axis=1, keepdims=True)
        g_hi = jnp.sum(jnp.where(diag, gates[1:2, :], 0.0), axis=1, keepdims=True)
        h = g_lo * expert(gu_lo_ref, dn_lo_ref) + g_hi * expert(gu_hi_ref, dn_hi_ref)
        ybuf[slot] = _layer_norm(DEEPNORM_ALPHA * x + h, g_ref[...], b_ref[...])
        start_scatter(blk, slot)

        @pl.when(blk == n_used - 1)
        def _():
            @pl.when(blk >= 1)
            def _():
                wait_scatter(blk - 1, 1 - slot)
            wait_scatter(blk, slot)


def _moe_experts(x2, slot_tok, blk_lo, blk_hi, blk_nvalid, n_used, slot_gates, w_gu, w_dn, g, b):
    n, d = x2.shape
    n_blocks = blk_lo.shape[0]
    rows = MOE_BLOCK
    grid_spec = pltpu.PrefetchScalarGridSpec(
        num_scalar_prefetch=5,
        grid=(n_blocks,),
        in_specs=[
            pl.BlockSpec(memory_space=pl.ANY),
            pl.BlockSpec((1, 2, rows), lambda i, *_: (i, 0, 0)),
            pl.BlockSpec((1, d, 2 * D_EXPERT), lambda i, tok, lo, hi, nv, nb: (lo[i], 0, 0)),
            pl.BlockSpec((1, d, 2 * D_EXPERT), lambda i, tok, lo, hi, nv, nb: (hi[i], 0, 0)),
            pl.BlockSpec((1, D_EXPERT, d), lambda i, tok, lo, hi, nv, nb: (lo[i], 0, 0)),
            pl.BlockSpec((1, D_EXPERT, d), lambda i, tok, lo, hi, nv, nb: (hi[i], 0, 0)),
            pl.BlockSpec((1, d), lambda i, *_: (0, 0)),
            pl.BlockSpec((1, d), lambda i, *_: (0, 0)),
        ],
        out_specs=pl.BlockSpec(memory_space=pl.ANY),
        scratch_shapes=[
            pltpu.VMEM((2, rows, d), F32),
            pltpu.VMEM((2, rows, d), F32),
            pltpu.SemaphoreType.DMA((2,)),
            pltpu.SemaphoreType.DMA((2,)),
        ],
    )
    return pl.pallas_call(
        _moe_kernel,
        grid_spec=grid_spec,
        out_shape=jax.ShapeDtypeStruct((n, d), F32),
        compiler_params=_params("arbitrary"),
        name="moe_experts",
    )(slot_tok, blk_lo, blk_hi, blk_nvalid, n_used, x2, slot_gates, w_gu, w_gu, w_dn, w_dn, g, b)


_PAIR_LO = np.array([0, 0, 0, 1, 1, 2], np.int32)
_PAIR_HI = np.array([1, 2, 3, 2, 3, 3], np.int32)


def _dispatch_tables(cls, g_lo, g_hi):
    n = cls.shape[0]
    rows = MOE_BLOCK
    n_blocks = n // rows + N_CLASSES
    n_slots = n_blocks * rows
    order = jnp.argsort(cls).astype(jnp.int32)
    counts = jnp.bincount(cls, length=N_CLASSES).astype(jnp.int32)
    padded = ((counts + rows - 1) // rows) * rows
    pad_end = jnp.cumsum(padded)
    pad_start = pad_end - padded
    start = jnp.cumsum(counts) - counts
    slot = jnp.arange(n_slots, dtype=jnp.int32)
    slot_cls = jnp.minimum(jnp.searchsorted(pad_end, slot, side='right'), N_CLASSES - 1).astype(jnp.int32)
    rank = slot - pad_start[slot_cls]
    valid = (rank < counts[slot_cls]) & (slot < pad_end[-1])
    src = jnp.clip(start[slot_cls] + rank, 0, n - 1)
    slot_tok = jnp.where(valid, order[src], -1).astype(jnp.int32)
    tok_safe = jnp.maximum(slot_tok, 0)
    slot_gates = jnp.stack([jnp.where(valid, g_lo[tok_safe], 0.0), jnp.where(valid, g_hi[tok_safe], 0.0)], axis=0)
    slot_gates = slot_gates.reshape(2, n_blocks, rows).transpose(1, 0, 2)
    n_used = (pad_end[-1] // rows).astype(jnp.int32)
    blk = jnp.arange(n_blocks, dtype=jnp.int32)
    blk_cls = slot_cls[jnp.minimum(blk, jnp.maximum(n_used - 1, 0)) * rows]
    blk_nvalid = jnp.sum(valid.reshape(n_blocks, rows), axis=1).astype(jnp.int32)
    grp = blk_cls // N_PAIRS
    pair = blk_cls % N_PAIRS
    blk_lo = grp * EXPERTS_PER_GROUP + jnp.asarray(_PAIR_LO)[pair]
    blk_hi = grp * EXPERTS_PER_GROUP + jnp.asarray(_PAIR_HI)[pair]
    return slot_tok, blk_lo, blk_hi, blk_nvalid, n_used.reshape(1), slot_gates


def _tile_heads(t64):
    return jnp.concatenate([t64] * (LANES // HEAD_DIM), axis=-1)


def _rope_tables(seq):
    t = jnp.arange(seq, dtype=jnp.int32)

    def tables(pos, dim):
        inv_freq = ROPE_THETA ** (-jnp.arange(0, dim, 2, dtype=F32) / dim)
        ang = pos.astype(F32)[:, None] * inv_freq[None, :]
        return jnp.cos(ang), jnp.sin(ang)

    c1, s1 = tables(t, HEAD_DIM)
    cos_1d = _tile_heads(jnp.concatenate([c1, c1], axis=-1))
    sin_1d = _tile_heads(jnp.concatenate([-s1, s1], axis=-1))
    cr, sr = tables(t // GRID_W, HEAD_DIM // 2)
    cc, sc = tables(t % GRID_W, HEAD_DIM // 2)
    cos_ax = _tile_heads(jnp.concatenate([cr, cr, cc, cc], axis=-1))
    sin_ax = _tile_heads(jnp.concatenate([-sr, sr, -sc, sc], axis=-1))
    return (cos_1d, sin_1d), (cos_ax, sin_ax)


def _block_diag_ones():
    i = np.arange(LANES)
    return jnp.asarray((i[:, None] // HEAD_DIM) == (i[None, :] // HEAD_DIM), BF16)


def kernel(x_prompt, x_sample, mem_prompt, mem_sample, a_w_in, a_sink, a_w_out, b_w_in, b_q_norm, b_k_norm,
           b_w_out, c_w_in, c_w_out, m_w_q, m_w_kv, m_w_out, ln_g, ln_b, router_w, router_bias, e_w_gu, e_w_down):
    seq = x_prompt.shape[1]
    assert x_sample.shape[1] == seq
    x = jnp.concatenate([x_prompt, x_sample], axis=0)
    mem = jnp.concatenate([mem_prompt, mem_sample], axis=0)
    batch = x.shape[0]
    n = batch * seq
    n_mem = mem.shape[1]
    x2 = x.reshape(n, D_MODEL)

    rope_1d, rope_ax = _rope_tables(seq)
    ones_bd = _block_diag_ones()
    unit_gains = jnp.ones((2, LANES), F32)
    scale = HEAD_DIM ** -0.5

    rw = router_w.astype(F32).reshape(D_MODEL, N_EXPERT_GROUPS, EXPERTS_PER_GROUP).transpose(0, 2, 1)
    rw = jnp.pad(rw.reshape(D_MODEL, N_EXPERTS), ((0, 0), (0, LANES - N_EXPERTS)))
    rw_hi = rw.astype(BF16)
    rw_lo = (rw - rw_hi.astype(F32)).astype(BF16)
    rb = router_bias.astype(F32).reshape(N_EXPERT_GROUPS, EXPERTS_PER_GROUP).T.reshape(N_EXPERTS, 1)
    rb_t = jnp.broadcast_to(rb, (N_EXPERTS, TOKEN_TILE))

    cq, ck = C_HEADS * HEAD_DIM, C_KV_HEADS * HEAD_DIM
    n_cg = len(C_PATTERNS)

    for i in range(DEPTH):
        kind, j = i % N_MIXERS, i // N_MIXERS
        lg = lambda s_: ln_g[i, s_].astype(F32).reshape(1, D_MODEL)
        lb = lambda s_: ln_b[i, s_].astype(F32).reshape(1, D_MODEL)
        if kind == 0:
            nq, nk = A_HEADS * HEAD_DIM, A_KV_HEADS * HEAD_DIM
            q, k, v = _qkv_project(x2, a_w_in[j].astype(BF16), rope_1d[0], rope_1d[1], unit_gains, ones_bd,
                                   nq=nq, nk=nk, nv=nk, half=HEAD_DIM // 2, qk_norm=False, q_scale=scale, seq=seq)
            o, = _banded_attention(q, k, v, a_sink[j].astype(F32), batch=batch, seq=seq, dilation=1, group=0,
                                   n_groups=1, n_q_heads=A_HEADS, n_kv_heads=A_KV_HEADS, window=A_WINDOW,
                                   tq=A_WINDOW, with_lse=False)
            x2 = _outproj_ln([o], [], a_w_out[j].astype(BF16), x2, lg(0), lb(0))
        elif kind == 1:
            nq, nk = B_HEADS * HEAD_DIM, B_KV_HEADS * HEAD_DIM
            gains = jnp.stack([_tile_heads(b_q_norm[j].astype(F32)), _tile_heads(b_k_norm[j].astype(F32))], axis=0)
            q, k, v = _qkv_project(x2, b_w_in[j].astype(BF16), rope_ax[0], rope_ax[1], gains, ones_bd,
                                   nq=nq, nk=nk, nv=nk, half=HEAD_DIM // 4, qk_norm=True, q_scale=scale, seq=seq)
            o = _flash_attention(q, k, v, batch=batch, seq=seq, n_q_heads=B_HEADS, n_kv_heads=B_KV_HEADS,
                                 tq=256, tk=1024)
            x2 = _outproj_ln([o], [], b_w_out[j].astype(BF16), x2, lg(0), lb(0))
        else:
            wc = c_w_in[j].reshape(D_MODEL, n_cg, cq + 2 * ck)
            wc = jnp.concatenate([wc[:, :, :cq].reshape(D_MODEL, n_cg * cq),
                                  wc[:, :, cq:cq + ck].reshape(D_MODEL, n_cg * ck),
                                  wc[:, :, cq + ck:].reshape(D_MODEL, n_cg * ck)], axis=1).astype(BF16)
            q, k, v = _qkv_project(x2, wc, rope_1d[0], rope_1d[1], unit_gains, ones_bd,
                                   nq=n_cg * cq, nk=n_cg * ck, nv=n_cg * ck, half=HEAD_DIM // 2, qk_norm=False,
                                   q_scale=scale, seq=seq)
            no_sink = jnp.full((C_HEADS,), NEG_INF, F32)
            os_, lses = [], []
            for gi, (window, dilation) in enumerate(C_PATTERNS):
                half_w = (window // 2) // dilation
                o, lse = _banded_attention(q, k, v, no_sink, batch=batch, seq=seq, dilation=dilation, group=gi,
                                           n_groups=n_cg, n_q_heads=C_HEADS, n_kv_heads=C_KV_HEADS,
                                           window=half_w, tq=2 * half_w, with_lse=True)
                os_.append(o)
                lses.append(lse)
            x2 = _outproj_ln(os_, lses, c_w_out[j].astype(BF16), x2, lg(0), lb(0))

        kv = _matmul(mem.reshape(batch * n_mem, D_MODEL), m_w_kv[i].astype(BF16), BF16, tm=n_mem)
        x2 = _memory_attention(x2.reshape(batch, seq, D_MODEL), kv.reshape(batch, n_mem, 2 * D_MODEL),
                               m_w_q[i].astype(BF16), m_w_out[i].astype(BF16), lg(1), lb(1)).reshape(n, D_MODEL)

        cls, g_lo, g_hi = _router(x2, rw_hi, rw_lo, rb_t)
        slot_tok, blk_lo, blk_hi, blk_nvalid, n_used, slot_gates = _dispatch_tables(cls, g_lo, g_hi)
        x2 = _moe_experts(x2, slot_tok, blk_lo, blk_hi, blk_nvalid, n_used, slot_gates,
                          e_w_gu[i].astype(BF16), e_w_down[i].astype(BF16), lg(2), lb(2))

    y = x2.reshape(batch, seq, D_MODEL)
    nb_prompt = x_prompt.shape[0]
    return (y[:nb_prompt], y[nb_prompt:])
```

```python
import functools

import jax
import jax.numpy as jnp
import numpy as np
from jax import lax
from jax.experimental import pallas as pl
from jax.experimental.pallas import tpu as pltpu

F32 = jnp.float32
BF16 = jnp.bfloat16

D_MODEL = 1024
DEPTH = 4
N_MIXERS = 3
HEAD_DIM = 64
ROPE_THETA = 10000.0
GRID_W = 64

A_HEADS, A_KV_HEADS, A_WINDOW = 16, 4, 128
B_HEADS, B_KV_HEADS = 16, 4
C_PATTERNS = ((128, 1), (512, 4), (2048, 16))
C_HEADS, C_KV_HEADS = 8, 2

MEM_HEADS = 4
MEM_HEAD_DIM = D_MODEL // MEM_HEADS

N_EXPERTS = 32
N_EXPERT_GROUPS = 8
EXPERTS_PER_GROUP = 4
D_EXPERT = D_MODEL // 4
N_PAIRS = 6
N_CLASSES = N_EXPERT_GROUPS * N_PAIRS
MOE_BLOCK = 128

LN_EPS = 1e-5
QK_NORM_EPS = 1e-6
DEEPNORM_ALPHA = (2 * DEPTH) ** 0.25

LANES = 128
TOKEN_TILE = 512
VMEM_LIMIT_BYTES = 48 * 1024 * 1024

NEG_INF = float("-inf")
LOG2_E = 1.4426950408889634


def _params(*semantics):
    return pltpu.CompilerParams(dimension_semantics=semantics, vmem_limit_bytes=VMEM_LIMIT_BYTES)


def _layer_norm(z, g, b):
    mu = jnp.mean(z, axis=-1, keepdims=True)
    zc = z - mu
    var = jnp.mean(zc * zc, axis=-1, keepdims=True)
    return zc * lax.rsqrt(var + LN_EPS) * g + b


def _dot(a, b):
    return jnp.dot(a, b, preferred_element_type=F32)


def _dot_nt(a, b):
    return lax.dot_general(a, b, (((1,), (1,)), ((), ())), preferred_element_type=F32)


def _qkv_kernel(x_ref, w_ref, cos_ref, sin_ref, gain_ref, ones_ref, q_ref, k_ref, v_ref, *,
                nq, nk, half, qk_norm, q_scale, v_transposed):
    acc = _dot(x_ref[...].astype(BF16), w_ref[...])
    cos = cos_ref[...]
    sin = sin_ref[...]
    lane = lax.broadcasted_iota(jnp.int32, cos.shape, 1)
    first = (lane % (2 * half)) < half

    def rotate(a, gain, scale):
        if qk_norm:
            a2 = a * a
            hi = a2.astype(BF16)
            lo = (a2 - hi.astype(F32)).astype(BF16)
            ms = (_dot(hi, ones_ref[...]) + _dot(lo, ones_ref[...])) * (1.0 / HEAD_DIM)
            a = a * lax.rsqrt(ms + QK_NORM_EPS) * gain
        partner = jnp.where(first, pltpu.roll(a, LANES - half, 1), pltpu.roll(a, half, 1))
        r = a * cos + partner * sin
        if scale != 1.0:
            r = r * scale
        return r.astype(BF16)

    gq = gain_ref[0:1, :]
    gk = gain_ref[1:2, :]
    for c in range(nq // LANES):
        q_ref[:, c * LANES:(c + 1) * LANES] = rotate(acc[:, c * LANES:(c + 1) * LANES], gq, q_scale)
    for c in range(nk // LANES):
        k_ref[:, c * LANES:(c + 1) * LANES] = rotate(acc[:, nq + c * LANES:nq + (c + 1) * LANES], gk, 1.0)
    if v_transposed:
        v_ref[0] = acc[:, nq + nk:].T.astype(BF16)
    else:
        v_ref[...] = acc[:, nq + nk:].astype(BF16)


def _qkv_project(x2, w, cos, sin, gains, ones_bd, *, n, nq, nk, nv, half, qk_norm, q_scale, seq,
                 v_transposed=False):
    d = x2.shape[1]
    tm = TOKEN_TILE
    tiles_per_seq = seq // tm
    kern = functools.partial(_qkv_kernel, nq=nq, nk=nk, half=half, qk_norm=qk_norm, q_scale=q_scale,
                             v_transposed=v_transposed)
    if v_transposed:
        v_spec = pl.BlockSpec((1, nv, tm), lambda i: (i, 0, 0))
        v_shape = jax.ShapeDtypeStruct((n // tm, nv, tm), BF16)
    else:
        v_spec = pl.BlockSpec((tm, nv), lambda i: (i, 0))
        v_shape = jax.ShapeDtypeStruct((n, nv), BF16)
    return pl.pallas_call(
        kern,
        grid=(n // tm,),
        in_specs=[
            pl.BlockSpec((tm, d), lambda i: (i, 0)),
            pl.BlockSpec((d, nq + nk + nv), lambda i: (0, 0)),
            pl.BlockSpec((tm, LANES), lambda i: (i % tiles_per_seq, 0)),
            pl.BlockSpec((tm, LANES), lambda i: (i % tiles_per_seq, 0)),
            pl.BlockSpec((2, LANES), lambda i: (0, 0)),
            pl.BlockSpec((LANES, LANES), lambda i: (0, 0)),
        ],
        out_specs=[
            pl.BlockSpec((tm, nq), lambda i: (i, 0)),
            pl.BlockSpec((tm, nk), lambda i: (i, 0)),
            v_spec,
        ],
        out_shape=[
            jax.ShapeDtypeStruct((n, nq), BF16),
            jax.ShapeDtypeStruct((n, nk), BF16),
            v_shape,
        ],
        compiler_params=_params("parallel"),
        name="qkv_project",
    )(x2, w, cos, sin, gains, ones_bd)


def _banded_kernel(sink_ref, q_ref, kp_ref, kc_ref, kn_ref, vp_ref, vc_ref, vn_ref, *out_refs,
                   tq, w, seq, n_q_heads, n_kv_heads, with_lse):
    o_ref = out_refs[0]
    j = pl.program_id(1)
    g_per = n_q_heads // n_kv_heads
    span = tq + 2 * w
    row = lax.broadcasted_iota(jnp.int32, (tq, span), 0)
    col = lax.broadcasted_iota(jnp.int32, (tq, span), 1)
    rel = col - w - row
    kpos = j * tq - w + col
    ok = (jnp.abs(rel) <= w) & (kpos >= 0) & (kpos < seq)

    outs, lses = [], []
    for kh in range(n_kv_heads):
        cs = slice(kh * HEAD_DIM, (kh + 1) * HEAD_DIM)
        kcat = jnp.concatenate([kp_ref[0, :, cs], kc_ref[0, :, cs], kn_ref[0, :, cs]], axis=0)
        vcat = jnp.concatenate([vp_ref[0, :, cs], vc_ref[0, :, cs], vn_ref[0, :, cs]], axis=0)
        for g in range(g_per):
            h = kh * g_per + g
            qh = q_ref[0, :, h * HEAD_DIM:(h + 1) * HEAD_DIM]
            s = jnp.where(ok, _dot_nt(qh, kcat), NEG_INF)
            sink = sink_ref[h]
            m = jnp.maximum(jnp.max(s, axis=-1, keepdims=True), sink)
            p = jnp.exp(s - m)
            denom = jnp.sum(p, axis=-1, keepdims=True) + jnp.exp(sink - m)
            o = _dot(p.astype(BF16), vcat) / denom
            outs.append(o)
            if with_lse:
                lses.append(jnp.broadcast_to(m + jnp.log(denom), (tq, HEAD_DIM)))
    o_ref[0] = jnp.concatenate(outs, axis=-1).astype(o_ref.dtype)
    if with_lse:
        out_refs[1][0] = jnp.concatenate(lses, axis=-1)


def _banded_attention(q, k, v, sink, *, batch, seq, dilation, group, n_groups, n_q_heads, n_kv_heads,
                      window, tq, with_lse):
    d = dilation
    sd = seq // d
    qw = n_q_heads * HEAD_DIM
    kw = n_kv_heads * HEAD_DIM
    w = window
    r_blocks = tq // w
    nb_w = sd // w
    q3 = q.reshape(batch, sd, d * n_groups * qw)
    k3 = k.reshape(batch, sd, d * n_groups * kw)
    v3 = v.reshape(batch, sd, d * n_groups * kw)

    def q_map(bc, j):
        return (bc // d, j, (bc % d) * n_groups + group)

    def prev_map(bc, j):
        return (bc // d, jnp.maximum(j * r_blocks - 1, 0), (bc % d) * n_groups + group)

    def next_map(bc, j):
        return (bc // d, jnp.minimum((j + 1) * r_blocks, nb_w - 1), (bc % d) * n_groups + group)

    def o_map(bc, j):
        return (bc // d, j, bc % d)

    kern = functools.partial(_banded_kernel, tq=tq, w=w, seq=sd, n_q_heads=n_q_heads,
                             n_kv_heads=n_kv_heads, with_lse=with_lse)
    out_specs = [pl.BlockSpec((1, tq, qw), o_map)]
    out_shape = [jax.ShapeDtypeStruct((batch, sd, d * qw), BF16)]
    if with_lse:
        out_specs.append(pl.BlockSpec((1, tq, qw), o_map))
        out_shape.append(jax.ShapeDtypeStruct((batch, sd, d * qw), F32))
    res = pl.pallas_call(
        kern,
        grid=(batch * d, sd // tq),
        in_specs=[
            pl.BlockSpec(memory_space=pltpu.SMEM),
            pl.BlockSpec((1, tq, qw), q_map),
            pl.BlockSpec((1, w, kw), prev_map),
            pl.BlockSpec((1, tq, kw), q_map),
            pl.BlockSpec((1, w, kw), next_map),
            pl.BlockSpec((1, w, kw), prev_map),
            pl.BlockSpec((1, tq, kw), q_map),
            pl.BlockSpec((1, w, kw), next_map),
        ],
        out_specs=out_specs,
        out_shape=out_shape,
        compiler_params=_params("parallel", "parallel"),
        name=f"banded_attention_d{d}",
    )(sink, q3, k3, k3, k3, v3, v3, v3)
    return [r.reshape(batch * seq, qw) for r in res]


def _flash_kernel(q_ref, k_ref, vt_ref, o_ref, st_a, st_b, *, tq, tk, tv, seq, n_kv_heads, g_per):
    ones = jnp.ones((HEAD_DIM, tk), BF16)
    outs = [None] * (n_kv_heads * g_per)
    for kh in range(n_kv_heads):
        cs = slice(kh * HEAD_DIM, (kh + 1) * HEAD_DIM)
        qs = jnp.concatenate(
            [q_ref[0, :, (kh * g_per + g) * HEAD_DIM:(kh * g_per + g + 1) * HEAD_DIM] for g in range(g_per)],
            axis=0)

        def scores(c, st_ref):
            start = pl.multiple_of(c * tk, tk)
            st_ref[...] = _dot_nt(k_ref[0, pl.ds(start, tk), cs], qs)

        def update(c, st_ref, m, acc):
            vt = jnp.concatenate([vt_ref[c * (tk // tv) + t, cs, :] for t in range(tk // tv)], axis=-1)
            vext = jnp.concatenate([vt, ones], axis=0)
            st = st_ref[...]
            m_new = jnp.maximum(m, jnp.max(st, axis=0, keepdims=True))
            alpha = jnp.exp2(m - m_new)
            p = jnp.exp2(st - m_new).astype(BF16)
            return m_new, alpha * acc + _dot(vext, p)

        def pair(c2, carry, last):
            m, acc = carry
            scores(2 * c2 + 1, st_b)
            m, acc = update(2 * c2, st_a, m, acc)
            if not last:
                scores(2 * c2 + 2, st_a)
            return update(2 * c2 + 1, st_b, m, acc)

        n_pairs = seq // (2 * tk)
        m0 = jnp.full((1, g_per * tq), NEG_INF, F32)
        acc0 = jnp.zeros((2 * HEAD_DIM, g_per * tq), F32)
        scores(0, st_a)
        carry = lax.fori_loop(0, n_pairs - 1, lambda c2, cr: pair(c2, cr, False), (m0, acc0))
        _, acc = pair(n_pairs - 1, carry, True)
        acc = acc.T
        o = acc[:, :HEAD_DIM] / acc[:, HEAD_DIM:]
        for g in range(g_per):
            outs[kh * g_per + g] = o[g * tq:(g + 1) * tq]
    o_ref[0] = jnp.concatenate(outs, axis=-1).astype(o_ref.dtype)


def _flash_attention(q, k, vt, *, batch, seq, n_q_heads, n_kv_heads, tq, tk):
    qw = n_q_heads * HEAD_DIM
    kw = n_kv_heads * HEAD_DIM
    tv = vt.shape[2]
    chunks = seq // tv
    kern = functools.partial(_flash_kernel, tq=tq, tk=tk, tv=tv, seq=seq, n_kv_heads=n_kv_heads,
                             g_per=n_q_heads // n_kv_heads)
    out = pl.pallas_call(
        kern,
        grid=(batch, seq // tq),
        in_specs=[
            pl.BlockSpec((1, tq, qw), lambda b, j: (b, j, 0)),
            pl.BlockSpec((1, seq, kw), lambda b, j: (b, 0, 0)),
            pl.BlockSpec((chunks, kw, tv), lambda b, j: (b, 0, 0)),
        ],
        out_specs=pl.BlockSpec((1, tq, qw), lambda b, j: (b, j, 0)),
        out_shape=jax.ShapeDtypeStruct((batch, seq, qw), BF16),
        scratch_shapes=[pltpu.VMEM((tk, (n_q_heads // n_kv_heads) * tq), F32) for _ in range(2)],
        compiler_params=_params("parallel", "arbitrary"),
        name="dense_attention",
    )(q.reshape(batch, seq, qw), k.reshape(batch, seq, kw), vt)
    return out.reshape(batch * seq, qw)


def _outproj_kernel(o_ref, w_ref, x_ref, g_ref, b_ref, y_ref):
    h = _dot(o_ref[...], w_ref[...])
    y_ref[...] = _layer_norm(DEEPNORM_ALPHA * x_ref[...] + h, g_ref[...], b_ref[...])


def _merge_outproj_kernel(o0_ref, o1_ref, o2_ref, l0_ref, l1_ref, l2_ref, w_ref, x_ref, g_ref, b_ref, y_ref):
    l0, l1, l2 = l0_ref[...], l1_ref[...], l2_ref[...]
    m = jnp.maximum(jnp.maximum(l0, l1), l2)
    e0, e1, e2 = jnp.exp(l0 - m), jnp.exp(l1 - m), jnp.exp(l2 - m)
    tot = e0 + e1 + e2
    o = (e0 * o0_ref[...].astype(F32) + e1 * o1_ref[...].astype(F32) + e2 * o2_ref[...].astype(F32)) / tot
    h = _dot(o.astype(BF16), w_ref[...])
    y_ref[...] = _layer_norm(DEEPNORM_ALPHA * x_ref[...] + h, g_ref[...], b_ref[...])


def _outproj_ln(os_, lses, w, x2, g, b):
    n, d = os_[0].shape[0], x2.shape[1]
    tm = TOKEN_TILE
    kdim = w.shape[0]
    row = lambda i: (i, 0)
    fixed = lambda i: (0, 0)
    acts = list(os_) + list(lses)
    kern = _merge_outproj_kernel if lses else _outproj_kernel
    return pl.pallas_call(
        kern,
        grid=(n // tm,),
        in_specs=[pl.BlockSpec((tm, kdim), row) for _ in acts] + [
            pl.BlockSpec((kdim, d), fixed),
            pl.BlockSpec((tm, d), row),
            pl.BlockSpec((1, d), fixed),
            pl.BlockSpec((1, d), fixed),
        ],
        out_specs=pl.BlockSpec((tm, d), row),
        out_shape=jax.ShapeDtypeStruct((n, d), F32),
        compiler_params=_params("parallel"),
        name="outproj_ln",
    )(*acts, w, x2, g, b)


def _matmul_kernel(x_ref, w_ref, o_ref):
    o_ref[...] = _dot(x_ref[...].astype(BF16), w_ref[...]).astype(o_ref.dtype)


def _matmul(x2, w, out_dtype, tm):
    n, kdim = x2.shape
    cols = w.shape[1]
    return pl.pallas_call(
        _matmul_kernel,
        grid=(n // tm,),
        in_specs=[pl.BlockSpec((tm, kdim), lambda i: (i, 0)), pl.BlockSpec((kdim, cols), lambda i: (0, 0))],
        out_specs=pl.BlockSpec((tm, cols), lambda i: (i, 0)),
        out_shape=jax.ShapeDtypeStruct((n, cols), out_dtype),
        compiler_params=_params("parallel"),
        name="matmul",
    )(x2, w)


def _memattn_kernel(x_ref, wq_ref, kv_ref, wo_ref, g_ref, b_ref, y_ref):
    x = x_ref[...]
    q = (_dot(x.astype(BF16), wq_ref[...]) * (MEM_HEAD_DIM ** -0.5)).astype(BF16)
    outs = []
    for h in range(MEM_HEADS):
        cs = slice(h * MEM_HEAD_DIM, (h + 1) * MEM_HEAD_DIM)
        kh = kv_ref[0, :, cs]
        vh = kv_ref[0, :, D_MODEL + h * MEM_HEAD_DIM:D_MODEL + (h + 1) * MEM_HEAD_DIM]
        s = _dot_nt(q[:, cs], kh)
        m = jnp.max(s, axis=-1, keepdims=True)
        p = jnp.exp(s - m)
        denom = jnp.sum(p, axis=-1, keepdims=True)
        outs.append(_dot(p.astype(BF16), vh) / denom)
    o = jnp.concatenate(outs, axis=-1).astype(BF16)
    h_out = _dot(o, wo_ref[...])
    y_ref[...] = _layer_norm(DEEPNORM_ALPHA * x + h_out, g_ref[...], b_ref[...])


def _memory_attention(x2, kv, wq, wo, g, b, *, seq):
    d = x2.shape[1]
    batch, n_mem = kv.shape[0], kv.shape[1]
    tm = TOKEN_TILE
    tiles = seq // tm
    return pl.pallas_call(
        _memattn_kernel,
        grid=(batch, tiles),
        in_specs=[
            pl.BlockSpec((tm, d), lambda bi, i: (bi * tiles + i, 0)),
            pl.BlockSpec((d, d), lambda bi, i: (0, 0)),
            pl.BlockSpec((1, n_mem, 2 * d), lambda bi, i: (bi, 0, 0)),
            pl.BlockSpec((d, d), lambda bi, i: (0, 0)),
            pl.BlockSpec((1, d), lambda bi, i: (0, 0)),
            pl.BlockSpec((1, d), lambda bi, i: (0, 0)),
        ],
        out_specs=pl.BlockSpec((tm, d), lambda bi, i: (bi * tiles + i, 0)),
        out_shape=jax.ShapeDtypeStruct((batch * seq, d), F32),
        compiler_params=_params("parallel", "parallel"),
        name="memory_attention",
    )(x2, wq, kv, wo, g, b)


def _router_kernel(x_ref, wh_ref, wl_ref, bias_ref, cls_ref, gate_ref):
    x = x_ref[...]
    xh = x.astype(BF16)
    xl = (x - xh.astype(F32)).astype(BF16)
    logits = _dot(xh, wh_ref[...]) + _dot(xl, wh_ref[...]) + _dot(xh, wl_ref[...])
    lt = logits.T[:N_EXPERTS]
    scores = 1.0 / (1.0 + jnp.exp(-lt))
    biased = scores + bias_ref[...]
    ng = N_EXPERT_GROUPS
    a = [biased[l * ng:(l + 1) * ng] for l in range(EXPERTS_PER_GROUP)]
    sc = [scores[l * ng:(l + 1) * ng] for l in range(EXPERTS_PER_GROUP)]
    gs = None
    for i in range(EXPERTS_PER_GROUP):
        for k in range(i + 1, EXPERTS_PER_GROUP):
            pair = a[i] + a[k]
            gs = pair if gs is None else jnp.maximum(gs, pair)
    gidx = lax.broadcasted_iota(jnp.int32, gs.shape, 0).astype(F32)
    gmax = jnp.max(gs, axis=0, keepdims=True)
    grp_f = jnp.min(jnp.where(gs == gmax, gidx, float(ng)), axis=0, keepdims=True)
    sel = gidx == grp_f
    grp = grp_f.astype(jnp.int32)
    v = [jnp.sum(jnp.where(sel, a[l], 0.0), axis=0, keepdims=True) for l in range(EXPERTS_PER_GROUP)]
    sv = [jnp.sum(jnp.where(sel, sc[l], 0.0), axis=0, keepdims=True) for l in range(EXPERTS_PER_GROUP)]

    def first_argmax(vals, excluded):
        best = None
        for l, val in enumerate(vals):
            cand = val if excluded is None else jnp.where(excluded == l, NEG_INF, val)
            best = cand if best is None else jnp.maximum(best, cand)
        idx = jnp.full(best.shape, EXPERTS_PER_GROUP - 1, jnp.int32)
        for l in range(EXPERTS_PER_GROUP - 1, -1, -1):
            hit = vals[l] == best
            if excluded is not None:
                hit = hit & (excluded != l)
            idx = jnp.where(hit, l, idx)
        return idx

    i1 = first_argmax(v, None)
    i2 = first_argmax(v, i1)

    def pick(vals, idx):
        out = vals[0]
        for l in range(1, EXPERTS_PER_GROUP):
            out = jnp.where(idx == l, vals[l], out)
        return out

    s1 = pick(sv, i1)
    s2 = pick(sv, i2)
    tot = s1 + s2
    lo = jnp.minimum(i1, i2)
    hi = jnp.maximum(i1, i2)
    first_is_lo = i1 < i2
    g_lo = jnp.where(first_is_lo, s1, s2) / tot
    g_hi = jnp.where(first_is_lo, s2, s1) / tot
    pair_idx = jnp.right_shift(lo * (7 - lo), 1) + (hi - lo - 1)
    cls_ref[0] = grp * N_PAIRS + pair_idx
    gate_ref[0] = jnp.concatenate([g_lo, g_hi], axis=0)


def _router(x2, wh, wl, bias_t):
    n, d = x2.shape
    tm = TOKEN_TILE
    nt = n // tm
    cls, gates = pl.pallas_call(
        _router_kernel,
        grid=(nt,),
        in_specs=[
            pl.BlockSpec((tm, d), lambda i: (i, 0)),
            pl.BlockSpec((d, LANES), lambda i: (0, 0)),
            pl.BlockSpec((d, LANES), lambda i: (0, 0)),
            pl.BlockSpec((N_EXPERTS, tm), lambda i: (0, 0)),
        ],
        out_specs=[
            pl.BlockSpec((1, 1, tm), lambda i: (i, 0, 0)),
            pl.BlockSpec((1, 2, tm), lambda i: (i, 0, 0)),
        ],
        out_shape=[
            jax.ShapeDtypeStruct((nt, 1, tm), jnp.int32),
            jax.ShapeDtypeStruct((nt, 2, tm), F32),
        ],
        compiler_params=_params("parallel"),
        name="router",
    )(x2, wh, wl, bias_t)
    return cls.reshape(n), gates[:, 0, :].reshape(n), gates[:, 1, :].reshape(n)


def _moe_kernel(tok_ref, lo_ref, hi_ref, nblk_ref,
                x_hbm, gate_ref, gu_lo_ref, gu_hi_ref, dn_lo_ref, dn_hi_ref, g_ref, b_ref,
                y_hbm, xbuf, ybuf, gsem, ssem, *, n_tokens):
    blk = pl.program_id(0)
    n_used = nblk_ref[0]
    rows = MOE_BLOCK

    def gather_copy(block, i, slot):
        tok = jnp.maximum(tok_ref[block * rows + i], 0)
        return pltpu.make_async_copy(x_hbm.at[pl.ds(tok, 1)], xbuf.at[slot, pl.ds(i, 1)], gsem.at[slot])

    def scatter_copy(block, i, slot):
        tok = tok_ref[block * rows + i]
        dst = jnp.where(tok < 0, n_tokens + slot * rows + i, tok)
        return pltpu.make_async_copy(ybuf.at[slot, pl.ds(i, 1)], y_hbm.at[pl.ds(dst, 1)], ssem.at[slot])

    def start_all(copy, block, slot):
        for i in range(rows):
            copy(block, i, slot).start()

    def wait_all(copy, block, slot):
        for i in range(rows):
            copy(block, i, slot).wait()

    @pl.when(blk < n_used)
    def _():
        slot = blk % 2
        nxt = jnp.minimum(blk + 1, n_used - 1)

        @pl.when(blk == 0)
        def _():
            start_all(gather_copy, 0, 0)
            ybuf[1] = jnp.zeros(ybuf.shape[1:], ybuf.dtype)
            for part in range(2):
                spare = pltpu.make_async_copy(ybuf.at[1], y_hbm.at[pl.ds(n_tokens + part * rows, rows)], ssem.at[1])
                spare.start()
                spare.wait()

        @pl.when(blk >= 2)
        def _():
            wait_all(scatter_copy, blk - 2, slot)

        wait_all(gather_copy, blk, slot)
        start_all(gather_copy, nxt, 1 - slot)

        x = xbuf[slot]
        xb = x.astype(BF16)

        def expert(gu_ref, dn_ref):
            gu = _dot(xb, gu_ref[0])
            gate_act = gu[:, :D_EXPERT]
            act = gate_act / (1.0 + jnp.exp(-gate_act)) * gu[:, D_EXPERT:]
            return _dot(act.astype(BF16), dn_ref[0])

        r_i = lax.broadcasted_iota(jnp.int32, (rows, rows), 0)
        c_i = lax.broadcasted_iota(jnp.int32, (rows, rows), 1)
        diag = r_i == c_i
        gates = gate_ref[0]
        g_lo = jnp.sum(jnp.where(diag, gates[0:1, :], 0.0), axis=1, keepdims=True)
        g_hi = jnp.sum(jnp.where(diag, gates[1:2, :], 0.0), axis=1, keepdims=True)
        h = g_lo * expert(gu_lo_ref, dn_lo_ref) + g_hi * expert(gu_hi_ref, dn_hi_ref)
        ybuf[slot] = _layer_norm(DEEPNORM_ALPHA * x + h, g_ref[...], b_ref[...])
        start_all(scatter_copy, blk, slot)

        @pl.when(blk == n_used - 1)
        def _():
            wait_all(gather_copy, nxt, 1 - slot)

            @pl.when(blk >= 1)
            def _():
                wait_all(scatter_copy, blk - 1, 1 - slot)
            wait_all(scatter_copy, blk, slot)


def _moe_experts(x2, slot_tok, blk_lo, blk_hi, n_used, slot_gates, w_gu, w_dn, g, b, *, n):
    d = x2.shape[1]
    n_blocks = blk_lo.shape[0]
    rows = MOE_BLOCK
    grid_spec = pltpu.PrefetchScalarGridSpec(
        num_scalar_prefetch=4,
        grid=(n_blocks,),
        in_specs=[
            pl.BlockSpec(memory_space=pl.ANY),
            pl.BlockSpec((1, 2, rows), lambda i, *_: (i, 0, 0)),
            pl.BlockSpec((1, d, 2 * D_EXPERT), lambda i, tok, lo, hi, nb: (lo[i], 0, 0)),
            pl.BlockSpec((1, d, 2 * D_EXPERT), lambda i, tok, lo, hi, nb: (hi[i], 0, 0)),
            pl.BlockSpec((1, D_EXPERT, d), lambda i, tok, lo, hi, nb: (lo[i], 0, 0)),
            pl.BlockSpec((1, D_EXPERT, d), lambda i, tok, lo, hi, nb: (hi[i], 0, 0)),
            pl.BlockSpec((1, d), lambda i, *_: (0, 0)),
            pl.BlockSpec((1, d), lambda i, *_: (0, 0)),
        ],
        out_specs=pl.BlockSpec(memory_space=pl.ANY),
        scratch_shapes=[
            pltpu.VMEM((2, rows, d), F32),
            pltpu.VMEM((2, rows, d), F32),
            pltpu.SemaphoreType.DMA((2,)),
            pltpu.SemaphoreType.DMA((2,)),
        ],
    )
    return pl.pallas_call(
        functools.partial(_moe_kernel, n_tokens=n),
        grid_spec=grid_spec,
        out_shape=jax.ShapeDtypeStruct((n + 2 * rows, d), F32),
        compiler_params=_params("arbitrary"),
        name="moe_experts",
    )(slot_tok, blk_lo, blk_hi, n_used, x2, slot_gates, w_gu, w_gu, w_dn, w_dn, g, b)


_PAIR_LO = np.array([0, 0, 0, 1, 1, 2], np.int32)
_PAIR_HI = np.array([1, 2, 3, 2, 3, 3], np.int32)


def _dispatch_tables(cls, g_lo, g_hi):
    n = cls.shape[0]
    rows = MOE_BLOCK
    n_blocks = n // rows + N_CLASSES
    i32 = jnp.int32
    cls_ids = jnp.arange(N_CLASSES, dtype=i32)
    _, order, glo_s, ghi_s = lax.sort((cls, jnp.arange(n, dtype=i32), g_lo, g_hi), num_keys=1)
    counts = jnp.sum((cls[:, None] == cls_ids[None, :]).astype(i32), axis=0)
    cls_blocks = (counts + rows - 1) // rows
    blk_end = jnp.cumsum(cls_blocks)
    blk_begin = blk_end - cls_blocks
    start = jnp.cumsum(counts) - counts
    n_used = blk_end[-1]
    blk = jnp.arange(n_blocks, dtype=i32)
    blk_eff = jnp.minimum(blk, n_used - 1)
    blk_cls = jnp.sum((blk_end[None, :] <= blk_eff[:, None]).astype(i32), axis=1)
    pick = blk_cls[:, None] == cls_ids[None, :]
    table = lambda t: jnp.sum(jnp.where(pick, t[None, :], 0), axis=1)
    blk_rank = (blk_eff - table(blk_begin)) * rows
    blk_nvalid = jnp.where(blk < n_used, jnp.clip(table(counts) - blk_rank, 0, rows), 0)
    row = jnp.arange(rows, dtype=i32)
    valid = row[None, :] < blk_nvalid[:, None]
    src = jnp.clip((table(start) + blk_rank)[:, None] + row[None, :], 0, n - 1)
    slot_tok = jnp.where(valid, order[src], -1).reshape(n_blocks * rows)
    slot_gates = jnp.stack([jnp.where(valid, glo_s[src], 0.0), jnp.where(valid, ghi_s[src], 0.0)], axis=1)
    grp = blk_cls // N_PAIRS
    pair = blk_cls % N_PAIRS
    pair_ids = jnp.arange(N_PAIRS, dtype=i32)
    pair_pick = pair[:, None] == pair_ids[None, :]
    blk_lo = grp * EXPERTS_PER_GROUP + jnp.sum(jnp.where(pair_pick, jnp.asarray(_PAIR_LO)[None, :], 0), axis=1)
    blk_hi = grp * EXPERTS_PER_GROUP + jnp.sum(jnp.where(pair_pick, jnp.asarray(_PAIR_HI)[None, :], 0), axis=1)
    return slot_tok, blk_lo.astype(i32), blk_hi.astype(i32), n_used.reshape(1).astype(i32), slot_gates


def _tile_heads(t64):
    return jnp.concatenate([t64] * (LANES // HEAD_DIM), axis=-1)


def _rope_tables(seq):
    t = jnp.arange(seq, dtype=jnp.int32)

    def tables(pos, dim):
        inv_freq = ROPE_THETA ** (-jnp.arange(0, dim, 2, dtype=F32) / dim)
        ang = pos.astype(F32)[:, None] * inv_freq[None, :]
        return jnp.cos(ang), jnp.sin(ang)

    c1, s1 = tables(t, HEAD_DIM)
    cos_1d = _tile_heads(jnp.concatenate([c1, c1], axis=-1))
    sin_1d = _tile_heads(jnp.concatenate([-s1, s1], axis=-1))
    cr, sr = tables(t // GRID_W, HEAD_DIM // 2)
    cc, sc = tables(t % GRID_W, HEAD_DIM // 2)
    cos_ax = _tile_heads(jnp.concatenate([cr, cr, cc, cc], axis=-1))
    sin_ax = _tile_heads(jnp.concatenate([-sr, sr, -sc, sc], axis=-1))
    return (cos_1d, sin_1d), (cos_ax, sin_ax)


def _block_diag_ones():
    i = np.arange(LANES)
    return jnp.asarray((i[:, None] // HEAD_DIM) == (i[None, :] // HEAD_DIM), BF16)


def kernel(x_prompt, x_sample, mem_prompt, mem_sample, a_w_in, a_sink, a_w_out, b_w_in, b_q_norm, b_k_norm,
           b_w_out, c_w_in, c_w_out, m_w_q, m_w_kv, m_w_out, ln_g, ln_b, router_w, router_bias, e_w_gu, e_w_down):
    seq = x_prompt.shape[1]
    assert x_sample.shape[1] == seq
    x = jnp.concatenate([x_prompt, x_sample], axis=0)
    mem = jnp.concatenate([mem_prompt, mem_sample], axis=0)
    batch = x.shape[0]
    n = batch * seq
    n_mem = mem.shape[1]
    x2 = x.reshape(n, D_MODEL)

    rope_1d, rope_ax = _rope_tables(seq)
    ones_bd = _block_diag_ones()
    unit_gains = jnp.ones((2, LANES), F32)
    scale = HEAD_DIM ** -0.5

    rw = router_w.astype(F32).reshape(D_MODEL, N_EXPERT_GROUPS, EXPERTS_PER_GROUP).transpose(0, 2, 1)
    rw = jnp.pad(rw.reshape(D_MODEL, N_EXPERTS), ((0, 0), (0, LANES - N_EXPERTS)))
    rw_hi = rw.astype(BF16)
    rw_lo = (rw - rw_hi.astype(F32)).astype(BF16)
    rb = router_bias.astype(F32).reshape(N_EXPERT_GROUPS, EXPERTS_PER_GROUP).T.reshape(N_EXPERTS, 1)
    rb_t = jnp.broadcast_to(rb, (N_EXPERTS, TOKEN_TILE))

    cq, ck = C_HEADS * HEAD_DIM, C_KV_HEADS * HEAD_DIM
    n_cg = len(C_PATTERNS)

    for i in range(DEPTH):
        kind, j = i % N_MIXERS, i // N_MIXERS
        lg = lambda s_: ln_g[i, s_].astype(F32).reshape(1, D_MODEL)
        lb = lambda s_: ln_b[i, s_].astype(F32).reshape(1, D_MODEL)
        if kind == 0:
            nq, nk = A_HEADS * HEAD_DIM, A_KV_HEADS * HEAD_DIM
            q, k, v = _qkv_project(x2, a_w_in[j].astype(BF16), rope_1d[0], rope_1d[1], unit_gains, ones_bd, n=n,
                                   nq=nq, nk=nk, nv=nk, half=HEAD_DIM // 2, qk_norm=False, q_scale=scale, seq=seq)
            o, = _banded_attention(q, k, v, a_sink[j].astype(F32), batch=batch, seq=seq, dilation=1, group=0,
                                   n_groups=1, n_q_heads=A_HEADS, n_kv_heads=A_KV_HEADS, window=A_WINDOW,
                                   tq=A_WINDOW, with_lse=False)
            x2 = _outproj_ln([o], [], a_w_out[j].astype(BF16), x2, lg(0), lb(0))
        elif kind == 1:
            nq, nk = B_HEADS * HEAD_DIM, B_KV_HEADS * HEAD_DIM
            gains = jnp.stack([_tile_heads(b_q_norm[j].astype(F32)), _tile_heads(b_k_norm[j].astype(F32))], axis=0)
            q, k, v = _qkv_project(x2, b_w_in[j].astype(BF16), rope_ax[0], rope_ax[1], gains, ones_bd, n=n,
                                   nq=nq, nk=nk, nv=nk, half=HEAD_DIM // 4, qk_norm=True,
                                   q_scale=scale * LOG2_E, seq=seq, v_transposed=True)
            o = _flash_attention(q, k, v, batch=batch, seq=seq, n_q_heads=B_HEADS, n_kv_heads=B_KV_HEADS,
                                 tq=256, tk=1024)
            x2 = _outproj_ln([o], [], b_w_out[j].astype(BF16), x2, lg(0), lb(0))
        else:
            wc = c_w_in[j].reshape(D_MODEL, n_cg, cq + 2 * ck)
            wc = jnp.concatenate([wc[:, :, :cq].reshape(D_MODEL, n_cg * cq),
                                  wc[:, :, cq:cq + ck].reshape(D_MODEL, n_cg * ck),
                                  wc[:, :, cq + ck:].reshape(D_MODEL, n_cg * ck)], axis=1).astype(BF16)
            q, k, v = _qkv_project(x2, wc, rope_1d[0], rope_1d[1], unit_gains, ones_bd, n=n,
                                   nq=n_cg * cq, nk=n_cg * ck, nv=n_cg * ck, half=HEAD_DIM // 2, qk_norm=False,
                                   q_scale=scale, seq=seq)
            no_sink = jnp.full((C_HEADS,), NEG_INF, F32)
            os_, lses = [], []
            for gi, (window, dilation) in enumerate(C_PATTERNS):
                half_w = (window // 2) // dilation
                o, lse = _banded_attention(q, k, v, no_sink, batch=batch, seq=seq, dilation=dilation, group=gi,
                                           n_groups=n_cg, n_q_heads=C_HEADS, n_kv_heads=C_KV_HEADS,
                                           window=half_w, tq=2 * half_w, with_lse=True)
                os_.append(o)
                lses.append(lse)
            x2 = _outproj_ln(os_, lses, c_w_out[j].astype(BF16), x2, lg(0), lb(0))

        kv = _matmul(mem.reshape(batch * n_mem, D_MODEL), m_w_kv[i].astype(BF16), BF16, tm=n_mem)
        x2 = _memory_attention(x2, kv.reshape(batch, n_mem, 2 * D_MODEL),
                               m_w_q[i].astype(BF16), m_w_out[i].astype(BF16), lg(1), lb(1), seq=seq)

        cls, g_lo, g_hi = _router(x2, rw_hi, rw_lo, rb_t)
        slot_tok, blk_lo, blk_hi, n_used, slot_gates = _dispatch_tables(cls, g_lo, g_hi)
        x2 = _moe_experts(x2, slot_tok, blk_lo, blk_hi, n_used, slot_gates,
                          e_w_gu[i].astype(BF16), e_w_down[i].astype(BF16), lg(2), lb(2), n=n)

    n_prompt = x_prompt.shape[0] * seq
    return (x2[:n_prompt].reshape(x_prompt.shape), x2[n_prompt:n].reshape(x_sample.shape))
```

```python
import functools

import jax
import jax.numpy as jnp
import numpy as np
from jax import lax
from jax.experimental import pallas as pl
from jax.experimental.pallas import tpu as pltpu

F32 = jnp.float32
BF16 = jnp.bfloat16

D_MODEL = 1024
DEPTH = 4
N_MIXERS = 3
HEAD_DIM = 64
ROPE_THETA = 10000.0
GRID_W = 64

A_HEADS, A_KV_HEADS, A_WINDOW = 16, 4, 128
B_HEADS, B_KV_HEADS = 16, 4
C_PATTERNS = ((128, 1), (512, 4), (2048, 16))
C_HEADS, C_KV_HEADS = 8, 2

MEM_HEADS = 4
MEM_HEAD_DIM = D_MODEL // MEM_HEADS

N_EXPERTS = 32
N_EXPERT_GROUPS = 8
EXPERTS_PER_GROUP = 4
D_EXPERT = D_MODEL // 4
N_PAIRS = 6
N_CLASSES = N_EXPERT_GROUPS * N_PAIRS
MOE_BLOCK = 128

LN_EPS = 1e-5
QK_NORM_EPS = 1e-6
DEEPNORM_ALPHA = (2 * DEPTH) ** 0.25

LANES = 128
TOKEN_TILE = 512
BANDED_QUERY_BLOCK = 256
FLASH_CHUNKS_PER_BODY = 4
VMEM_LIMIT_BYTES = 48 * 1024 * 1024

NEG_INF = float("-inf")
LOG2_E = 1.4426950408889634
LN_2 = 0.6931471805599453


def _params(*semantics):
    return pltpu.CompilerParams(dimension_semantics=semantics, vmem_limit_bytes=VMEM_LIMIT_BYTES)


def _layer_norm(z, g, b):
    mu = jnp.mean(z, axis=-1, keepdims=True)
    zc = z - mu
    var = jnp.mean(zc * zc, axis=-1, keepdims=True)
    return zc * lax.rsqrt(var + LN_EPS) * g + b


def _dot(a, b):
    return jnp.dot(a, b, preferred_element_type=F32)


def _dot_nt(a, b):
    return lax.dot_general(a, b, (((1,), (1,)), ((), ())), preferred_element_type=F32)


def _qkv_kernel(x_ref, w_ref, cos_ref, sin_ref, gain_ref, ones_ref, q_ref, k_ref, v_ref, *,
                nq, nk, half, qk_norm, q_scale, v_transposed):
    acc = _dot(x_ref[...].astype(BF16), w_ref[...])
    cos = cos_ref[...]
    sin = sin_ref[...]
    lane = lax.broadcasted_iota(jnp.int32, cos.shape, 1)
    first = (lane % (2 * half)) < half

    def rotate(a, gain, scale):
        if qk_norm:
            a2 = a * a
            hi = a2.astype(BF16)
            lo = (a2 - hi.astype(F32)).astype(BF16)
            ms = (_dot(hi, ones_ref[...]) + _dot(lo, ones_ref[...])) * (1.0 / HEAD_DIM)
            a = a * lax.rsqrt(ms + QK_NORM_EPS) * gain
        partner = jnp.where(first, pltpu.roll(a, LANES - half, 1), pltpu.roll(a, half, 1))
        r = a * cos + partner * sin
        if scale != 1.0:
            r = r * scale
        return r.astype(BF16)

    gq = gain_ref[0:1, :]
    gk = gain_ref[1:2, :]
    for c in range(nq // LANES):
        q_ref[:, c * LANES:(c + 1) * LANES] = rotate(acc[:, c * LANES:(c + 1) * LANES], gq, q_scale)
    for c in range(nk // LANES):
        k_ref[:, c * LANES:(c + 1) * LANES] = rotate(acc[:, nq + c * LANES:nq + (c + 1) * LANES], gk, 1.0)
    if v_transposed:
        v_ref[0] = acc[:, nq + nk:].T.astype(BF16)
    else:
        v_ref[...] = acc[:, nq + nk:].astype(BF16)


def _qkv_project(x2, w, cos, sin, gains, ones_bd, *, n, nq, nk, nv, half, qk_norm, q_scale, seq,
                 v_transposed=False):
    d = x2.shape[1]
    tm = TOKEN_TILE
    tiles_per_seq = seq // tm
    kern = functools.partial(_qkv_kernel, nq=nq, nk=nk, half=half, qk_norm=qk_norm, q_scale=q_scale,
                             v_transposed=v_transposed)
    if v_transposed:
        v_spec = pl.BlockSpec((1, nv, tm), lambda i: (i, 0, 0))
        v_shape = jax.ShapeDtypeStruct((n // tm, nv, tm), BF16)
    else:
        v_spec = pl.BlockSpec((tm, nv), lambda i: (i, 0))
        v_shape = jax.ShapeDtypeStruct((n, nv), BF16)
    return pl.pallas_call(
        kern,
        grid=(n // tm,),
        in_specs=[
            pl.BlockSpec((tm, d), lambda i: (i, 0)),
            pl.BlockSpec((d, nq + nk + nv), lambda i: (0, 0)),
            pl.BlockSpec((tm, LANES), lambda i: (i % tiles_per_seq, 0)),
            pl.BlockSpec((tm, LANES), lambda i: (i % tiles_per_seq, 0)),
            pl.BlockSpec((2, LANES), lambda i: (0, 0)),
            pl.BlockSpec((LANES, LANES), lambda i: (0, 0)),
        ],
        out_specs=[
            pl.BlockSpec((tm, nq), lambda i: (i, 0)),
            pl.BlockSpec((tm, nk), lambda i: (i, 0)),
            v_spec,
        ],
        out_shape=[
            jax.ShapeDtypeStruct((n, nq), BF16),
            jax.ShapeDtypeStruct((n, nk), BF16),
            v_shape,
        ],
        compiler_params=_params("parallel"),
        name="qkv_project",
    )(x2, w, cos, sin, gains, ones_bd)


def _banded_kernel(sink_ref, band_ref, q_ref, kp_ref, kc_ref, kn_ref, vp_ref, vc_ref, vn_ref, *out_refs,
                   tq, w, n_q_heads, n_kv_heads, with_lse):
    o_ref = out_refs[0]
    j = pl.program_id(1)
    g_per = n_q_heads // n_kv_heads
    span = tq + 2 * w
    m_lanes = g_per * tq
    key = lax.broadcasted_iota(jnp.int32, (span, 1), 0)
    off_seq = ((key < w) & (j == 0)) | ((key >= w + tq) & (j == pl.num_programs(1) - 1))
    bias = band_ref[...] + jnp.where(off_seq, NEG_INF, 0.0)
    lane_row = lax.broadcasted_iota(jnp.int32, (1, m_lanes), 1)
    half_rows = lax.broadcasted_iota(jnp.int32, (2 * HEAD_DIM, m_lanes), 0) >= HEAD_DIM
    ones = jnp.ones((HEAD_DIM, span), BF16)

    vcat = jnp.concatenate([vp_ref[0], vc_ref[0], vn_ref[0]], axis=0).astype(F32)
    vt_all = jnp.concatenate([vcat[:, c * LANES:(c + 1) * LANES].T for c in range(vcat.shape[1] // LANES)],
                             axis=0).astype(BF16)

    def scores(kh):
        cs = slice(kh * HEAD_DIM, (kh + 1) * HEAD_DIM)
        kcat = jnp.concatenate([kp_ref[0, :, cs], kc_ref[0, :, cs], kn_ref[0, :, cs]], axis=0)
        qs = jnp.concatenate(
            [q_ref[0, :, (kh * g_per + g) * HEAD_DIM:(kh * g_per + g + 1) * HEAD_DIM] for g in range(g_per)],
            axis=0)
        return _dot_nt(kcat, qs) + bias

    outs, lses = [], []
    st_next = scores(0)
    for kh in range(n_kv_heads):
        cs = slice(kh * HEAD_DIM, (kh + 1) * HEAD_DIM)
        st = st_next
        if kh + 1 < n_kv_heads:
            st_next = scores(kh + 1)
        sink = jnp.full((1, m_lanes), sink_ref[kh * g_per], F32)
        for g in range(1, g_per):
            sink = jnp.where(lane_row >= g * tq, sink_ref[kh * g_per + g], sink)
        sink = sink * LOG2_E
        m = jnp.maximum(jnp.max(st, axis=0, keepdims=True), sink)
        p = jnp.exp2(st - m).astype(BF16)
        vext = jnp.concatenate([vt_all[cs, :], ones], axis=0)
        acc = _dot(vext, p)
        acc = jnp.where(half_rows, acc + jnp.exp2(sink - m), acc)
        num, den = acc[:HEAD_DIM], acc[HEAD_DIM:]
        low = jnp.log(den) + m * LN_2 if with_lse else den
        res = jnp.concatenate([num / den, low], axis=0).T
        for g in range(g_per):
            outs.append(res[g * tq:(g + 1) * tq, :HEAD_DIM])
            if with_lse:
                lses.append(res[g * tq:(g + 1) * tq, HEAD_DIM:])
    o_ref[0] = jnp.concatenate(outs, axis=-1).astype(o_ref.dtype)
    if with_lse:
        out_refs[1][0] = jnp.concatenate(lses, axis=-1)


def _banded_attention(q, k, v, sink, *, batch, seq, dilation, group, n_groups, n_q_heads, n_kv_heads,
                      window, tq, with_lse):
    d = dilation
    sd = seq // d
    qw = n_q_heads * HEAD_DIM
    kw = n_kv_heads * HEAD_DIM
    w = window
    r_blocks = tq // w
    nb_w = sd // w
    q3 = q.reshape(batch, sd, d * n_groups * qw)
    k3 = k.reshape(batch, sd, d * n_groups * kw)
    v3 = v.reshape(batch, sd, d * n_groups * kw)

    def q_map(bc, j):
        return (bc // d, j, (bc % d) * n_groups + group)

    def prev_map(bc, j):
        return (bc // d, jnp.maximum(j * r_blocks - 1, 0), (bc % d) * n_groups + group)

    def next_map(bc, j):
        return (bc // d, jnp.minimum((j + 1) * r_blocks, nb_w - 1), (bc % d) * n_groups + group)

    def o_map(bc, j):
        return (bc // d, j, bc % d)

    g_per = n_q_heads // n_kv_heads
    rel = np.arange(tq + 2 * w)[:, None] - w - (np.arange(g_per * tq) % tq)[None, :]
    band = jnp.asarray(np.where(np.abs(rel) <= w, 0.0, NEG_INF), F32)

    kern = functools.partial(_banded_kernel, tq=tq, w=w, n_q_heads=n_q_heads,
                             n_kv_heads=n_kv_heads, with_lse=with_lse)
    out_specs = [pl.BlockSpec((1, tq, qw), o_map)]
    out_shape = [jax.ShapeDtypeStruct((batch, sd, d * qw), BF16)]
    if with_lse:
        out_specs.append(pl.BlockSpec((1, tq, qw), o_map))
        out_shape.append(jax.ShapeDtypeStruct((batch, sd, d * qw), F32))
    res = pl.pallas_call(
        kern,
        grid=(batch * d, sd // tq),
        in_specs=[
            pl.BlockSpec(memory_space=pltpu.SMEM),
            pl.BlockSpec(band.shape, lambda bc, j: (0, 0)),
            pl.BlockSpec((1, tq, qw), q_map),
            pl.BlockSpec((1, w, kw), prev_map),
            pl.BlockSpec((1, tq, kw), q_map),
            pl.BlockSpec((1, w, kw), next_map),
            pl.BlockSpec((1, w, kw), prev_map),
            pl.BlockSpec((1, tq, kw), q_map),
            pl.BlockSpec((1, w, kw), next_map),
        ],
        out_specs=out_specs,
        out_shape=out_shape,
        compiler_params=_params("parallel", "parallel"),
        name=f"banded_attention_d{d}",
    )(sink, band, q3, k3, k3, k3, v3, v3, v3)
    return [r.reshape(batch * seq, qw) for r in res]


def _flash_kernel(q_ref, k_ref, vt_ref, o_ref, st_a, st_b, *, tq, tk, tv, seq, n_kv_heads, g_per):
    ones = jnp.ones((HEAD_DIM, tk), BF16)
    outs = [None] * (n_kv_heads * g_per)
    for kh in range(n_kv_heads):
        cs = slice(kh * HEAD_DIM, (kh + 1) * HEAD_DIM)
        qs = jnp.concatenate(
            [q_ref[0, :, (kh * g_per + g) * HEAD_DIM:(kh * g_per + g + 1) * HEAD_DIM] for g in range(g_per)],
            axis=0)

        def scores(c, st_ref):
            start = pl.multiple_of(c * tk, tk)
            st_ref[...] = _dot_nt(k_ref[0, pl.ds(start, tk), cs], qs)

        def update(c, st_ref, m, acc):
            vt = jnp.concatenate([vt_ref[c * (tk // tv) + t, cs, :] for t in range(tk // tv)], axis=-1)
            vext = jnp.concatenate([vt, ones], axis=0)
            st = st_ref[...]
            m_new = jnp.maximum(m, jnp.max(st, axis=0, keepdims=True))
            alpha = jnp.exp2(m - m_new)
            p = jnp.exp2(st - m_new).astype(BF16)
            return m_new, alpha * acc + _dot(vext, p)

        bufs = (st_a, st_b)

        def group(cg, carry, last):
            m, acc = carry
            for t in range(FLASH_CHUNKS_PER_BODY):
                c = cg * FLASH_CHUNKS_PER_BODY + t
                if not (last and t == FLASH_CHUNKS_PER_BODY - 1):
                    scores(c + 1, bufs[(t + 1) % 2])
                m, acc = update(c, bufs[t % 2], m, acc)
            return m, acc

        n_groups = seq // (FLASH_CHUNKS_PER_BODY * tk)
        m0 = jnp.full((1, g_per * tq), NEG_INF, F32)
        acc0 = jnp.zeros((2 * HEAD_DIM, g_per * tq), F32)
        scores(0, st_a)
        carry = lax.fori_loop(0, n_groups - 1, lambda cg, cr: group(cg, cr, False), (m0, acc0))
        _, acc = group(n_groups - 1, carry, True)
        acc = acc.T
        o = acc[:, :HEAD_DIM] / acc[:, HEAD_DIM:]
        for g in range(g_per):
            outs[kh * g_per + g] = o[g * tq:(g + 1) * tq]
    o_ref[0] = jnp.concatenate(outs, axis=-1).astype(o_ref.dtype)


def _flash_attention(q, k, vt, *, batch, seq, n_q_heads, n_kv_heads, tq, tk):
    qw = n_q_heads * HEAD_DIM
    kw = n_kv_heads * HEAD_DIM
    tv = vt.shape[2]
    chunks = seq // tv
    kern = functools.partial(_flash_kernel, tq=tq, tk=tk, tv=tv, seq=seq, n_kv_heads=n_kv_heads,
                             g_per=n_q_heads // n_kv_heads)
    out = pl.pallas_call(
        kern,
        grid=(batch, seq // tq),
        in_specs=[
            pl.BlockSpec((1, tq, qw), lambda b, j: (b, j, 0)),
            pl.BlockSpec((1, seq, kw), lambda b, j: (b, 0, 0)),
            pl.BlockSpec((chunks, kw, tv), lambda b, j: (b, 0, 0)),
        ],
        out_specs=pl.BlockSpec((1, tq, qw), lambda b, j: (b, j, 0)),
        out_shape=jax.ShapeDtypeStruct((batch, seq, qw), BF16),
        scratch_shapes=[pltpu.VMEM((tk, (n_q_heads // n_kv_heads) * tq), F32) for _ in range(2)],
        compiler_params=_params("parallel", "arbitrary"),
        name="dense_attention",
    )(q.reshape(batch, seq, qw), k.reshape(batch, seq, kw), vt)
    return out.reshape(batch * seq, qw)


def _outproj_kernel(o_ref, w_ref, x_ref, g_ref, b_ref, y_ref):
    h = _dot(o_ref[...], w_ref[...])
    y_ref[...] = _layer_norm(DEEPNORM_ALPHA * x_ref[...] + h, g_ref[...], b_ref[...])


def _merge_outproj_kernel(o0_ref, o1_ref, o2_ref, l0_ref, l1_ref, l2_ref, w_ref, x_ref, g_ref, b_ref, y_ref):
    l0, l1, l2 = l0_ref[...], l1_ref[...], l2_ref[...]
    m = jnp.maximum(jnp.maximum(l0, l1), l2)
    e0, e1, e2 = jnp.exp(l0 - m), jnp.exp(l1 - m), jnp.exp(l2 - m)
    tot = e0 + e1 + e2
    o = (e0 * o0_ref[...].astype(F32) + e1 * o1_ref[...].astype(F32) + e2 * o2_ref[...].astype(F32)) / tot
    h = _dot(o.astype(BF16), w_ref[...])
    y_ref[...] = _layer_norm(DEEPNORM_ALPHA * x_ref[...] + h, g_ref[...], b_ref[...])


def _outproj_ln(os_, lses, w, x2, g, b):
    n, d = os_[0].shape[0], x2.shape[1]
    tm = TOKEN_TILE
    kdim = w.shape[0]
    row = lambda i: (i, 0)
    fixed = lambda i: (0, 0)
    acts = list(os_) + list(lses)
    kern = _merge_outproj_kernel if lses else _outproj_kernel
    return pl.pallas_call(
        kern,
        grid=(n // tm,),
        in_specs=[pl.BlockSpec((tm, kdim), row) for _ in acts] + [
            pl.BlockSpec((kdim, d), fixed),
            pl.BlockSpec((tm, d), row),
            pl.BlockSpec((1, d), fixed),
            pl.BlockSpec((1, d), fixed),
        ],
        out_specs=pl.BlockSpec((tm, d), row),
        out_shape=jax.ShapeDtypeStruct((n, d), F32),
        compiler_params=_params("parallel"),
        name="outproj_ln",
    )(*acts, w, x2, g, b)


def _matmul_kernel(x_ref, w_ref, o_ref):
    o_ref[...] = _dot(x_ref[...].astype(BF16), w_ref[...]).astype(o_ref.dtype)


def _matmul(x2, w, out_dtype, tm):
    n, kdim = x2.shape
    cols = w.shape[1]
    return pl.pallas_call(
        _matmul_kernel,
        grid=(n // tm,),
        in_specs=[pl.BlockSpec((tm, kdim), lambda i: (i, 0)), pl.BlockSpec((kdim, cols), lambda i: (0, 0))],
        out_specs=pl.BlockSpec((tm, cols), lambda i: (i, 0)),
        out_shape=jax.ShapeDtypeStruct((n, cols), out_dtype),
        compiler_params=_params("parallel"),
        name="matmul",
    )(x2, w)


def _memattn_kernel(x_ref, wq_ref, kv_ref, wo_ref, g_ref, b_ref, y_ref):
    x = x_ref[...]
    q = (_dot(x.astype(BF16), wq_ref[...]) * (MEM_HEAD_DIM ** -0.5)).astype(BF16)
    outs = []
    for h in range(MEM_HEADS):
        cs = slice(h * MEM_HEAD_DIM, (h + 1) * MEM_HEAD_DIM)
        kh = kv_ref[0, :, cs]
        vh = kv_ref[0, :, D_MODEL + h * MEM_HEAD_DIM:D_MODEL + (h + 1) * MEM_HEAD_DIM]
        s = _dot_nt(q[:, cs], kh)
        m = jnp.max(s, axis=-1, keepdims=True)
        p = jnp.exp(s - m)
        denom = jnp.sum(p, axis=-1, keepdims=True)
        outs.append(_dot(p.astype(BF16), vh) / denom)
    o = jnp.concatenate(outs, axis=-1).astype(BF16)
    h_out = _dot(o, wo_ref[...])
    y_ref[...] = _layer_norm(DEEPNORM_ALPHA * x + h_out, g_ref[...], b_ref[...])


def _memory_attention(x2, kv, wq, wo, g, b, *, seq):
    d = x2.shape[1]
    batch, n_mem = kv.shape[0], kv.shape[1]
    tm = TOKEN_TILE
    tiles = seq // tm
    return pl.pallas_call(
        _memattn_kernel,
        grid=(batch, tiles),
        in_specs=[
            pl.BlockSpec((tm, d), lambda bi, i: (bi * tiles + i, 0)),
            pl.BlockSpec((d, d), lambda bi, i: (0, 0)),
            pl.BlockSpec((1, n_mem, 2 * d), lambda bi, i: (bi, 0, 0)),
            pl.BlockSpec((d, d), lambda bi, i: (0, 0)),
            pl.BlockSpec((1, d), lambda bi, i: (0, 0)),
            pl.BlockSpec((1, d), lambda bi, i: (0, 0)),
        ],
        out_specs=pl.BlockSpec((tm, d), lambda bi, i: (bi * tiles + i, 0)),
        out_shape=jax.ShapeDtypeStruct((batch * seq, d), F32),
        compiler_params=_params("parallel", "parallel"),
        name="memory_attention",
    )(x2, wq, kv, wo, g, b)


def _router_kernel(x_ref, wh_ref, wl_ref, bias_ref, cls_ref, gate_ref):
    x = x_ref[...]
    xh = x.astype(BF16)
    xl = (x - xh.astype(F32)).astype(BF16)
    logits = _dot(xh, wh_ref[...]) + _dot(xl, wh_ref[...]) + _dot(xh, wl_ref[...])
    lt = logits.T[:N_EXPERTS]
    scores = 1.0 / (1.0 + jnp.exp(-lt))
    biased = scores + bias_ref[...]
    ng = N_EXPERT_GROUPS
    a = [biased[l * ng:(l + 1) * ng] for l in range(EXPERTS_PER_GROUP)]
    sc = [scores[l * ng:(l + 1) * ng] for l in range(EXPERTS_PER_GROUP)]
    gs = None
    for i in range(EXPERTS_PER_GROUP):
        for k in range(i + 1, EXPERTS_PER_GROUP):
            pair = a[i] + a[k]
            gs = pair if gs is None else jnp.maximum(gs, pair)
    gidx = lax.broadcasted_iota(jnp.int32, gs.shape, 0).astype(F32)
    gmax = jnp.max(gs, axis=0, keepdims=True)
    grp_f = jnp.min(jnp.where(gs == gmax, gidx, float(ng)), axis=0, keepdims=True)
    sel = gidx == grp_f
    grp = grp_f.astype(jnp.int32)
    v = [jnp.sum(jnp.where(sel, a[l], 0.0), axis=0, keepdims=True) for l in range(EXPERTS_PER_GROUP)]
    sv = [jnp.sum(jnp.where(sel, sc[l], 0.0), axis=0, keepdims=True) for l in range(EXPERTS_PER_GROUP)]

    def first_argmax(vals, excluded):
        best = None
        for l, val in enumerate(vals):
            cand = val if excluded is None else jnp.where(excluded == l, NEG_INF, val)
            best = cand if best is None else jnp.maximum(best, cand)
        idx = jnp.full(best.shape, EXPERTS_PER_GROUP - 1, jnp.int32)
        for l in range(EXPERTS_PER_GROUP - 1, -1, -1):
            hit = vals[l] == best
            if excluded is not None:
                hit = hit & (excluded != l)
            idx = jnp.where(hit, l, idx)
        return idx

    i1 = first_argmax(v, None)
    i2 = first_argmax(v, i1)

    def pick(vals, idx):
        out = vals[0]
        for l in range(1, EXPERTS_PER_GROUP):
            out = jnp.where(idx == l, vals[l], out)
        return out

    s1 = pick(sv, i1)
    s2 = pick(sv, i2)
    tot = s1 + s2
    lo = jnp.minimum(i1, i2)
    hi = jnp.maximum(i1, i2)
    first_is_lo = i1 < i2
    g_lo = jnp.where(first_is_lo, s1, s2) / tot
    g_hi = jnp.where(first_is_lo, s2, s1) / tot
    pair_idx = jnp.right_shift(lo * (7 - lo), 1) + (hi - lo - 1)
    cls_ref[0] = grp * N_PAIRS + pair_idx
    gate_ref[0] = jnp.concatenate([g_lo, g_hi], axis=0)


def _router(x2, wh, wl, bias_t):
    n, d = x2.shape
    tm = TOKEN_TILE
    nt = n // tm
    cls, gates = pl.pallas_call(
        _router_kernel,
        grid=(nt,),
        in_specs=[
            pl.BlockSpec((tm, d), lambda i: (i, 0)),
            pl.BlockSpec((d, LANES), lambda i: (0, 0)),
            pl.BlockSpec((d, LANES), lambda i: (0, 0)),
            pl.BlockSpec((N_EXPERTS, tm), lambda i: (0, 0)),
        ],
        out_specs=[
            pl.BlockSpec((1, 1, tm), lambda i: (i, 0, 0)),
            pl.BlockSpec((1, 2, tm), lambda i: (i, 0, 0)),
        ],
        out_shape=[
            jax.ShapeDtypeStruct((nt, 1, tm), jnp.int32),
            jax.ShapeDtypeStruct((nt, 2, tm), F32),
        ],
        compiler_params=_params("parallel"),
        name="router",
    )(x2, wh, wl, bias_t)
    return cls.reshape(n), gates[:, 0, :].reshape(n), gates[:, 1, :].reshape(n)


def _moe_kernel(tok_ref, lo_ref, hi_ref, nblk_ref,
                x_hbm, gate_ref, gu_lo_ref, gu_hi_ref, dn_lo_ref, dn_hi_ref, g_ref, b_ref,
                y_hbm, xbuf, ybuf, gsem, ssem, *, n_tokens):
    blk = pl.program_id(0)
    n_used = nblk_ref[0]
    rows = MOE_BLOCK

    def gather_copy(block, i, slot):
        tok = 0 if block is None else jnp.maximum(tok_ref[block * rows + i], 0)
        return pltpu.make_async_copy(x_hbm.at[pl.ds(tok, 1)], xbuf.at[slot, pl.ds(i, 1)], gsem.at[slot])

    def scatter_copy(block, i, slot):
        if block is None:
            dst = 0
        else:
            tok = tok_ref[block * rows + i]
            dst = jnp.where(tok < 0, n_tokens + slot * rows + i, tok)
        return pltpu.make_async_copy(ybuf.at[slot, pl.ds(i, 1)], y_hbm.at[pl.ds(dst, 1)], ssem.at[slot])

    def start_all(copy, block, slot):
        for i in range(rows):
            copy(block, i, slot).start()

    def wait_all(copy, slot):
        for i in range(rows):
            copy(None, i, slot).wait()

    @pl.when(blk < n_used)
    def _():
        slot = blk % 2
        nxt = jnp.minimum(blk + 1, n_used - 1)

        @pl.when(blk == 0)
        def _():
            start_all(gather_copy, 0, 0)
            ybuf[1] = jnp.zeros(ybuf.shape[1:], ybuf.dtype)
            for part in range(2):
                spare = pltpu.make_async_copy(ybuf.at[1], y_hbm.at[pl.ds(n_tokens + part * rows, rows)], ssem.at[1])
                spare.start()
                spare.wait()

        @pl.when(blk >= 2)
        def _():
            wait_all(scatter_copy, slot)

        wait_all(gather_copy, slot)
        start_all(gather_copy, nxt, 1 - slot)

        x = xbuf[slot]
        xb = x.astype(BF16)

        def expert(gu_ref, dn_ref):
            gu = _dot(xb, gu_ref[0])
            gate_act = gu[:, :D_EXPERT]
            act = gate_act / (1.0 + jnp.exp(-gate_act)) * gu[:, D_EXPERT:]
            return _dot(act.astype(BF16), dn_ref[0])

        r_i = lax.broadcasted_iota(jnp.int32, (rows, rows), 0)
        c_i = lax.broadcasted_iota(jnp.int32, (rows, rows), 1)
        diag = r_i == c_i
        gates = gate_ref[0]
        g_lo = jnp.sum(jnp.where(diag, gates[0:1, :], 0.0), axis=1, keepdims=True)
        g_hi = jnp.sum(jnp.where(diag, gates[1:2, :], 0.0), axis=1, keepdims=True)
        h = g_lo * expert(gu_lo_ref, dn_lo_ref) + g_hi * expert(gu_hi_ref, dn_hi_ref)
        ybuf[slot] = _layer_norm(DEEPNORM_ALPHA * x + h, g_ref[...], b_ref[...])
        start_all(scatter_copy, blk, slot)

        @pl.when(blk == n_used - 1)
        def _():
            wait_all(gather_copy, 1 - slot)

            @pl.when(blk >= 1)
            def _():
                wait_all(scatter_copy, 1 - slot)
            wait_all(scatter_copy, slot)


def _moe_experts(x2, slot_tok, blk_lo, blk_hi, n_used, slot_gates, w_gu, w_dn, g, b, *, n):
    d = x2.shape[1]
    n_blocks = blk_lo.shape[0]
    rows = MOE_BLOCK
    grid_spec = pltpu.PrefetchScalarGridSpec(
        num_scalar_prefetch=4,
        grid=(n_blocks,),
        in_specs=[
            pl.BlockSpec(memory_space=pl.ANY),
            pl.BlockSpec((1, 2, rows), lambda i, *_: (i, 0, 0)),
            pl.BlockSpec((1, d, 2 * D_EXPERT), lambda i, tok, lo, hi, nb: (lo[i], 0, 0)),
            pl.BlockSpec((1, d, 2 * D_EXPERT), lambda i, tok, lo, hi, nb: (hi[i], 0, 0)),
            pl.BlockSpec((1, D_EXPERT, d), lambda i, tok, lo, hi, nb: (lo[i], 0, 0)),
            pl.BlockSpec((1, D_EXPERT, d), lambda i, tok, lo, hi, nb: (hi[i], 0, 0)),
            pl.BlockSpec((1, d), lambda i, *_: (0, 0)),
            pl.BlockSpec((1, d), lambda i, *_: (0, 0)),
        ],
        out_specs=pl.BlockSpec(memory_space=pl.ANY),
        scratch_shapes=[
            pltpu.VMEM((2, rows, d), F32),
            pltpu.VMEM((2, rows, d), F32),
            pltpu.SemaphoreType.DMA((2,)),
            pltpu.SemaphoreType.DMA((2,)),
        ],
    )
    return pl.pallas_call(
        functools.partial(_moe_kernel, n_tokens=n),
        grid_spec=grid_spec,
        out_shape=jax.ShapeDtypeStruct((n + 2 * rows, d), F32),
        compiler_params=_params("arbitrary"),
        name="moe_experts",
    )(slot_tok, blk_lo, blk_hi, n_used, x2, slot_gates, w_gu, w_gu, w_dn, w_dn, g, b)


_PAIR_LO = np.array([0, 0, 0, 1, 1, 2], np.int32)
_PAIR_HI = np.array([1, 2, 3, 2, 3, 3], np.int32)


def _dispatch_tables(cls, g_lo, g_hi):
    n = cls.shape[0]
    rows = MOE_BLOCK
    n_blocks = n // rows + N_CLASSES
    i32 = jnp.int32
    cls_ids = jnp.arange(N_CLASSES, dtype=i32)
    _, order, glo_s, ghi_s = lax.sort((cls, jnp.arange(n, dtype=i32), g_lo, g_hi), num_keys=1)
    counts = jnp.sum((cls[:, None] == cls_ids[None, :]).astype(i32), axis=0)
    cls_blocks = (counts + rows - 1) // rows
    blk_end = jnp.cumsum(cls_blocks)
    blk_begin = blk_end - cls_blocks
    start = jnp.cumsum(counts) - counts
    n_used = blk_end[-1]
    blk = jnp.arange(n_blocks, dtype=i32)
    blk_eff = jnp.minimum(blk, n_used - 1)
    blk_cls = jnp.sum((blk_end[None, :] <= blk_eff[:, None]).astype(i32), axis=1)
    pick = blk_cls[:, None] == cls_ids[None, :]
    table = lambda t: jnp.sum(jnp.where(pick, t[None, :], 0), axis=1)
    blk_rank = (blk_eff - table(blk_begin)) * rows
    blk_nvalid = jnp.where(blk < n_used, jnp.clip(table(counts) - blk_rank, 0, rows), 0)
    row = jnp.arange(rows, dtype=i32)
    valid = row[None, :] < blk_nvalid[:, None]
    src = jnp.clip((table(start) + blk_rank)[:, None] + row[None, :], 0, n - 1)
    slot_tok = jnp.where(valid, order[src], -1).reshape(n_blocks * rows)
    slot_gates = jnp.stack([jnp.where(valid, glo_s[src], 0.0), jnp.where(valid, ghi_s[src], 0.0)], axis=1)
    grp = blk_cls // N_PAIRS
    pair = blk_cls % N_PAIRS
    pair_ids = jnp.arange(N_PAIRS, dtype=i32)
    pair_pick = pair[:, None] == pair_ids[None, :]
    blk_lo = grp * EXPERTS_PER_GROUP + jnp.sum(jnp.where(pair_pick, jnp.asarray(_PAIR_LO)[None, :], 0), axis=1)
    blk_hi = grp * EXPERTS_PER_GROUP + jnp.sum(jnp.where(pair_pick, jnp.asarray(_PAIR_HI)[None, :], 0), axis=1)
    return slot_tok, blk_lo.astype(i32), blk_hi.astype(i32), n_used.reshape(1).astype(i32), slot_gates


def _tile_heads(t64):
    return jnp.concatenate([t64] * (LANES // HEAD_DIM), axis=-1)


def _rope_tables(seq):
    t = jnp.arange(seq, dtype=jnp.int32)

    def tables(pos, dim):
        inv_freq = ROPE_THETA ** (-jnp.arange(0, dim, 2, dtype=F32) / dim)
        ang = pos.astype(F32)[:, None] * inv_freq[None, :]
        return jnp.cos(ang), jnp.sin(ang)

    c1, s1 = tables(t, HEAD_DIM)
    cos_1d = _tile_heads(jnp.concatenate([c1, c1], axis=-1))
    sin_1d = _tile_heads(jnp.concatenate([-s1, s1], axis=-1))
    cr, sr = tables(t // GRID_W, HEAD_DIM // 2)
    cc, sc = tables(t % GRID_W, HEAD_DIM // 2)
    cos_ax = _tile_heads(jnp.concatenate([cr, cr, cc, cc], axis=-1))
    sin_ax = _tile_heads(jnp.concatenate([-sr, sr, -sc, sc], axis=-1))
    return (cos_1d, sin_1d), (cos_ax, sin_ax)


def _block_diag_ones():
    i = np.arange(LANES)
    return jnp.asarray((i[:, None] // HEAD_DIM) == (i[None, :] // HEAD_DIM), BF16)


def kernel(x_prompt, x_sample, mem_prompt, mem_sample, a_w_in, a_sink, a_w_out, b_w_in, b_q_norm, b_k_norm,
           b_w_out, c_w_in, c_w_out, m_w_q, m_w_kv, m_w_out, ln_g, ln_b, router_w, router_bias, e_w_gu, e_w_down):
    seq = x_prompt.shape[1]
    assert x_sample.shape[1] == seq
    x = jnp.concatenate([x_prompt, x_sample], axis=0)
    mem = jnp.concatenate([mem_prompt, mem_sample], axis=0)
    batch = x.shape[0]
    n = batch * seq
    n_mem = mem.shape[1]
    x2 = x.reshape(n, D_MODEL)

    rope_1d, rope_ax = _rope_tables(seq)
    ones_bd = _block_diag_ones()
    unit_gains = jnp.ones((2, LANES), F32)
    scale = HEAD_DIM ** -0.5

    rw = router_w.astype(F32).reshape(D_MODEL, N_EXPERT_GROUPS, EXPERTS_PER_GROUP).transpose(0, 2, 1)
    rw = jnp.pad(rw.reshape(D_MODEL, N_EXPERTS), ((0, 0), (0, LANES - N_EXPERTS)))
    rw_hi = rw.astype(BF16)
    rw_lo = (rw - rw_hi.astype(F32)).astype(BF16)
    rb = router_bias.astype(F32).reshape(N_EXPERT_GROUPS, EXPERTS_PER_GROUP).T.reshape(N_EXPERTS, 1)
    rb_t = jnp.broadcast_to(rb, (N_EXPERTS, TOKEN_TILE))

    cq, ck = C_HEADS * HEAD_DIM, C_KV_HEADS * HEAD_DIM
    n_cg = len(C_PATTERNS)

    for i in range(DEPTH):
        kind, j = i % N_MIXERS, i // N_MIXERS
        lg = lambda s_: ln_g[i, s_].astype(F32).reshape(1, D_MODEL)
        lb = lambda s_: ln_b[i, s_].astype(F32).reshape(1, D_MODEL)
        if kind == 0:
            nq, nk = A_HEADS * HEAD_DIM, A_KV_HEADS * HEAD_DIM
            q, k, v = _qkv_project(x2, a_w_in[j].astype(BF16), rope_1d[0], rope_1d[1], unit_gains, ones_bd, n=n,
                                   nq=nq, nk=nk, nv=nk, half=HEAD_DIM // 2, qk_norm=False,
                                   q_scale=scale * LOG2_E, seq=seq)
            o, = _banded_attention(q, k, v, a_sink[j].astype(F32), batch=batch, seq=seq, dilation=1, group=0,
                                   n_groups=1, n_q_heads=A_HEADS, n_kv_heads=A_KV_HEADS, window=A_WINDOW,
                                   tq=BANDED_QUERY_BLOCK, with_lse=False)
            x2 = _outproj_ln([o], [], a_w_out[j].astype(BF16), x2, lg(0), lb(0))
        elif kind == 1:
            nq, nk = B_HEADS * HEAD_DIM, B_KV_HEADS * HEAD_DIM
            gains = jnp.stack([_tile_heads(b_q_norm[j].astype(F32)), _tile_heads(b_k_norm[j].astype(F32))], axis=0)
            q, k, v = _qkv_project(x2, b_w_in[j].astype(BF16), rope_ax[0], rope_ax[1], gains, ones_bd, n=n,
                                   nq=nq, nk=nk, nv=nk, half=HEAD_DIM // 4, qk_norm=True,
                                   q_scale=scale * LOG2_E, seq=seq, v_transposed=True)
            o = _flash_attention(q, k, v, batch=batch, seq=seq, n_q_heads=B_HEADS, n_kv_heads=B_KV_HEADS,
                                 tq=256, tk=1024)
            x2 = _outproj_ln([o], [], b_w_out[j].astype(BF16), x2, lg(0), lb(0))
        else:
            wc = c_w_in[j].reshape(D_MODEL, n_cg, cq + 2 * ck)
            wc = jnp.concatenate([wc[:, :, :cq].reshape(D_MODEL, n_cg * cq),
                                  wc[:, :, cq:cq + ck].reshape(D_MODEL, n_cg * ck),
                                  wc[:, :, cq + ck:].reshape(D_MODEL, n_cg * ck)], axis=1).astype(BF16)
            q, k, v = _qkv_project(x2, wc, rope_1d[0], rope_1d[1], unit_gains, ones_bd, n=n,
                                   nq=n_cg * cq, nk=n_cg * ck, nv=n_cg * ck, half=HEAD_DIM // 2, qk_norm=False,
                                   q_scale=scale * LOG2_E, seq=seq)
            no_sink = jnp.full((C_HEADS,), NEG_INF, F32)
            os_, lses = [], []
            for gi, (window, dilation) in enumerate(C_PATTERNS):
                half_w = (window // 2) // dilation
                o, lse = _banded_attention(q, k, v, no_sink, batch=batch, seq=seq, dilation=dilation, group=gi,
                                           n_groups=n_cg, n_q_heads=C_HEADS, n_kv_heads=C_KV_HEADS,
                                           window=half_w, tq=BANDED_QUERY_BLOCK, with_lse=True)
                os_.append(o)
                lses.append(lse)
            x2 = _outproj_ln(os_, lses, c_w_out[j].astype(BF16), x2, lg(0), lb(0))

        kv = _matmul(mem.reshape(batch * n_mem, D_MODEL), m_w_kv[i].astype(BF16), BF16, tm=n_mem)
        x2 = _memory_attention(x2, kv.reshape(batch, n_mem, 2 * D_MODEL),
                               m_w_q[i].astype(BF16), m_w_out[i].astype(BF16), lg(1), lb(1), seq=seq)

        cls, g_lo, g_hi = _router(x2, rw_hi, rw_lo, rb_t)
        slot_tok, blk_lo, blk_hi, n_used, slot_gates = _dispatch_tables(cls, g_lo, g_hi)
        x2 = _moe_experts(x2, slot_tok, blk_lo, blk_hi, n_used, slot_gates,
                          e_w_gu[i].astype(BF16), e_w_down[i].astype(BF16), lg(2), lb(2), n=n)

    n_prompt = x_prompt.shape[0] * seq
    return (x2[:n_prompt].reshape(x_prompt.shape), x2[n_prompt:n].reshape(x_sample.shape))
```

```python
import functools

import jax
import jax.numpy as jnp
import numpy as np
from jax import lax
from jax.experimental import pallas as pl
from jax.experimental.pallas import tpu as pltpu

F32 = jnp.float32
BF16 = jnp.bfloat16

D_MODEL = 1024
DEPTH = 4
N_MIXERS = 3
HEAD_DIM = 64
ROPE_THETA = 10000.0
GRID_W = 64

A_HEADS, A_KV_HEADS, A_WINDOW = 16, 4, 128
B_HEADS, B_KV_HEADS = 16, 4
C_PATTERNS = ((128, 1), (512, 4), (2048, 16))
C_HEADS, C_KV_HEADS = 8, 2

MEM_HEADS = 4
MEM_HEAD_DIM = D_MODEL // MEM_HEADS

N_EXPERTS = 32
N_EXPERT_GROUPS = 8
EXPERTS_PER_GROUP = 4
D_EXPERT = D_MODEL // 4
N_PAIRS = 6
N_CLASSES = N_EXPERT_GROUPS * N_PAIRS
MOE_BLOCK = 128

LN_EPS = 1e-5
QK_NORM_EPS = 1e-6
DEEPNORM_ALPHA = (2 * DEPTH) ** 0.25

LANES = 128
TOKEN_TILE = 512
BANDED_QUERY_BLOCK = 256
FLASH_CHUNKS_PER_BODY = 4
VMEM_LIMIT_BYTES = 48 * 1024 * 1024

NEG_INF = float("-inf")
LOG2_E = 1.4426950408889634
LN_2 = 0.6931471805599453


def _params(*semantics):
    return pltpu.CompilerParams(dimension_semantics=semantics, vmem_limit_bytes=VMEM_LIMIT_BYTES)


def _layer_norm(z, g, b):
    mu = jnp.mean(z, axis=-1, keepdims=True)
    zc = z - mu
    var = jnp.mean(zc * zc, axis=-1, keepdims=True)
    return zc * lax.rsqrt(var + LN_EPS) * g + b


def _dot(a, b):
    return jnp.dot(a, b, preferred_element_type=F32)


def _dot_nt(a, b):
    return lax.dot_general(a, b, (((1,), (1,)), ((), ())), preferred_element_type=F32)


def _qkv_kernel(x_ref, w_ref, cos_ref, sin_ref, gain_ref, ones_ref, q_ref, k_ref, v_ref, *,
                nq, nk, half, qk_norm, q_scale, v_transposed):
    acc = _dot(x_ref[...].astype(BF16), w_ref[...])
    cos = cos_ref[...]
    sin = sin_ref[...]
    lane = lax.broadcasted_iota(jnp.int32, cos.shape, 1)
    first = (lane % (2 * half)) < half

    def rotate(a, gain, scale):
        if qk_norm:
            a2 = a * a
            hi = a2.astype(BF16)
            lo = (a2 - hi.astype(F32)).astype(BF16)
            ms = (_dot(hi, ones_ref[...]) + _dot(lo, ones_ref[...])) * (1.0 / HEAD_DIM)
            a = a * lax.rsqrt(ms + QK_NORM_EPS) * gain
        partner = jnp.where(first, pltpu.roll(a, LANES - half, 1), pltpu.roll(a, half, 1))
        r = a * cos + partner * sin
        if scale != 1.0:
            r = r * scale
        return r.astype(BF16)

    gq = gain_ref[0:1, :]
    gk = gain_ref[1:2, :]
    for c in range(nq // LANES):
        q_ref[:, c * LANES:(c + 1) * LANES] = rotate(acc[:, c * LANES:(c + 1) * LANES], gq, q_scale)
    for c in range(nk // LANES):
        k_ref[:, c * LANES:(c + 1) * LANES] = rotate(acc[:, nq + c * LANES:nq + (c + 1) * LANES], gk, 1.0)
    if v_transposed:
        v_ref[0] = acc[:, nq + nk:].T.astype(BF16)
    else:
        v_ref[...] = acc[:, nq + nk:].astype(BF16)


def _qkv_project(x2, w, cos, sin, gains, ones_bd, *, n, nq, nk, nv, half, qk_norm, q_scale, seq,
                 v_transposed=False):
    d = x2.shape[1]
    tm = TOKEN_TILE
    tiles_per_seq = seq // tm
    kern = functools.partial(_qkv_kernel, nq=nq, nk=nk, half=half, qk_norm=qk_norm, q_scale=q_scale,
                             v_transposed=v_transposed)
    if v_transposed:
        v_spec = pl.BlockSpec((1, nv, tm), lambda i: (i, 0, 0))
        v_shape = jax.ShapeDtypeStruct((n // tm, nv, tm), BF16)
    else:
        v_spec = pl.BlockSpec((tm, nv), lambda i: (i, 0))
        v_shape = jax.ShapeDtypeStruct((n, nv), BF16)
    return pl.pallas_call(
        kern,
        grid=(n // tm,),
        in_specs=[
            pl.BlockSpec((tm, d), lambda i: (i, 0)),
            pl.BlockSpec((d, nq + nk + nv), lambda i: (0, 0)),
            pl.BlockSpec((tm, LANES), lambda i: (i % tiles_per_seq, 0)),
            pl.BlockSpec((tm, LANES), lambda i: (i % tiles_per_seq, 0)),
            pl.BlockSpec((2, LANES), lambda i: (0, 0)),
            pl.BlockSpec((LANES, LANES), lambda i: (0, 0)),
        ],
        out_specs=[
            pl.BlockSpec((tm, nq), lambda i: (i, 0)),
            pl.BlockSpec((tm, nk), lambda i: (i, 0)),
            v_spec,
        ],
        out_shape=[
            jax.ShapeDtypeStruct((n, nq), BF16),
            jax.ShapeDtypeStruct((n, nk), BF16),
            v_shape,
        ],
        compiler_params=_params("parallel"),
        name="qkv_project",
    )(x2, w, cos, sin, gains, ones_bd)


def _qkv_dilated_kernel(x_ref, w_ref, cos_ref, sin_ref, *refs, dilations, cq, ck, q_scale):
    n_g = len(dilations)
    out_refs, acc_ref = refs[:3 * n_g], refs[3 * n_g]
    acc = _dot(x_ref[...].astype(BF16), w_ref[...])
    for c in range(acc_ref.shape[0]):
        acc_ref[c] = acc[:, c * LANES:(c + 1) * LANES]
    tm = acc_ref.shape[1]
    half = HEAD_DIM // 2
    gw = cq + 2 * ck

    for p, d in enumerate(dilations):
        q_ref, k_ref, v_ref = out_refs[3 * p:3 * p + 3]
        lane = lax.broadcasted_iota(jnp.int32, (tm // d, LANES), 1)
        first = (lane % (2 * half)) < half
        for r in range(d):
            rows = slice(None) if d == 1 else pl.ds(r, tm // d, stride=d)
            cos = cos_ref[rows, :]
            sin = sin_ref[rows, :]

            def put(ref, c, val):
                if d == 1:
                    ref[:, c * LANES:(c + 1) * LANES] = val.astype(BF16)
                else:
                    ref[0, r, :, c * LANES:(c + 1) * LANES] = val.astype(BF16)

            def rotate(a, scale):
                partner = jnp.where(first, pltpu.roll(a, LANES - half, 1), pltpu.roll(a, half, 1))
                return (a * cos + partner * sin) * scale

            for c in range(cq // LANES):
                put(q_ref, c, rotate(acc_ref[(p * gw) // LANES + c, rows, :], q_scale))
            for c in range(ck // LANES):
                put(k_ref, c, rotate(acc_ref[(p * gw + cq) // LANES + c, rows, :], 1.0))
                put(v_ref, c, acc_ref[(p * gw + cq + ck) // LANES + c, rows, :])


def _qkv_project_dilated(x2, w, cos, sin, *, n, seq, dilations, cq, ck, q_scale):
    d_model = x2.shape[1]
    tm = TOKEN_TILE
    tiles = seq // tm
    out_specs, out_shape = [], []
    for d in dilations:
        for cols in (cq, ck, ck):
            if d == 1:
                out_specs.append(pl.BlockSpec((tm, cols), lambda i: (i, 0)))
                out_shape.append(jax.ShapeDtypeStruct((n, cols), BF16))
            else:
                out_specs.append(pl.BlockSpec((1, d, tm // d, cols), lambda i: (i // tiles, 0, i % tiles, 0)))
                out_shape.append(jax.ShapeDtypeStruct((n // seq, d, seq // d, cols), BF16))
    kern = functools.partial(_qkv_dilated_kernel, dilations=tuple(dilations), cq=cq, ck=ck, q_scale=q_scale)
    res = pl.pallas_call(
        kern,
        grid=(n // tm,),
        in_specs=[
            pl.BlockSpec((tm, d_model), lambda i: (i, 0)),
            pl.BlockSpec(w.shape, lambda i: (0, 0)),
            pl.BlockSpec((tm, LANES), lambda i: (i % tiles, 0)),
            pl.BlockSpec((tm, LANES), lambda i: (i % tiles, 0)),
        ],
        out_specs=out_specs,
        out_shape=out_shape,
        scratch_shapes=[pltpu.VMEM((w.shape[1] // LANES, tm, LANES), F32)],
        compiler_params=_params("parallel"),
        name="qkv_project_dilated",
    )(x2, w, cos, sin)
    return [tuple(res[3 * p:3 * p + 3]) for p in range(len(dilations))]


def _banded_kernel(sink_ref, band_ref, q_ref, kp_ref, kc_ref, kn_ref, vp_ref, vc_ref, vn_ref, *out_refs,
                   tq, w, n_q_heads, n_kv_heads, with_lse):
    o_ref = out_refs[0]
    j = pl.program_id(1)
    g_per = n_q_heads // n_kv_heads
    span = tq + 2 * w
    m_lanes = g_per * tq
    key = lax.broadcasted_iota(jnp.int32, (span, 1), 0)
    off_seq = ((key < w) & (j == 0)) | ((key >= w + tq) & (j == pl.num_programs(1) - 1))
    bias = band_ref[...] + jnp.where(off_seq, NEG_INF, 0.0)
    lane_row = lax.broadcasted_iota(jnp.int32, (1, m_lanes), 1)
    half_rows = lax.broadcasted_iota(jnp.int32, (2 * HEAD_DIM, m_lanes), 0) >= HEAD_DIM
    ones = jnp.ones((HEAD_DIM, span), BF16)

    vcat = jnp.concatenate([vp_ref[0], vc_ref[0], vn_ref[0]], axis=0).astype(F32)
    vt_all = jnp.concatenate([vcat[:, c * LANES:(c + 1) * LANES].T for c in range(vcat.shape[1] // LANES)],
                             axis=0).astype(BF16)

    def scores(kh):
        cs = slice(kh * HEAD_DIM, (kh + 1) * HEAD_DIM)
        kcat = jnp.concatenate([kp_ref[0, :, cs], kc_ref[0, :, cs], kn_ref[0, :, cs]], axis=0)
        qs = jnp.concatenate(
            [q_ref[0, :, (kh * g_per + g) * HEAD_DIM:(kh * g_per + g + 1) * HEAD_DIM] for g in range(g_per)],
            axis=0)
        return _dot_nt(kcat, qs) + bias

    outs, lses = [], []
    st_next = scores(0)
    for kh in range(n_kv_heads):
        cs = slice(kh * HEAD_DIM, (kh + 1) * HEAD_DIM)
        st = st_next
        if kh + 1 < n_kv_heads:
            st_next = scores(kh + 1)
        sink = jnp.full((1, m_lanes), sink_ref[kh * g_per], F32)
        for g in range(1, g_per):
            sink = jnp.where(lane_row >= g * tq, sink_ref[kh * g_per + g], sink)
        sink = sink * LOG2_E
        m = jnp.maximum(jnp.max(st, axis=0, keepdims=True), sink)
        p = jnp.exp2(st - m).astype(BF16)
        vext = jnp.concatenate([vt_all[cs, :], ones], axis=0)
        acc = _dot(vext, p)
        acc = jnp.where(half_rows, acc + jnp.exp2(sink - m), acc)
        num, den = acc[:HEAD_DIM], acc[HEAD_DIM:]
        low = jnp.log(den) + m * LN_2 if with_lse else den
        res = jnp.concatenate([num / den, low], axis=0).T
        for g in range(g_per):
            outs.append(res[g * tq:(g + 1) * tq, :HEAD_DIM])
            if with_lse:
                lses.append(res[g * tq:(g + 1) * tq, HEAD_DIM:])
    o_ref[0] = jnp.concatenate(outs, axis=-1).astype(o_ref.dtype)
    if with_lse:
        out_refs[1][0] = jnp.concatenate(lses, axis=-1)


def _banded_attention(q, k, v, sink, *, batch, seq, n_q_heads, n_kv_heads, window, tq, with_lse):
    qw = n_q_heads * HEAD_DIM
    kw = n_kv_heads * HEAD_DIM
    w = window
    r_blocks = tq // w
    nb_w = seq // w
    q3 = q.reshape(batch, seq, qw)
    k3 = k.reshape(batch, seq, kw)
    v3 = v.reshape(batch, seq, kw)

    def q_map(b, j):
        return (b, j, 0)

    def prev_map(b, j):
        return (b, jnp.maximum(j * r_blocks - 1, 0), 0)

    def next_map(b, j):
        return (b, jnp.minimum((j + 1) * r_blocks, nb_w - 1), 0)

    g_per = n_q_heads // n_kv_heads
    rel = np.arange(tq + 2 * w)[:, None] - w - (np.arange(g_per * tq) % tq)[None, :]
    band = jnp.asarray(np.where(np.abs(rel) <= w, 0.0, NEG_INF), F32)

    kern = functools.partial(_banded_kernel, tq=tq, w=w, n_q_heads=n_q_heads,
                             n_kv_heads=n_kv_heads, with_lse=with_lse)
    out_specs = [pl.BlockSpec((1, tq, qw), q_map)]
    out_shape = [jax.ShapeDtypeStruct((batch, seq, qw), BF16)]
    if with_lse:
        out_specs.append(pl.BlockSpec((1, tq, qw), q_map))
        out_shape.append(jax.ShapeDtypeStruct((batch, seq, qw), F32))
    res = pl.pallas_call(
        kern,
        grid=(batch, seq // tq),
        in_specs=[
            pl.BlockSpec(memory_space=pltpu.SMEM),
            pl.BlockSpec(band.shape, lambda b, j: (0, 0)),
            pl.BlockSpec((1, tq, qw), q_map),
            pl.BlockSpec((1, w, kw), prev_map),
            pl.BlockSpec((1, tq, kw), q_map),
            pl.BlockSpec((1, w, kw), next_map),
            pl.BlockSpec((1, w, kw), prev_map),
            pl.BlockSpec((1, tq, kw), q_map),
            pl.BlockSpec((1, w, kw), next_map),
        ],
        out_specs=out_specs,
        out_shape=out_shape,
        compiler_params=_params("parallel", "parallel"),
        name="banded_attention",
    )(sink, band, q3, k3, k3, k3, v3, v3, v3)
    return [r.reshape(batch * seq, qw) for r in res]


def _flash_kernel(q_ref, k_ref, vt_ref, o_ref, st_a, st_b, *, tq, tk, tv, seq, n_kv_heads, g_per):
    ones = jnp.ones((HEAD_DIM, tk), BF16)
    outs = [None] * (n_kv_heads * g_per)
    for kh in range(n_kv_heads):
        cs = slice(kh * HEAD_DIM, (kh + 1) * HEAD_DIM)
        qs = jnp.concatenate(
            [q_ref[0, :, (kh * g_per + g) * HEAD_DIM:(kh * g_per + g + 1) * HEAD_DIM] for g in range(g_per)],
            axis=0)

        def scores(c, st_ref):
            start = pl.multiple_of(c * tk, tk)
            st_ref[...] = _dot_nt(k_ref[0, pl.ds(start, tk), cs], qs)

        def update(c, st_ref, m, acc):
            vt = jnp.concatenate([vt_ref[c * (tk // tv) + t, cs, :] for t in range(tk // tv)], axis=-1)
            vext = jnp.concatenate([vt, ones], axis=0)
            st = st_ref[...]
            m_new = jnp.maximum(m, jnp.max(st, axis=0, keepdims=True))
            alpha = jnp.exp2(m - m_new)
            p = jnp.exp2(st - m_new).astype(BF16)
            return m_new, alpha * acc + _dot(vext, p)

        bufs = (st_a, st_b)

        def group(cg, carry, last):
            m, acc = carry
            for t in range(FLASH_CHUNKS_PER_BODY):
                c = cg * FLASH_CHUNKS_PER_BODY + t
                if not (last and t == FLASH_CHUNKS_PER_BODY - 1):
                    scores(c + 1, bufs[(t + 1) % 2])
                m, acc = update(c, bufs[t % 2], m, acc)
            return m, acc

        n_groups = seq // (FLASH_CHUNKS_PER_BODY * tk)
        m0 = jnp.full((1, g_per * tq), NEG_INF, F32)
        acc0 = jnp.zeros((2 * HEAD_DIM, g_per * tq), F32)
        scores(0, st_a)
        carry = lax.fori_loop(0, n_groups - 1, lambda cg, cr: group(cg, cr, False), (m0, acc0))
        _, acc = group(n_groups - 1, carry, True)
        acc = acc.T
        o = acc[:, :HEAD_DIM] / acc[:, HEAD_DIM:]
        for g in range(g_per):
            outs[kh * g_per + g] = o[g * tq:(g + 1) * tq]
    o_ref[0] = jnp.concatenate(outs, axis=-1).astype(o_ref.dtype)


def _flash_attention(q, k, vt, *, batch, seq, n_q_heads, n_kv_heads, tq, tk):
    qw = n_q_heads * HEAD_DIM
    kw = n_kv_heads * HEAD_DIM
    tv = vt.shape[2]
    chunks = seq // tv
    kern = functools.partial(_flash_kernel, tq=tq, tk=tk, tv=tv, seq=seq, n_kv_heads=n_kv_heads,
                             g_per=n_q_heads // n_kv_heads)
    out = pl.pallas_call(
        kern,
        grid=(batch, seq // tq),
        in_specs=[
            pl.BlockSpec((1, tq, qw), lambda b, j: (b, j, 0)),
            pl.BlockSpec((1, seq, kw), lambda b, j: (b, 0, 0)),
            pl.BlockSpec((chunks, kw, tv), lambda b, j: (b, 0, 0)),
        ],
        out_specs=pl.BlockSpec((1, tq, qw), lambda b, j: (b, j, 0)),
        out_shape=jax.ShapeDtypeStruct((batch, seq, qw), BF16),
        scratch_shapes=[pltpu.VMEM((tk, (n_q_heads // n_kv_heads) * tq), F32) for _ in range(2)],
        compiler_params=_params("parallel", "arbitrary"),
        name="dense_attention",
    )(q.reshape(batch, seq, qw), k.reshape(batch, seq, kw), vt)
    return out.reshape(batch * seq, qw)


def _outproj_kernel(o_ref, w_ref, x_ref, g_ref, b_ref, y_ref):
    h = _dot(o_ref[...], w_ref[...])
    y_ref[...] = _layer_norm(DEEPNORM_ALPHA * x_ref[...] + h, g_ref[...], b_ref[...])


def _merge_outproj_kernel(*refs, dilations):
    n_g = len(dilations)
    o_refs, l_refs = refs[:n_g], refs[n_g:2 * n_g]
    w_ref, x_ref, g_ref, b_ref, y_ref = refs[2 * n_g:2 * n_g + 5]
    stage = refs[2 * n_g + 5:]
    tm = y_ref.shape[0]

    def token_order(ref, d, buf):
        if d == 1:
            return ref[...].astype(F32)
        for r in range(d):
            rows = ref[0, r].astype(F32)
            for c in range(buf.shape[0]):
                buf[c, pl.ds(r, tm // d, stride=d), :] = rows[:, c * LANES:(c + 1) * LANES]
        return jnp.concatenate([buf[c] for c in range(buf.shape[0])], axis=-1)

    os_, ls_, used = [], [], 0
    for p, d in enumerate(dilations):
        os_.append(token_order(o_refs[p], d, stage[used] if d > 1 else None))
        ls_.append(token_order(l_refs[p], d, stage[used + 1] if d > 1 else None))
        used += 2 if d > 1 else 0
    m = functools.reduce(jnp.maximum, ls_)
    es = [jnp.exp(l - m) for l in ls_]
    o = sum(e * o_p for e, o_p in zip(es, os_)) / sum(es)
    h = _dot(o.astype(BF16), w_ref[...])
    y_ref[...] = _layer_norm(DEEPNORM_ALPHA * x_ref[...] + h, g_ref[...], b_ref[...])


def _outproj_ln(os_, lses, w, x2, g, b, *, n, seq, dilations=None):
    d_model = x2.shape[1]
    tm = TOKEN_TILE
    tiles = seq // tm
    kdim = w.shape[0]
    row = lambda i: (i, 0)
    fixed = lambda i: (0, 0)
    acts = list(os_) + list(lses)
    scratch = []
    if lses:
        kern = functools.partial(_merge_outproj_kernel, dilations=tuple(dilations))
        act_specs = []
        for a, d in zip(acts, list(dilations) * 2):
            if d == 1:
                act_specs.append(pl.BlockSpec((tm, kdim), row))
            else:
                act_specs.append(pl.BlockSpec((1, d, tm // d, kdim), lambda i: (i // tiles, 0, i % tiles, 0)))
        scratch = [pltpu.VMEM((kdim // LANES, tm, LANES), F32) for d in dilations if d > 1 for _ in range(2)]
    else:
        kern = _outproj_kernel
        act_specs = [pl.BlockSpec((tm, kdim), row)]
    return pl.pallas_call(
        kern,
        grid=(n // tm,),
        in_specs=act_specs + [
            pl.BlockSpec((kdim, d_model), fixed),
            pl.BlockSpec((tm, d_model), row),
            pl.BlockSpec((1, d_model), fixed),
            pl.BlockSpec((1, d_model), fixed),
        ],
        out_specs=pl.BlockSpec((tm, d_model), row),
        out_shape=jax.ShapeDtypeStruct((n, d_model), F32),
        scratch_shapes=scratch,
        compiler_params=_params("parallel"),
        name="outproj_ln",
    )(*acts, w, x2, g, b)


def _matmul_kernel(x_ref, w_ref, o_ref):
    o_ref[...] = _dot(x_ref[...].astype(BF16), w_ref[...]).astype(o_ref.dtype)


def _matmul(x2, w, out_dtype, tm):
    n, kdim = x2.shape
    cols = w.shape[1]
    return pl.pallas_call(
        _matmul_kernel,
        grid=(n // tm,),
        in_specs=[pl.BlockSpec((tm, kdim), lambda i: (i, 0)), pl.BlockSpec((kdim, cols), lambda i: (0, 0))],
        out_specs=pl.BlockSpec((tm, cols), lambda i: (i, 0)),
        out_shape=jax.ShapeDtypeStruct((n, cols), out_dtype),
        compiler_params=_params("parallel"),
        name="matmul",
    )(x2, w)


def _memattn_kernel(x_ref, wq_ref, kv_ref, wo_ref, g_ref, b_ref, rwh_ref, rwl_ref, rb_ref,
                    y_ref, cls_ref, gate_ref):
    x = x_ref[...]
    q = (_dot(x.astype(BF16), wq_ref[...]) * (MEM_HEAD_DIM ** -0.5)).astype(BF16)
    outs = []
    for h in range(MEM_HEADS):
        cs = slice(h * MEM_HEAD_DIM, (h + 1) * MEM_HEAD_DIM)
        kh = kv_ref[0, :, cs]
        vh = kv_ref[0, :, D_MODEL + h * MEM_HEAD_DIM:D_MODEL + (h + 1) * MEM_HEAD_DIM]
        s = _dot_nt(q[:, cs], kh)
        m = jnp.max(s, axis=-1, keepdims=True)
        p = jnp.exp(s - m)
        denom = jnp.sum(p, axis=-1, keepdims=True)
        outs.append(_dot(p.astype(BF16), vh) / denom)
    o = jnp.concatenate(outs, axis=-1).astype(BF16)
    h_out = _dot(o, wo_ref[...])
    y = _layer_norm(DEEPNORM_ALPHA * x + h_out, g_ref[...], b_ref[...])
    y_ref[...] = y
    cls, gates = _route(y, rwh_ref, rwl_ref, rb_ref)
    cls_ref[0] = cls
    gate_ref[0] = gates


def _memory_attention(x2, kv, wq, wo, g, b, rw_hi, rw_lo, rb_t, *, seq):
    d = x2.shape[1]
    batch, n_mem = kv.shape[0], kv.shape[1]
    tm = TOKEN_TILE
    tiles = seq // tm
    n = batch * seq
    fixed = lambda bi, i: (0, 0)
    y, cls, gates = pl.pallas_call(
        _memattn_kernel,
        grid=(batch, tiles),
        in_specs=[
            pl.BlockSpec((tm, d), lambda bi, i: (bi * tiles + i, 0)),
            pl.BlockSpec((d, d), fixed),
            pl.BlockSpec((1, n_mem, 2 * d), lambda bi, i: (bi, 0, 0)),
            pl.BlockSpec((d, d), fixed),
            pl.BlockSpec((1, d), fixed),
            pl.BlockSpec((1, d), fixed),
            pl.BlockSpec((d, LANES), fixed),
            pl.BlockSpec((d, LANES), fixed),
            pl.BlockSpec((N_EXPERTS, tm), fixed),
        ],
        out_specs=[
            pl.BlockSpec((tm, d), lambda bi, i: (bi * tiles + i, 0)),
            pl.BlockSpec((1, 1, tm), lambda bi, i: (bi * tiles + i, 0, 0)),
            pl.BlockSpec((1, 2, tm), lambda bi, i: (bi * tiles + i, 0, 0)),
        ],
        out_shape=[
            jax.ShapeDtypeStruct((n, d), F32),
            jax.ShapeDtypeStruct((n // tm, 1, tm), jnp.int32),
            jax.ShapeDtypeStruct((n // tm, 2, tm), F32),
        ],
        compiler_params=_params("parallel", "parallel"),
        name="memory_attention",
    )(x2, wq, kv, wo, g, b, rw_hi, rw_lo, rb_t)
    return y, cls.reshape(n), gates[:, 0, :].reshape(n), gates[:, 1, :].reshape(n)


def _route(x, wh_ref, wl_ref, bias_ref):
    xh = x.astype(BF16)
    xl = (x - xh.astype(F32)).astype(BF16)
    logits = _dot(xh, wh_ref[...]) + _dot(xl, wh_ref[...]) + _dot(xh, wl_ref[...])
    lt = logits.T[:N_EXPERTS]
    scores = 1.0 / (1.0 + jnp.exp(-lt))
    biased = scores + bias_ref[...]
    ng = N_EXPERT_GROUPS
    a = [biased[l * ng:(l + 1) * ng] for l in range(EXPERTS_PER_GROUP)]
    sc = [scores[l * ng:(l + 1) * ng] for l in range(EXPERTS_PER_GROUP)]
    gs = None
    for i in range(EXPERTS_PER_GROUP):
        for k in range(i + 1, EXPERTS_PER_GROUP):
            pair = a[i] + a[k]
            gs = pair if gs is None else jnp.maximum(gs, pair)
    gidx = lax.broadcasted_iota(jnp.int32, gs.shape, 0).astype(F32)
    gmax = jnp.max(gs, axis=0, keepdims=True)
    grp_f = jnp.min(jnp.where(gs == gmax, gidx, float(ng)), axis=0, keepdims=True)
    sel = gidx == grp_f
    grp = grp_f.astype(jnp.int32)
    v = [jnp.sum(jnp.where(sel, a[l], 0.0), axis=0, keepdims=True) for l in range(EXPERTS_PER_GROUP)]
    sv = [jnp.sum(jnp.where(sel, sc[l], 0.0), axis=0, keepdims=True) for l in range(EXPERTS_PER_GROUP)]

    def first_argmax(vals, excluded):
        best = None
        for l, val in enumerate(vals):
            cand = val if excluded is None else jnp.where(excluded == l, NEG_INF, val)
            best = cand if best is None else jnp.maximum(best, cand)
        idx = jnp.full(best.shape, EXPERTS_PER_GROUP - 1, jnp.int32)
        for l in range(EXPERTS_PER_GROUP - 1, -1, -1):
            hit = vals[l] == best
            if excluded is not None:
                hit = hit & (excluded != l)
            idx = jnp.where(hit, l, idx)
        return idx

    i1 = first_argmax(v, None)
    i2 = first_argmax(v, i1)

    def pick(vals, idx):
        out = vals[0]
        for l in range(1, EXPERTS_PER_GROUP):
            out = jnp.where(idx == l, vals[l], out)
        return out

    s1 = pick(sv, i1)
    s2 = pick(sv, i2)
    tot = s1 + s2
    lo = jnp.minimum(i1, i2)
    hi = jnp.maximum(i1, i2)
    first_is_lo = i1 < i2
    g_lo = jnp.where(first_is_lo, s1, s2) / tot
    g_hi = jnp.where(first_is_lo, s2, s1) / tot
    pair_idx = jnp.right_shift(lo * (7 - lo), 1) + (hi - lo - 1)
    return grp * N_PAIRS + pair_idx, jnp.concatenate([g_lo, g_hi], axis=0)


def _moe_kernel(tok_ref, lo_ref, hi_ref, nblk_ref,
                x_hbm, gate_ref, gu_lo_ref, gu_hi_ref, dn_lo_ref, dn_hi_ref, g_ref, b_ref,
                y_hbm, xbuf, ybuf, gsem, ssem, *, n_tokens):
    blk = pl.program_id(0)
    n_used = nblk_ref[0]
    rows = MOE_BLOCK

    def gather_copy(block, i, slot):
        tok = 0 if block is None else jnp.maximum(tok_ref[block * rows + i], 0)
        return pltpu.make_async_copy(x_hbm.at[pl.ds(tok, 1)], xbuf.at[slot, pl.ds(i, 1)], gsem.at[slot])

    def scatter_copy(block, i, slot):
        if block is None:
            dst = 0
        else:
            tok = tok_ref[block * rows + i]
            dst = jnp.where(tok < 0, n_tokens + slot * rows + i, tok)
        return pltpu.make_async_copy(ybuf.at[slot, pl.ds(i, 1)], y_hbm.at[pl.ds(dst, 1)], ssem.at[slot])

    def start_all(copy, block, slot):
        for i in range(rows):
            copy(block, i, slot).start(priority=i % 2)

    def wait_all(copy, slot):
        for i in range(rows):
            copy(None, i, slot).wait()

    @pl.when(blk < n_used)
    def _():
        slot = blk % 2
        nxt = jnp.minimum(blk + 1, n_used - 1)

        @pl.when(blk == 0)
        def _():
            start_all(gather_copy, 0, 0)
            ybuf[1] = jnp.zeros(ybuf.shape[1:], ybuf.dtype)
            for part in range(2):
                spare = pltpu.make_async_copy(ybuf.at[1], y_hbm.at[pl.ds(n_tokens + part * rows, rows)], ssem.at[1])
                spare.start()
                spare.wait()

        @pl.when(blk >= 2)
        def _():
            wait_all(scatter_copy, slot)

        wait_all(gather_copy, slot)
        start_all(gather_copy, nxt, 1 - slot)

        x = xbuf[slot]
        xb = x.astype(BF16)

        def expert(gu_ref, dn_ref):
            gu = _dot(xb, gu_ref[0])
            gate_act = gu[:, :D_EXPERT]
            act = gate_act / (1.0 + jnp.exp(-gate_act)) * gu[:, D_EXPERT:]
            return _dot(act.astype(BF16), dn_ref[0])

        r_i = lax.broadcasted_iota(jnp.int32, (rows, rows), 0)
        c_i = lax.broadcasted_iota(jnp.int32, (rows, rows), 1)
        diag = r_i == c_i
        gates = gate_ref[0]
        g_lo = jnp.sum(jnp.where(diag, gates[0:1, :], 0.0), axis=1, keepdims=True)
        g_hi = jnp.sum(jnp.where(diag, gates[1:2, :], 0.0), axis=1, keepdims=True)
        h = g_lo * expert(gu_lo_ref, dn_lo_ref) + g_hi * expert(gu_hi_ref, dn_hi_ref)
        ybuf[slot] = _layer_norm(DEEPNORM_ALPHA * x + h, g_ref[...], b_ref[...])
        start_all(scatter_copy, blk, slot)

        @pl.when(blk == n_used - 1)
        def _():
            wait_all(gather_copy, 1 - slot)

            @pl.when(blk >= 1)
            def _():
                wait_all(scatter_copy, 1 - slot)
            wait_all(scatter_copy, slot)


def _moe_experts(x2, slot_tok, blk_lo, blk_hi, n_used, slot_gates, w_gu, w_dn, g, b, *, n):
    d = x2.shape[1]
    n_blocks = blk_lo.shape[0]
    rows = MOE_BLOCK
    grid_spec = pltpu.PrefetchScalarGridSpec(
        num_scalar_prefetch=4,
        grid=(n_blocks,),
        in_specs=[
            pl.BlockSpec(memory_space=pl.ANY),
            pl.BlockSpec((1, 2, rows), lambda i, *_: (i, 0, 0)),
            pl.BlockSpec((1, d, 2 * D_EXPERT), lambda i, tok, lo, hi, nb: (lo[i], 0, 0)),
            pl.BlockSpec((1, d, 2 * D_EXPERT), lambda i, tok, lo, hi, nb: (hi[i], 0, 0)),
            pl.BlockSpec((1, D_EXPERT, d), lambda i, tok, lo, hi, nb: (lo[i], 0, 0)),
            pl.BlockSpec((1, D_EXPERT, d), lambda i, tok, lo, hi, nb: (hi[i], 0, 0)),
            pl.BlockSpec((1, d), lambda i, *_: (0, 0)),
            pl.BlockSpec((1, d), lambda i, *_: (0, 0)),
        ],
        out_specs=pl.BlockSpec(memory_space=pl.ANY),
        scratch_shapes=[
            pltpu.VMEM((2, rows, d), F32),
            pltpu.VMEM((2, rows, d), F32),
            pltpu.SemaphoreType.DMA((2,)),
            pltpu.SemaphoreType.DMA((2,)),
        ],
    )
    return pl.pallas_call(
        functools.partial(_moe_kernel, n_tokens=n),
        grid_spec=grid_spec,
        out_shape=jax.ShapeDtypeStruct((n + 2 * rows, d), F32),
        compiler_params=_params("arbitrary"),
        name="moe_experts",
    )(slot_tok, blk_lo, blk_hi, n_used, x2, slot_gates, w_gu, w_gu, w_dn, w_dn, g, b)


_PAIR_LO = np.array([0, 0, 0, 1, 1, 2], np.int32)
_PAIR_HI = np.array([1, 2, 3, 2, 3, 3], np.int32)


def _dispatch_tables(cls, g_lo, g_hi):
    n = cls.shape[0]
    rows = MOE_BLOCK
    n_blocks = n // rows + N_CLASSES
    i32 = jnp.int32
    cls_ids = jnp.arange(N_CLASSES, dtype=i32)
    _, order, glo_s, ghi_s = lax.sort((cls, jnp.arange(n, dtype=i32), g_lo, g_hi), num_keys=1)
    counts = jnp.sum((cls[:, None] == cls_ids[None, :]).astype(i32), axis=0)
    cls_blocks = (counts + rows - 1) // rows
    blk_end = jnp.cumsum(cls_blocks)
    blk_begin = blk_end - cls_blocks
    start = jnp.cumsum(counts) - counts
    n_used = blk_end[-1]
    blk = jnp.arange(n_blocks, dtype=i32)
    blk_eff = jnp.minimum(blk, n_used - 1)
    blk_cls = jnp.sum((blk_end[None, :] <= blk_eff[:, None]).astype(i32), axis=1)
    pick = blk_cls[:, None] == cls_ids[None, :]
    table = lambda t: jnp.sum(jnp.where(pick, t[None, :], 0), axis=1)
    blk_rank = (blk_eff - table(blk_begin)) * rows
    blk_nvalid = jnp.where(blk < n_used, jnp.clip(table(counts) - blk_rank, 0, rows), 0)
    row = jnp.arange(rows, dtype=i32)
    valid = row[None, :] < blk_nvalid[:, None]
    src = jnp.clip((table(start) + blk_rank)[:, None] + row[None, :], 0, n - 1)
    slot_tok = jnp.where(valid, order[src], -1).reshape(n_blocks * rows)
    slot_gates = jnp.stack([jnp.where(valid, glo_s[src], 0.0), jnp.where(valid, ghi_s[src], 0.0)], axis=1)
    grp = blk_cls // N_PAIRS
    pair = blk_cls % N_PAIRS
    pair_ids = jnp.arange(N_PAIRS, dtype=i32)
    pair_pick = pair[:, None] == pair_ids[None, :]
    blk_lo = grp * EXPERTS_PER_GROUP + jnp.sum(jnp.where(pair_pick, jnp.asarray(_PAIR_LO)[None, :], 0), axis=1)
    blk_hi = grp * EXPERTS_PER_GROUP + jnp.sum(jnp.where(pair_pick, jnp.asarray(_PAIR_HI)[None, :], 0), axis=1)
    return slot_tok, blk_lo.astype(i32), blk_hi.astype(i32), n_used.reshape(1).astype(i32), slot_gates


def _tile_heads(t64):
    return jnp.concatenate([t64] * (LANES // HEAD_DIM), axis=-1)


def _rope_tables(seq):
    t = jnp.arange(seq, dtype=jnp.int32)

    def tables(pos, dim):
        inv_freq = ROPE_THETA ** (-jnp.arange(0, dim, 2, dtype=F32) / dim)
        ang = pos.astype(F32)[:, None] * inv_freq[None, :]
        return jnp.cos(ang), jnp.sin(ang)

    c1, s1 = tables(t, HEAD_DIM)
    cos_1d = _tile_heads(jnp.concatenate([c1, c1], axis=-1))
    sin_1d = _tile_heads(jnp.concatenate([-s1, s1], axis=-1))
    cr, sr = tables(t // GRID_W, HEAD_DIM // 2)
    cc, sc = tables(t % GRID_W, HEAD_DIM // 2)
    cos_ax = _tile_heads(jnp.concatenate([cr, cr, cc, cc], axis=-1))
    sin_ax = _tile_heads(jnp.concatenate([-sr, sr, -sc, sc], axis=-1))
    return (cos_1d, sin_1d), (cos_ax, sin_ax)


def _block_diag_ones():
    i = np.arange(LANES)
    return jnp.asarray((i[:, None] // HEAD_DIM) == (i[None, :] // HEAD_DIM), BF16)


def kernel(x_prompt, x_sample, mem_prompt, mem_sample, a_w_in, a_sink, a_w_out, b_w_in, b_q_norm, b_k_norm,
           b_w_out, c_w_in, c_w_out, m_w_q, m_w_kv, m_w_out, ln_g, ln_b, router_w, router_bias, e_w_gu, e_w_down):
    seq = x_prompt.shape[1]
    assert x_sample.shape[1] == seq
    x = jnp.concatenate([x_prompt, x_sample], axis=0)
    mem = jnp.concatenate([mem_prompt, mem_sample], axis=0)
    batch = x.shape[0]
    n = batch * seq
    n_mem = mem.shape[1]
    x2 = x.reshape(n, D_MODEL)

    rope_1d, rope_ax = _rope_tables(seq)
    ones_bd = _block_diag_ones()
    unit_gains = jnp.ones((2, LANES), F32)
    scale = HEAD_DIM ** -0.5

    rw = router_w.astype(F32).reshape(D_MODEL, N_EXPERT_GROUPS, EXPERTS_PER_GROUP).transpose(0, 2, 1)
    rw = jnp.pad(rw.reshape(D_MODEL, N_EXPERTS), ((0, 0), (0, LANES - N_EXPERTS)))
    rw_hi = rw.astype(BF16)
    rw_lo = (rw - rw_hi.astype(F32)).astype(BF16)
    rb = router_bias.astype(F32).reshape(N_EXPERT_GROUPS, EXPERTS_PER_GROUP).T.reshape(N_EXPERTS, 1)
    rb_t = jnp.broadcast_to(rb, (N_EXPERTS, TOKEN_TILE))

    cq, ck = C_HEADS * HEAD_DIM, C_KV_HEADS * HEAD_DIM

    for i in range(DEPTH):
        kind, j = i % N_MIXERS, i // N_MIXERS
        lg = lambda s_: ln_g[i, s_].astype(F32).reshape(1, D_MODEL)
        lb = lambda s_: ln_b[i, s_].astype(F32).reshape(1, D_MODEL)
        if kind == 0:
            nq, nk = A_HEADS * HEAD_DIM, A_KV_HEADS * HEAD_DIM
            q, k, v = _qkv_project(x2, a_w_in[j].astype(BF16), rope_1d[0], rope_1d[1], unit_gains, ones_bd, n=n,
                                   nq=nq, nk=nk, nv=nk, half=HEAD_DIM // 2, qk_norm=False,
                                   q_scale=scale * LOG2_E, seq=seq)
            o, = _banded_attention(q, k, v, a_sink[j].astype(F32), batch=batch, seq=seq, n_q_heads=A_HEADS,
                                   n_kv_heads=A_KV_HEADS, window=A_WINDOW, tq=BANDED_QUERY_BLOCK, with_lse=False)
            x2 = _outproj_ln([o], [], a_w_out[j].astype(BF16), x2, lg(0), lb(0), n=n, seq=seq)
        elif kind == 1:
            nq, nk = B_HEADS * HEAD_DIM, B_KV_HEADS * HEAD_DIM
            gains = jnp.stack([_tile_heads(b_q_norm[j].astype(F32)), _tile_heads(b_k_norm[j].astype(F32))], axis=0)
            q, k, v = _qkv_project(x2, b_w_in[j].astype(BF16), rope_ax[0], rope_ax[1], gains, ones_bd, n=n,
                                   nq=nq, nk=nk, nv=nk, half=HEAD_DIM // 4, qk_norm=True,
                                   q_scale=scale * LOG2_E, seq=seq, v_transposed=True)
            o = _flash_attention(q, k, v, batch=batch, seq=seq, n_q_heads=B_HEADS, n_kv_heads=B_KV_HEADS,
                                 tq=256, tk=1024)
            x2 = _outproj_ln([o], [], b_w_out[j].astype(BF16), x2, lg(0), lb(0), n=n, seq=seq)
        else:
            dilations = [dil for _, dil in C_PATTERNS]
            qkv = _qkv_project_dilated(x2, c_w_in[j].astype(BF16), rope_1d[0], rope_1d[1], n=n, seq=seq,
                                       dilations=dilations, cq=cq, ck=ck, q_scale=scale * LOG2_E)
            no_sink = jnp.full((C_HEADS,), NEG_INF, F32)
            os_, lses = [], []
            for (window, dil), (q, k, v) in zip(C_PATTERNS, qkv):
                half_w = (window // 2) // dil
                o, lse = _banded_attention(q.reshape(n, cq), k.reshape(n, ck), v.reshape(n, ck), no_sink,
                                           batch=batch * dil, seq=seq // dil, n_q_heads=C_HEADS,
                                           n_kv_heads=C_KV_HEADS, window=half_w, tq=BANDED_QUERY_BLOCK,
                                           with_lse=True)
                shape = (n, cq) if dil == 1 else (batch, dil, seq // dil, cq)
                os_.append(o.reshape(shape))
                lses.append(lse.reshape(shape))
            x2 = _outproj_ln(os_, lses, c_w_out[j].astype(BF16), x2, lg(0), lb(0), n=n, seq=seq,
                             dilations=dilations)

        kv = _matmul(mem.reshape(batch * n_mem, D_MODEL), m_w_kv[i].astype(BF16), BF16, tm=n_mem)
        x2, cls, g_lo, g_hi = _memory_attention(x2, kv.reshape(batch, n_mem, 2 * D_MODEL), m_w_q[i].astype(BF16),
                                                m_w_out[i].astype(BF16), lg(1), lb(1), rw_hi, rw_lo, rb_t, seq=seq)
        slot_tok, blk_lo, blk_hi, n_used, slot_gates = _dispatch_tables(cls, g_lo, g_hi)
        x2 = _moe_experts(x2, slot_tok, blk_lo, blk_hi, n_used, slot_gates,
                          e_w_gu[i].astype(BF16), e_w_down[i].astype(BF16), lg(2), lb(2), n=n)

    n_prompt = x_prompt.shape[0] * seq
    return (x2[:n_prompt].reshape(x_prompt.shape), x2[n_prompt:n].reshape(x_sample.shape))
```

```python
import functools

import jax
import jax.numpy as jnp
import numpy as np
from jax import lax
from jax.experimental import pallas as pl
from jax.experimental.pallas import tpu as pltpu

F32 = jnp.float32
BF16 = jnp.bfloat16

D_MODEL = 1024
DEPTH = 4
N_MIXERS = 3
HEAD_DIM = 64
ROPE_THETA = 10000.0
GRID_W = 64

A_HEADS, A_KV_HEADS, A_WINDOW = 16, 4, 128
B_HEADS, B_KV_HEADS = 16, 4
C_PATTERNS = ((128, 1), (512, 4), (2048, 16))
C_HEADS, C_KV_HEADS = 8, 2

MEM_HEADS = 4
MEM_HEAD_DIM = D_MODEL // MEM_HEADS

N_EXPERTS = 32
N_EXPERT_GROUPS = 8
EXPERTS_PER_GROUP = 4
D_EXPERT = D_MODEL // 4
N_PAIRS = 6
N_CLASSES = N_EXPERT_GROUPS * N_PAIRS
MOE_BLOCK = 128

LN_EPS = 1e-5
QK_NORM_EPS = 1e-6
DEEPNORM_ALPHA = (2 * DEPTH) ** 0.25

LANES = 128
TOKEN_TILE = 512
BANDED_QUERY_BLOCK = 256
BANDED_SUB_BLOCK = 128
FLASH_CHUNKS_PER_BODY = 4
VMEM_LIMIT_BYTES = 48 * 1024 * 1024

NEG_INF = float("-inf")
LOG2_E = 1.4426950408889634
LN_2 = 0.6931471805599453


def _params(*semantics):
    return pltpu.CompilerParams(dimension_semantics=semantics, vmem_limit_bytes=VMEM_LIMIT_BYTES)


def _layer_norm(z, g, b):
    mu = jnp.mean(z, axis=-1, keepdims=True)
    zc = z - mu
    var = jnp.mean(zc * zc, axis=-1, keepdims=True)
    return zc * lax.rsqrt(var + LN_EPS) * g + b


def _dot(a, b):
    return jnp.dot(a, b, preferred_element_type=F32)


def _dot_nt(a, b):
    return lax.dot_general(a, b, (((1,), (1,)), ((), ())), preferred_element_type=F32)


def _qkv_kernel(x_ref, w_ref, cos_ref, sin_ref, gain_ref, ones_ref, q_ref, k_ref, v_ref, *,
                nq, nk, half, qk_norm, q_scale, v_transposed):
    acc = _dot(x_ref[...].astype(BF16), w_ref[...])
    cos = cos_ref[...]
    sin = sin_ref[...]
    lane = lax.broadcasted_iota(jnp.int32, cos.shape, 1)
    first = (lane % (2 * half)) < half

    def rotate(a, gain, scale):
        if qk_norm:
            a2 = a * a
            hi = a2.astype(BF16)
            lo = (a2 - hi.astype(F32)).astype(BF16)
            ms = (_dot(hi, ones_ref[...]) + _dot(lo, ones_ref[...])) * (1.0 / HEAD_DIM)
            a = a * lax.rsqrt(ms + QK_NORM_EPS) * gain
        partner = jnp.where(first, pltpu.roll(a, LANES - half, 1), pltpu.roll(a, half, 1))
        r = a * cos + partner * sin
        if scale != 1.0:
            r = r * scale
        return r.astype(BF16)

    gq = gain_ref[0:1, :]
    gk = gain_ref[1:2, :]
    for c in range(nq // LANES):
        q_ref[:, c * LANES:(c + 1) * LANES] = rotate(acc[:, c * LANES:(c + 1) * LANES], gq, q_scale)
    for c in range(nk // LANES):
        k_ref[:, c * LANES:(c + 1) * LANES] = rotate(acc[:, nq + c * LANES:nq + (c + 1) * LANES], gk, 1.0)
    if v_transposed:
        v_ref[0] = acc[:, nq + nk:].T.astype(BF16)
    else:
        v_ref[...] = acc[:, nq + nk:].astype(BF16)


def _qkv_project(x2, w, cos, sin, gains, ones_bd, *, n, nq, nk, nv, half, qk_norm, q_scale, seq,
                 v_transposed=False):
    d = x2.shape[1]
    tm = TOKEN_TILE
    tiles_per_seq = seq // tm
    kern = functools.partial(_qkv_kernel, nq=nq, nk=nk, half=half, qk_norm=qk_norm, q_scale=q_scale,
                             v_transposed=v_transposed)
    if v_transposed:
        v_spec = pl.BlockSpec((1, nv, tm), lambda i: (i, 0, 0))
        v_shape = jax.ShapeDtypeStruct((n // tm, nv, tm), BF16)
    else:
        v_spec = pl.BlockSpec((tm, nv), lambda i: (i, 0))
        v_shape = jax.ShapeDtypeStruct((n, nv), BF16)
    return pl.pallas_call(
        kern,
        grid=(n // tm,),
        in_specs=[
            pl.BlockSpec((tm, d), lambda i: (i, 0)),
            pl.BlockSpec((d, nq + nk + nv), lambda i: (0, 0)),
            pl.BlockSpec((tm, LANES), lambda i: (i % tiles_per_seq, 0)),
            pl.BlockSpec((tm, LANES), lambda i: (i % tiles_per_seq, 0)),
            pl.BlockSpec((2, LANES), lambda i: (0, 0)),
            pl.BlockSpec((LANES, LANES), lambda i: (0, 0)),
        ],
        out_specs=[
            pl.BlockSpec((tm, nq), lambda i: (i, 0)),
            pl.BlockSpec((tm, nk), lambda i: (i, 0)),
            v_spec,
        ],
        out_shape=[
            jax.ShapeDtypeStruct((n, nq), BF16),
            jax.ShapeDtypeStruct((n, nk), BF16),
            v_shape,
        ],
        compiler_params=_params("parallel"),
        name="qkv_project",
    )(x2, w, cos, sin, gains, ones_bd)


def _qkv_dilated_kernel(x_ref, w_ref, cos_ref, sin_ref, *refs, dilations, cq, ck, q_scale):
    n_g = len(dilations)
    out_refs, acc_ref = refs[:3 * n_g], refs[3 * n_g]
    acc = _dot(x_ref[...].astype(BF16), w_ref[...])
    for c in range(acc_ref.shape[0]):
        acc_ref[c] = acc[:, c * LANES:(c + 1) * LANES]
    tm = acc_ref.shape[1]
    half = HEAD_DIM // 2
    gw = cq + 2 * ck

    for p, d in enumerate(dilations):
        q_ref, k_ref, v_ref = out_refs[3 * p:3 * p + 3]
        lane = lax.broadcasted_iota(jnp.int32, (tm // d, LANES), 1)
        first = (lane % (2 * half)) < half
        for r in range(d):
            rows = slice(None) if d == 1 else pl.ds(r, tm // d, stride=d)
            cos = cos_ref[rows, :]
            sin = sin_ref[rows, :]

            def put(ref, c, val):
                if d == 1:
                    ref[:, c * LANES:(c + 1) * LANES] = val.astype(BF16)
                else:
                    ref[0, r, :, c * LANES:(c + 1) * LANES] = val.astype(BF16)

            def rotate(a, scale):
                partner = jnp.where(first, pltpu.roll(a, LANES - half, 1), pltpu.roll(a, half, 1))
                return (a * cos + partner * sin) * scale

            for c in range(cq // LANES):
                put(q_ref, c, rotate(acc_ref[(p * gw) // LANES + c, rows, :], q_scale))
            for c in range(ck // LANES):
                put(k_ref, c, rotate(acc_ref[(p * gw + cq) // LANES + c, rows, :], 1.0))
                put(v_ref, c, acc_ref[(p * gw + cq + ck) // LANES + c, rows, :])


def _qkv_project_dilated(x2, w, cos, sin, *, n, seq, dilations, cq, ck, q_scale):
    d_model = x2.shape[1]
    tm = TOKEN_TILE
    tiles = seq // tm
    out_specs, out_shape = [], []
    for d in dilations:
        for cols in (cq, ck, ck):
            if d == 1:
                out_specs.append(pl.BlockSpec((tm, cols), lambda i: (i, 0)))
                out_shape.append(jax.ShapeDtypeStruct((n, cols), BF16))
            else:
                out_specs.append(pl.BlockSpec((1, d, tm // d, cols), lambda i: (i // tiles, 0, i % tiles, 0)))
                out_shape.append(jax.ShapeDtypeStruct((n // seq, d, seq // d, cols), BF16))
    kern = functools.partial(_qkv_dilated_kernel, dilations=tuple(dilations), cq=cq, ck=ck, q_scale=q_scale)
    res = pl.pallas_call(
        kern,
        grid=(n // tm,),
        in_specs=[
            pl.BlockSpec((tm, d_model), lambda i: (i, 0)),
            pl.BlockSpec(w.shape, lambda i: (0, 0)),
            pl.BlockSpec((tm, LANES), lambda i: (i % tiles, 0)),
            pl.BlockSpec((tm, LANES), lambda i: (i % tiles, 0)),
        ],
        out_specs=out_specs,
        out_shape=out_shape,
        scratch_shapes=[pltpu.VMEM((w.shape[1] // LANES, tm, LANES), F32)],
        compiler_params=_params("parallel"),
        name="qkv_project_dilated",
    )(x2, w, cos, sin)
    return [tuple(res[3 * p:3 * p + 3]) for p in range(len(dilations))]


def _banded_kernel(sink_ref, band_ref, q_ref, kp_ref, kc_ref, kn_ref, vp_ref, vc_ref, vn_ref, *out_refs,
                   tq, sub, w, n_q_heads, n_kv_heads, with_lse):
    o_ref = out_refs[0]
    j = pl.program_id(1)
    g_per = n_q_heads // n_kv_heads
    n_sub = tq // sub
    span = tq + 2 * w
    sub_span = sub + 2 * w
    m_lanes = g_per * sub
    key = lax.broadcasted_iota(jnp.int32, (span, 1), 0)
    off_seq = ((key < w) & (j == 0)) | ((key >= w + tq) & (j == pl.num_programs(1) - 1))
    edge = jnp.where(off_seq, NEG_INF, 0.0)
    biases = [band_ref[...] + edge[sb * sub:sb * sub + sub_span] for sb in range(n_sub)]
    lane_row = lax.broadcasted_iota(jnp.int32, (1, m_lanes), 1)
    half_rows = lax.broadcasted_iota(jnp.int32, (2 * HEAD_DIM, m_lanes), 0) >= HEAD_DIM
    ones = jnp.ones((HEAD_DIM, sub_span), BF16)

    vcat = jnp.concatenate([vp_ref[0], vc_ref[0], vn_ref[0]], axis=0).astype(F32)
    vt_all = jnp.concatenate([vcat[:, c * LANES:(c + 1) * LANES].T for c in range(vcat.shape[1] // LANES)],
                             axis=0).astype(BF16)
    kcats = [jnp.concatenate([kp_ref[0, :, kh * HEAD_DIM:(kh + 1) * HEAD_DIM],
                              kc_ref[0, :, kh * HEAD_DIM:(kh + 1) * HEAD_DIM],
                              kn_ref[0, :, kh * HEAD_DIM:(kh + 1) * HEAD_DIM]], axis=0)
             for kh in range(n_kv_heads)]

    def scores(kh, sb):
        r0 = sb * sub
        qs = jnp.concatenate(
            [q_ref[0, r0:r0 + sub, (kh * g_per + g) * HEAD_DIM:(kh * g_per + g + 1) * HEAD_DIM]
             for g in range(g_per)], axis=0)
        return _dot_nt(kcats[kh][r0:r0 + sub_span], qs) + biases[sb]

    tasks = [(kh, sb) for kh in range(n_kv_heads) for sb in range(n_sub)]
    o_parts = [[None] * n_sub for _ in range(n_q_heads)]
    l_parts = [[None] * n_sub for _ in range(n_q_heads)]
    st_next = scores(*tasks[0])
    for t, (kh, sb) in enumerate(tasks):
        r0 = sb * sub
        st = st_next
        if t + 1 < len(tasks):
            st_next = scores(*tasks[t + 1])
        sink = jnp.full((1, m_lanes), sink_ref[kh * g_per], F32)
        for g in range(1, g_per):
            sink = jnp.where(lane_row >= g * sub, sink_ref[kh * g_per + g], sink)
        sink = sink * LOG2_E
        m = jnp.maximum(jnp.max(st, axis=0, keepdims=True), sink)
        p = jnp.exp2(st - m).astype(BF16)
        vext = jnp.concatenate([vt_all[kh * HEAD_DIM:(kh + 1) * HEAD_DIM, r0:r0 + sub_span], ones], axis=0)
        acc = _dot(vext, p)
        acc = jnp.where(half_rows, acc + jnp.exp2(sink - m), acc)
        num, den = acc[:HEAD_DIM], acc[HEAD_DIM:]
        low = jnp.log(den) + m * LN_2 if with_lse else den
        res = jnp.concatenate([num / den, low], axis=0).T
        for g in range(g_per):
            o_parts[kh * g_per + g][sb] = res[g * sub:(g + 1) * sub, :HEAD_DIM]
            l_parts[kh * g_per + g][sb] = res[g * sub:(g + 1) * sub, HEAD_DIM:]
    o_ref[0] = jnp.concatenate([jnp.concatenate(parts, axis=0) for parts in o_parts], axis=-1).astype(o_ref.dtype)
    if with_lse:
        out_refs[1][0] = jnp.concatenate([jnp.concatenate(parts, axis=0) for parts in l_parts], axis=-1)


def _banded_attention(q, k, v, sink, *, batch, seq, n_q_heads, n_kv_heads, window, tq, with_lse):
    qw = n_q_heads * HEAD_DIM
    kw = n_kv_heads * HEAD_DIM
    w = window
    r_blocks = tq // w
    nb_w = seq // w
    q3 = q.reshape(batch, seq, qw)
    k3 = k.reshape(batch, seq, kw)
    v3 = v.reshape(batch, seq, kw)

    def q_map(b, j):
        return (b, j, 0)

    def prev_map(b, j):
        return (b, jnp.maximum(j * r_blocks - 1, 0), 0)

    def next_map(b, j):
        return (b, jnp.minimum((j + 1) * r_blocks, nb_w - 1), 0)

    g_per = n_q_heads // n_kv_heads
    sub = BANDED_SUB_BLOCK
    rel = np.arange(sub + 2 * w)[:, None] - w - (np.arange(g_per * sub) % sub)[None, :]
    band = jnp.asarray(np.where(np.abs(rel) <= w, 0.0, NEG_INF), F32)

    kern = functools.partial(_banded_kernel, tq=tq, sub=sub, w=w, n_q_heads=n_q_heads,
                             n_kv_heads=n_kv_heads, with_lse=with_lse)
    out_specs = [pl.BlockSpec((1, tq, qw), q_map)]
    out_shape = [jax.ShapeDtypeStruct((batch, seq, qw), BF16)]
    if with_lse:
        out_specs.append(pl.BlockSpec((1, tq, qw), q_map))
        out_shape.append(jax.ShapeDtypeStruct((batch, seq, qw), F32))
    res = pl.pallas_call(
        kern,
        grid=(batch, seq // tq),
        in_specs=[
            pl.BlockSpec(memory_space=pltpu.SMEM),
            pl.BlockSpec(band.shape, lambda b, j: (0, 0)),
            pl.BlockSpec((1, tq, qw), q_map),
            pl.BlockSpec((1, w, kw), prev_map),
            pl.BlockSpec((1, tq, kw), q_map),
            pl.BlockSpec((1, w, kw), next_map),
            pl.BlockSpec((1, w, kw), prev_map),
            pl.BlockSpec((1, tq, kw), q_map),
            pl.BlockSpec((1, w, kw), next_map),
        ],
        out_specs=out_specs,
        out_shape=out_shape,
        compiler_params=_params("parallel", "parallel"),
        name="banded_attention",
    )(sink, band, q3, k3, k3, k3, v3, v3, v3)
    return [r.reshape(batch * seq, qw) for r in res]


def _flash_kernel(q_ref, k_ref, vt_ref, o_ref, st_a, st_b, *, tq, tk, tv, seq, n_kv_heads, g_per):
    ones = jnp.ones((HEAD_DIM, tk), BF16)
    outs = [None] * (n_kv_heads * g_per)
    for kh in range(n_kv_heads):
        cs = slice(kh * HEAD_DIM, (kh + 1) * HEAD_DIM)
        qs = jnp.concatenate(
            [q_ref[0, :, (kh * g_per + g) * HEAD_DIM:(kh * g_per + g + 1) * HEAD_DIM] for g in range(g_per)],
            axis=0)

        def scores(c, st_ref):
            start = pl.multiple_of(c * tk, tk)
            st_ref[...] = _dot_nt(k_ref[0, pl.ds(start, tk), cs], qs)

        def update(c, st_ref, m, acc):
            vt = jnp.concatenate([vt_ref[c * (tk // tv) + t, cs, :] for t in range(tk // tv)], axis=-1)
            vext = jnp.concatenate([vt, ones], axis=0)
            st = st_ref[...]
            m_new = jnp.maximum(m, jnp.max(st, axis=0, keepdims=True))
            alpha = jnp.exp2(m - m_new)
            p = jnp.exp2(st - m_new).astype(BF16)
            return m_new, alpha * acc + _dot(vext, p)

        bufs = (st_a, st_b)

        def group(cg, carry, last):
            m, acc = carry
            for t in range(FLASH_CHUNKS_PER_BODY):
                c = cg * FLASH_CHUNKS_PER_BODY + t
                if not (last and t == FLASH_CHUNKS_PER_BODY - 1):
                    scores(c + 1, bufs[(t + 1) % 2])
                m, acc = update(c, bufs[t % 2], m, acc)
            return m, acc

        n_groups = seq // (FLASH_CHUNKS_PER_BODY * tk)
        m0 = jnp.full((1, g_per * tq), NEG_INF, F32)
        acc0 = jnp.zeros((2 * HEAD_DIM, g_per * tq), F32)
        scores(0, st_a)
        carry = lax.fori_loop(0, n_groups - 1, lambda cg, cr: group(cg, cr, False), (m0, acc0))
        _, acc = group(n_groups - 1, carry, True)
        acc = acc.T
        o = acc[:, :HEAD_DIM] / acc[:, HEAD_DIM:]
        for g in range(g_per):
            outs[kh * g_per + g] = o[g * tq:(g + 1) * tq]
    o_ref[0] = jnp.concatenate(outs, axis=-1).astype(o_ref.dtype)


def _flash_attention(q, k, vt, *, batch, seq, n_q_heads, n_kv_heads, tq, tk):
    qw = n_q_heads * HEAD_DIM
    kw = n_kv_heads * HEAD_DIM
    tv = vt.shape[2]
    chunks = seq // tv
    kern = functools.partial(_flash_kernel, tq=tq, tk=tk, tv=tv, seq=seq, n_kv_heads=n_kv_heads,
                             g_per=n_q_heads // n_kv_heads)
    out = pl.pallas_call(
        kern,
        grid=(batch, seq // tq),
        in_specs=[
            pl.BlockSpec((1, tq, qw), lambda b, j: (b, j, 0)),
            pl.BlockSpec((1, seq, kw), lambda b, j: (b, 0, 0)),
            pl.BlockSpec((chunks, kw, tv), lambda b, j: (b, 0, 0)),
        ],
        out_specs=pl.BlockSpec((1, tq, qw), lambda b, j: (b, j, 0)),
        out_shape=jax.ShapeDtypeStruct((batch, seq, qw), BF16),
        scratch_shapes=[pltpu.VMEM((tk, (n_q_heads // n_kv_heads) * tq), F32) for _ in range(2)],
        compiler_params=_params("parallel", "arbitrary"),
        name="dense_attention",
    )(q.reshape(batch, seq, qw), k.reshape(batch, seq, kw), vt)
    return out.reshape(batch * seq, qw)


def _outproj_kernel(o_ref, w_ref, x_ref, g_ref, b_ref, y_ref):
    h = _dot(o_ref[...], w_ref[...])
    y_ref[...] = _layer_norm(DEEPNORM_ALPHA * x_ref[...] + h, g_ref[...], b_ref[...])


def _merge_outproj_kernel(*refs, dilations):
    n_g = len(dilations)
    o_refs, l_refs = refs[:n_g], refs[n_g:2 * n_g]
    w_ref, x_ref, g_ref, b_ref, y_ref = refs[2 * n_g:2 * n_g + 5]
    stage = refs[2 * n_g + 5:]
    tm = y_ref.shape[0]

    def token_order(ref, d, buf):
        if d == 1:
            return ref[...].astype(F32)
        for r in range(d):
            rows = ref[0, r].astype(F32)
            for c in range(buf.shape[0]):
                buf[c, pl.ds(r, tm // d, stride=d), :] = rows[:, c * LANES:(c + 1) * LANES]
        return jnp.concatenate([buf[c] for c in range(buf.shape[0])], axis=-1)

    os_, ls_, used = [], [], 0
    for p, d in enumerate(dilations):
        os_.append(token_order(o_refs[p], d, stage[used] if d > 1 else None))
        ls_.append(token_order(l_refs[p], d, stage[used + 1] if d > 1 else None))
        used += 2 if d > 1 else 0
    m = functools.reduce(jnp.maximum, ls_)
    es = [jnp.exp(l - m) for l in ls_]
    o = sum(e * o_p for e, o_p in zip(es, os_)) / sum(es)
    h = _dot(o.astype(BF16), w_ref[...])
    y_ref[...] = _layer_norm(DEEPNORM_ALPHA * x_ref[...] + h, g_ref[...], b_ref[...])


def _outproj_ln(os_, lses, w, x2, g, b, *, n, seq, dilations=None):
    d_model = x2.shape[1]
    tm = TOKEN_TILE
    tiles = seq // tm
    kdim = w.shape[0]
    row = lambda i: (i, 0)
    fixed = lambda i: (0, 0)
    acts = list(os_) + list(lses)
    scratch = []
    if lses:
        kern = functools.partial(_merge_outproj_kernel, dilations=tuple(dilations))
        act_specs = []
        for a, d in zip(acts, list(dilations) * 2):
            if d == 1:
                act_specs.append(pl.BlockSpec((tm, kdim), row))
            else:
                act_specs.append(pl.BlockSpec((1, d, tm // d, kdim), lambda i: (i // tiles, 0, i % tiles, 0)))
        scratch = [pltpu.VMEM((kdim // LANES, tm, LANES), F32) for d in dilations if d > 1 for _ in range(2)]
    else:
        kern = _outproj_kernel
        act_specs = [pl.BlockSpec((tm, kdim), row)]
    return pl.pallas_call(
        kern,
        grid=(n // tm,),
        in_specs=act_specs + [
            pl.BlockSpec((kdim, d_model), fixed),
            pl.BlockSpec((tm, d_model), row),
            pl.BlockSpec((1, d_model), fixed),
            pl.BlockSpec((1, d_model), fixed),
        ],
        out_specs=pl.BlockSpec((tm, d_model), row),
        out_shape=jax.ShapeDtypeStruct((n, d_model), F32),
        scratch_shapes=scratch,
        compiler_params=_params("parallel"),
        name="outproj_ln",
    )(*acts, w, x2, g, b)


def _matmul_kernel(x_ref, w_ref, o_ref):
    o_ref[...] = _dot(x_ref[...].astype(BF16), w_ref[...]).astype(o_ref.dtype)


def _matmul(x2, w, out_dtype, tm):
    n, kdim = x2.shape
    cols = w.shape[1]
    return pl.pallas_call(
        _matmul_kernel,
        grid=(n // tm,),
        in_specs=[pl.BlockSpec((tm, kdim), lambda i: (i, 0)), pl.BlockSpec((kdim, cols), lambda i: (0, 0))],
        out_specs=pl.BlockSpec((tm, cols), lambda i: (i, 0)),
        out_shape=jax.ShapeDtypeStruct((n, cols), out_dtype),
        compiler_params=_params("parallel"),
        name="matmul",
    )(x2, w)


def _memattn_kernel(x_ref, wq_ref, kv_ref, wo_ref, g_ref, b_ref, rw_ref, rb_ref,
                    y_ref, cls_ref, gate_ref):
    x = x_ref[...]
    q = (_dot(x.astype(BF16), wq_ref[...]) * (MEM_HEAD_DIM ** -0.5)).astype(BF16)
    outs = []
    for h in range(MEM_HEADS):
        cs = slice(h * MEM_HEAD_DIM, (h + 1) * MEM_HEAD_DIM)
        kh = kv_ref[0, :, cs]
        vh = kv_ref[0, :, D_MODEL + h * MEM_HEAD_DIM:D_MODEL + (h + 1) * MEM_HEAD_DIM]
        s = _dot_nt(q[:, cs], kh)
        m = jnp.max(s, axis=-1, keepdims=True)
        p = jnp.exp(s - m)
        denom = jnp.sum(p, axis=-1, keepdims=True)
        outs.append(_dot(p.astype(BF16), vh) / denom)
    o = jnp.concatenate(outs, axis=-1).astype(BF16)
    h_out = _dot(o, wo_ref[...])
    y = _layer_norm(DEEPNORM_ALPHA * x + h_out, g_ref[...], b_ref[...])
    y_ref[...] = y
    cls, gates = _route(y, rw_ref, rb_ref)
    cls_ref[0] = cls
    gate_ref[0] = gates


def _memory_attention(x2, kv, wq, wo, g, b, rw_cat, rb_t, *, seq):
    d = x2.shape[1]
    batch, n_mem = kv.shape[0], kv.shape[1]
    tm = TOKEN_TILE
    tiles = seq // tm
    n = batch * seq
    fixed = lambda bi, i: (0, 0)
    y, cls, gates = pl.pallas_call(
        _memattn_kernel,
        grid=(batch, tiles),
        in_specs=[
            pl.BlockSpec((tm, d), lambda bi, i: (bi * tiles + i, 0)),
            pl.BlockSpec((d, d), fixed),
            pl.BlockSpec((1, n_mem, 2 * d), lambda bi, i: (bi, 0, 0)),
            pl.BlockSpec((d, d), fixed),
            pl.BlockSpec((1, d), fixed),
            pl.BlockSpec((1, d), fixed),
            pl.BlockSpec((d, 2 * LANES), fixed),
            pl.BlockSpec((N_EXPERTS, tm), fixed),
        ],
        out_specs=[
            pl.BlockSpec((tm, d), lambda bi, i: (bi * tiles + i, 0)),
            pl.BlockSpec((1, 1, tm), lambda bi, i: (bi * tiles + i, 0, 0)),
            pl.BlockSpec((1, 2, tm), lambda bi, i: (bi * tiles + i, 0, 0)),
        ],
        out_shape=[
            jax.ShapeDtypeStruct((n, d), F32),
            jax.ShapeDtypeStruct((n // tm, 1, tm), jnp.int32),
            jax.ShapeDtypeStruct((n // tm, 2, tm), F32),
        ],
        compiler_params=_params("parallel", "parallel"),
        name="memory_attention",
    )(x2, wq, kv, wo, g, b, rw_cat, rb_t)
    return y, cls.reshape(n), gates[:, 0, :].reshape(n), gates[:, 1, :].reshape(n)


def _route(x, w_ref, bias_ref):
    xh = x.astype(BF16)
    xl = (x - xh.astype(F32)).astype(BF16)
    both = _dot(xh, w_ref[...])
    logits = both[:, :LANES] + both[:, LANES:] + _dot(xl, w_ref[:, :LANES])
    lt = logits.T[:N_EXPERTS]
    scores = 1.0 / (1.0 + jnp.exp(-lt))
    biased = scores + bias_ref[...]
    ng = N_EXPERT_GROUPS
    a = [biased[l * ng:(l + 1) * ng] for l in range(EXPERTS_PER_GROUP)]
    sc = [scores[l * ng:(l + 1) * ng] for l in range(EXPERTS_PER_GROUP)]
    gs = None
    for i in range(EXPERTS_PER_GROUP):
        for k in range(i + 1, EXPERTS_PER_GROUP):
            pair = a[i] + a[k]
            gs = pair if gs is None else jnp.maximum(gs, pair)
    gidx = lax.broadcasted_iota(jnp.int32, gs.shape, 0).astype(F32)
    gmax = jnp.max(gs, axis=0, keepdims=True)
    grp_f = jnp.min(jnp.where(gs == gmax, gidx, float(ng)), axis=0, keepdims=True)
    sel = gidx == grp_f
    grp = grp_f.astype(jnp.int32)
    v = [jnp.sum(jnp.where(sel, a[l], 0.0), axis=0, keepdims=True) for l in range(EXPERTS_PER_GROUP)]
    sv = [jnp.sum(jnp.where(sel, sc[l], 0.0), axis=0, keepdims=True) for l in range(EXPERTS_PER_GROUP)]

    def first_argmax(vals, excluded):
        best = None
        for l, val in enumerate(vals):
            cand = val if excluded is None else jnp.where(excluded == l, NEG_INF, val)
            best = cand if best is None else jnp.maximum(best, cand)
        idx = jnp.full(best.shape, EXPERTS_PER_GROUP - 1, jnp.int32)
        for l in range(EXPERTS_PER_GROUP - 1, -1, -1):
            hit = vals[l] == best
            if excluded is not None:
                hit = hit & (excluded != l)
            idx = jnp.where(hit, l, idx)
        return idx

    i1 = first_argmax(v, None)
    i2 = first_argmax(v, i1)

    def pick(vals, idx):
        out = vals[0]
        for l in range(1, EXPERTS_PER_GROUP):
            out = jnp.where(idx == l, vals[l], out)
        return out

    s1 = pick(sv, i1)
    s2 = pick(sv, i2)
    tot = s1 + s2
    lo = jnp.minimum(i1, i2)
    hi = jnp.maximum(i1, i2)
    first_is_lo = i1 < i2
    g_lo = jnp.where(first_is_lo, s1, s2) / tot
    g_hi = jnp.where(first_is_lo, s2, s1) / tot
    pair_idx = jnp.right_shift(lo * (7 - lo), 1) + (hi - lo - 1)
    return grp * N_PAIRS + pair_idx, jnp.concatenate([g_lo, g_hi], axis=0)


def _moe_kernel(tok_ref, lo_ref, hi_ref, nblk_ref,
                x_hbm, gate_ref, gu_lo_ref, gu_hi_ref, dn_lo_ref, dn_hi_ref, g_ref, b_ref,
                y_hbm, xbuf, ybuf, gsem, ssem, *, n_tokens):
    blk = pl.program_id(0)
    n_used = nblk_ref[0]
    rows = MOE_BLOCK

    def gather_copy(block, i, slot):
        tok = 0 if block is None else jnp.maximum(tok_ref[block * rows + i], 0)
        return pltpu.make_async_copy(x_hbm.at[pl.ds(tok, 1)], xbuf.at[slot, pl.ds(i, 1)], gsem.at[slot])

    def scatter_copy(block, i, slot):
        if block is None:
            dst = 0
        else:
            tok = tok_ref[block * rows + i]
            dst = jnp.where(tok < 0, n_tokens + slot * rows + i, tok)
        return pltpu.make_async_copy(ybuf.at[slot, pl.ds(i, 1)], y_hbm.at[pl.ds(dst, 1)], ssem.at[slot])

    def start_all(copy, block, slot):
        for i in range(rows):
            copy(block, i, slot).start()

    def wait_all(copy, slot):
        for i in range(rows):
            copy(None, i, slot).wait()

    @pl.when(blk < n_used)
    def _():
        slot = blk % 2
        nxt = jnp.minimum(blk + 1, n_used - 1)

        @pl.when(blk == 0)
        def _():
            start_all(gather_copy, 0, 0)
            ybuf[1] = jnp.zeros(ybuf.shape[1:], ybuf.dtype)
            for part in range(2):
                spare = pltpu.make_async_copy(ybuf.at[1], y_hbm.at[pl.ds(n_tokens + part * rows, rows)], ssem.at[1])
                spare.start()
                spare.wait()

        @pl.when(blk >= 2)
        def _():
            wait_all(scatter_copy, slot)

        wait_all(gather_copy, slot)
        start_all(gather_copy, nxt, 1 - slot)

        x = xbuf[slot]
        xb = x.astype(BF16)

        def expert(gu_ref, dn_ref):
            gu = _dot(xb, gu_ref[0])
            gate_act = gu[:, :D_EXPERT]
            act = gate_act / (1.0 + jnp.exp(-gate_act)) * gu[:, D_EXPERT:]
            return _dot(act.astype(BF16), dn_ref[0])

        r_i = lax.broadcasted_iota(jnp.int32, (rows, rows), 0)
        c_i = lax.broadcasted_iota(jnp.int32, (rows, rows), 1)
        diag = r_i == c_i
        gates = gate_ref[0]
        g_lo = jnp.sum(jnp.where(diag, gates[0:1, :], 0.0), axis=1, keepdims=True)
        g_hi = jnp.sum(jnp.where(diag, gates[1:2, :], 0.0), axis=1, keepdims=True)
        h = g_lo * expert(gu_lo_ref, dn_lo_ref) + g_hi * expert(gu_hi_ref, dn_hi_ref)
        ybuf[slot] = _layer_norm(DEEPNORM_ALPHA * x + h, g_ref[...], b_ref[...])
        start_all(scatter_copy, blk, slot)

        @pl.when(blk == n_used - 1)
        def _():
            wait_all(gather_copy, 1 - slot)

            @pl.when(blk >= 1)
            def _():
                wait_all(scatter_copy, 1 - slot)
            wait_all(scatter_copy, slot)


def _moe_experts(x2, slot_tok, blk_lo, blk_hi, n_used, slot_gates, w_gu, w_dn, g, b, *, n):
    d = x2.shape[1]
    n_blocks = blk_lo.shape[0]
    rows = MOE_BLOCK
    grid_spec = pltpu.PrefetchScalarGridSpec(
        num_scalar_prefetch=4,
        grid=(n_blocks,),
        in_specs=[
            pl.BlockSpec(memory_space=pl.ANY),
            pl.BlockSpec((1, 2, rows), lambda i, *_: (i, 0, 0)),
            pl.BlockSpec((1, d, 2 * D_EXPERT), lambda i, tok, lo, hi, nb: (lo[i], 0, 0)),
            pl.BlockSpec((1, d, 2 * D_EXPERT), lambda i, tok, lo, hi, nb: (hi[i], 0, 0)),
            pl.BlockSpec((1, D_EXPERT, d), lambda i, tok, lo, hi, nb: (lo[i], 0, 0)),
            pl.BlockSpec((1, D_EXPERT, d), lambda i, tok, lo, hi, nb: (hi[i], 0, 0)),
            pl.BlockSpec((1, d), lambda i, *_: (0, 0)),
            pl.BlockSpec((1, d), lambda i, *_: (0, 0)),
        ],
        out_specs=pl.BlockSpec(memory_space=pl.ANY),
        scratch_shapes=[
            pltpu.VMEM((2, rows, d), F32),
            pltpu.VMEM((2, rows, d), F32),
            pltpu.SemaphoreType.DMA((2,)),
            pltpu.SemaphoreType.DMA((2,)),
        ],
    )
    return pl.pallas_call(
        functools.partial(_moe_kernel, n_tokens=n),
        grid_spec=grid_spec,
        out_shape=jax.ShapeDtypeStruct((n + 2 * rows, d), F32),
        compiler_params=_params("arbitrary"),
        name="moe_experts",
    )(slot_tok, blk_lo, blk_hi, n_used, x2, slot_gates, w_gu, w_gu, w_dn, w_dn, g, b)


_PAIR_LO = np.array([0, 0, 0, 1, 1, 2], np.int32)
_PAIR_HI = np.array([1, 2, 3, 2, 3, 3], np.int32)


def _dispatch_tables(cls, g_lo, g_hi):
    n = cls.shape[0]
    rows = MOE_BLOCK
    n_blocks = n // rows + N_CLASSES
    i32 = jnp.int32
    cls_ids = jnp.arange(N_CLASSES, dtype=i32)
    _, order, glo_s, ghi_s = lax.sort((cls, jnp.arange(n, dtype=i32), g_lo, g_hi), num_keys=1)
    counts = jnp.sum((cls[:, None] == cls_ids[None, :]).astype(i32), axis=0)
    cls_blocks = (counts + rows - 1) // rows
    blk_end = jnp.cumsum(cls_blocks)
    blk_begin = blk_end - cls_blocks
    start = jnp.cumsum(counts) - counts
    n_used = blk_end[-1]
    blk = jnp.arange(n_blocks, dtype=i32)
    blk_eff = jnp.minimum(blk, n_used - 1)
    blk_cls = jnp.sum((blk_end[None, :] <= blk_eff[:, None]).astype(i32), axis=1)
    pick = blk_cls[:, None] == cls_ids[None, :]
    table = lambda t: jnp.sum(jnp.where(pick, t[None, :], 0), axis=1)
    blk_rank = (blk_eff - table(blk_begin)) * rows
    blk_nvalid = jnp.where(blk < n_used, jnp.clip(table(counts) - blk_rank, 0, rows), 0)
    row = jnp.arange(rows, dtype=i32)
    valid = row[None, :] < blk_nvalid[:, None]
    src = jnp.clip((table(start) + blk_rank)[:, None] + row[None, :], 0, n - 1)
    slot_tok = jnp.where(valid, order[src], -1).reshape(n_blocks * rows)
    slot_gates = jnp.stack([jnp.where(valid, glo_s[src], 0.0), jnp.where(valid, ghi_s[src], 0.0)], axis=1)
    grp = blk_cls // N_PAIRS
    pair = blk_cls % N_PAIRS
    pair_ids = jnp.arange(N_PAIRS, dtype=i32)
    pair_pick = pair[:, None] == pair_ids[None, :]
    blk_lo = grp * EXPERTS_PER_GROUP + jnp.sum(jnp.where(pair_pick, jnp.asarray(_PAIR_LO)[None, :], 0), axis=1)
    blk_hi = grp * EXPERTS_PER_GROUP + jnp.sum(jnp.where(pair_pick, jnp.asarray(_PAIR_HI)[None, :], 0), axis=1)
    return slot_tok, blk_lo.astype(i32), blk_hi.astype(i32), n_used.reshape(1).astype(i32), slot_gates


def _tile_heads(t64):
    return jnp.concatenate([t64] * (LANES // HEAD_DIM), axis=-1)


def _rope_tables(seq):
    t = jnp.arange(seq, dtype=jnp.int32)

    def tables(pos, dim):
        inv_freq = ROPE_THETA ** (-jnp.arange(0, dim, 2, dtype=F32) / dim)
        ang = pos.astype(F32)[:, None] * inv_freq[None, :]
        return jnp.cos(ang), jnp.sin(ang)

    c1, s1 = tables(t, HEAD_DIM)
    cos_1d = _tile_heads(jnp.concatenate([c1, c1], axis=-1))
    sin_1d = _tile_heads(jnp.concatenate([-s1, s1], axis=-1))
    cr, sr = tables(t // GRID_W, HEAD_DIM // 2)
    cc, sc = tables(t % GRID_W, HEAD_DIM // 2)
    cos_ax = _tile_heads(jnp.concatenate([cr, cr, cc, cc], axis=-1))
    sin_ax = _tile_heads(jnp.concatenate([-sr, sr, -sc, sc], axis=-1))
    return (cos_1d, sin_1d), (cos_ax, sin_ax)


def _block_diag_ones():
    i = np.arange(LANES)
    return jnp.asarray((i[:, None] // HEAD_DIM) == (i[None, :] // HEAD_DIM), BF16)


def kernel(x_prompt, x_sample, mem_prompt, mem_sample, a_w_in, a_sink, a_w_out, b_w_in, b_q_norm, b_k_norm,
           b_w_out, c_w_in, c_w_out, m_w_q, m_w_kv, m_w_out, ln_g, ln_b, router_w, router_bias, e_w_gu, e_w_down):
    seq = x_prompt.shape[1]
    assert x_sample.shape[1] == seq
    x = jnp.concatenate([x_prompt, x_sample], axis=0)
    mem = jnp.concatenate([mem_prompt, mem_sample], axis=0)
    batch = x.shape[0]
    n = batch * seq
    n_mem = mem.shape[1]
    x2 = x.reshape(n, D_MODEL)

    rope_1d, rope_ax = _rope_tables(seq)
    ones_bd = _block_diag_ones()
    unit_gains = jnp.ones((2, LANES), F32)
    scale = HEAD_DIM ** -0.5

    rw = router_w.astype(F32).reshape(D_MODEL, N_EXPERT_GROUPS, EXPERTS_PER_GROUP).transpose(0, 2, 1)
    rw = jnp.pad(rw.reshape(D_MODEL, N_EXPERTS), ((0, 0), (0, LANES - N_EXPERTS)))
    rw_hi = rw.astype(BF16)
    rw_cat = jnp.concatenate([rw_hi, (rw - rw_hi.astype(F32)).astype(BF16)], axis=1)
    rb = router_bias.astype(F32).reshape(N_EXPERT_GROUPS, EXPERTS_PER_GROUP).T.reshape(N_EXPERTS, 1)
    rb_t = jnp.broadcast_to(rb, (N_EXPERTS, TOKEN_TILE))

    cq, ck = C_HEADS * HEAD_DIM, C_KV_HEADS * HEAD_DIM

    for i in range(DEPTH):
        kind, j = i % N_MIXERS, i // N_MIXERS
        lg = lambda s_: ln_g[i, s_].astype(F32).reshape(1, D_MODEL)
        lb = lambda s_: ln_b[i, s_].astype(F32).reshape(1, D_MODEL)
        if kind == 0:
            nq, nk = A_HEADS * HEAD_DIM, A_KV_HEADS * HEAD_DIM
            q, k, v = _qkv_project(x2, a_w_in[j].astype(BF16), rope_1d[0], rope_1d[1], unit_gains, ones_bd, n=n,
                                   nq=nq, nk=nk, nv=nk, half=HEAD_DIM // 2, qk_norm=False,
                                   q_scale=scale * LOG2_E, seq=seq)
            o, = _banded_attention(q, k, v, a_sink[j].astype(F32), batch=batch, seq=seq, n_q_heads=A_HEADS,
                                   n_kv_heads=A_KV_HEADS, window=A_WINDOW, tq=BANDED_QUERY_BLOCK, with_lse=False)
            x2 = _outproj_ln([o], [], a_w_out[j].astype(BF16), x2, lg(0), lb(0), n=n, seq=seq)
        elif kind == 1:
            nq, nk = B_HEADS * HEAD_DIM, B_KV_HEADS * HEAD_DIM
            gains = jnp.stack([_tile_heads(b_q_norm[j].astype(F32)), _tile_heads(b_k_norm[j].astype(F32))], axis=0)
            q, k, v = _qkv_project(x2, b_w_in[j].astype(BF16), rope_ax[0], rope_ax[1], gains, ones_bd, n=n,
                                   nq=nq, nk=nk, nv=nk, half=HEAD_DIM // 4, qk_norm=True,
                                   q_scale=scale * LOG2_E, seq=seq, v_transposed=True)
            o = _flash_attention(q, k, v, batch=batch, seq=seq, n_q_heads=B_HEADS, n_kv_heads=B_KV_HEADS,
                                 tq=256, tk=1024)
            x2 = _outproj_ln([o], [], b_w_out[j].astype(BF16), x2, lg(0), lb(0), n=n, seq=seq)
        else:
            dilations = [dil for _, dil in C_PATTERNS]
            qkv = _qkv_project_dilated(x2, c_w_in[j].astype(BF16), rope_1d[0], rope_1d[1], n=n, seq=seq,
                                       dilations=dilations, cq=cq, ck=ck, q_scale=scale * LOG2_E)
            no_sink = jnp.full((C_HEADS,), NEG_INF, F32)
            os_, lses = [], []
            for (window, dil), (q, k, v) in zip(C_PATTERNS, qkv):
                half_w = (window // 2) // dil
                o, lse = _banded_attention(q.reshape(n, cq), k.reshape(n, ck), v.reshape(n, ck), no_sink,
                                           batch=batch * dil, seq=seq // dil, n_q_heads=C_HEADS,
                                           n_kv_heads=C_KV_HEADS, window=half_w, tq=BANDED_QUERY_BLOCK,
                                           with_lse=True)
                shape = (n, cq) if dil == 1 else (batch, dil, seq // dil, cq)
                os_.append(o.reshape(shape))
                lses.append(lse.reshape(shape))
            x2 = _outproj_ln(os_, lses, c_w_out[j].astype(BF16), x2, lg(0), lb(0), n=n, seq=seq,
                             dilations=dilations)

        kv = _matmul(mem.reshape(batch * n_mem, D_MODEL), m_w_kv[i].astype(BF16), BF16, tm=n_mem)
        x2, cls, g_lo, g_hi = _memory_attention(x2, kv.reshape(batch, n_mem, 2 * D_MODEL), m_w_q[i].astype(BF16),
                                                m_w_out[i].astype(BF16), lg(1), lb(1), rw_cat, rb_t, seq=seq)
        slot_tok, blk_lo, blk_hi, n_used, slot_gates = _dispatch_tables(cls, g_lo, g_hi)
        x2 = _moe_experts(x2, slot_tok, blk_lo, blk_hi, n_used, slot_gates,
                          e_w_gu[i].astype(BF16), e_w_down[i].astype(BF16), lg(2), lb(2), n=n)

    n_prompt = x_prompt.shape[0] * seq
    return (x2[:n_prompt].reshape(x_prompt.shape), x2[n_prompt:n].reshape(x_sample.shape))
```

```python
import functools

import jax
import jax.numpy as jnp
import numpy as np
from jax import lax
from jax.experimental import pallas as pl
from jax.experimental.pallas import tpu as pltpu

F32 = jnp.float32
BF16 = jnp.bfloat16

D_MODEL = 1024
DEPTH = 4
N_MIXERS = 3
HEAD_DIM = 64
ROPE_THETA = 10000.0
GRID_W = 64

A_HEADS, A_KV_HEADS, A_WINDOW = 16, 4, 128
B_HEADS, B_KV_HEADS = 16, 4
C_PATTERNS = ((128, 1), (512, 4), (2048, 16))
C_HEADS, C_KV_HEADS = 8, 2

MEM_HEADS = 4
MEM_HEAD_DIM = D_MODEL // MEM_HEADS

N_EXPERTS = 32
N_EXPERT_GROUPS = 8
EXPERTS_PER_GROUP = 4
D_EXPERT = D_MODEL // 4
N_PAIRS = 6
N_CLASSES = N_EXPERT_GROUPS * N_PAIRS
MOE_BLOCK = 128

LN_EPS = 1e-5
QK_NORM_EPS = 1e-6
DEEPNORM_ALPHA = (2 * DEPTH) ** 0.25

LANES = 128
TOKEN_TILE = 512
BANDED_QUERY_BLOCK = 256
BANDED_SUB_BLOCK = 128
FLASH_CHUNKS_PER_BODY = 4
VMEM_LIMIT_BYTES = 48 * 1024 * 1024

NEG_INF = float("-inf")
LOG2_E = 1.4426950408889634
LN_2 = 0.6931471805599453


def _params(*semantics):
    return pltpu.CompilerParams(dimension_semantics=semantics, vmem_limit_bytes=VMEM_LIMIT_BYTES)


def _layer_norm(z, g, b):
    mu = jnp.mean(z, axis=-1, keepdims=True)
    zc = z - mu
    var = jnp.mean(zc * zc, axis=-1, keepdims=True)
    return zc * lax.rsqrt(var + LN_EPS) * g + b


def _dot(a, b):
    return jnp.dot(a, b, preferred_element_type=F32)


def _dot_nt(a, b):
    return lax.dot_general(a, b, (((1,), (1,)), ((), ())), preferred_element_type=F32)


def _qkv_kernel(x_ref, w_ref, cos_ref, sin_ref, gain_ref, ones_ref, q_ref, k_ref, v_ref, *,
                nq, nk, half, qk_norm, q_scale, v_transposed):
    acc = _dot(x_ref[...].astype(BF16), w_ref[...])
    cos = cos_ref[...]
    sin = sin_ref[...]
    lane = lax.broadcasted_iota(jnp.int32, cos.shape, 1)
    first = (lane % (2 * half)) < half

    mean_sq = {}
    if qk_norm:
        width = ones_ref.shape[0]
        for c2 in range((nq + nk) // width):
            a2 = jnp.square(acc[:, c2 * width:(c2 + 1) * width])
            hi = a2.astype(BF16)
            lo = (a2 - hi.astype(F32)).astype(BF16)
            ms = (_dot(hi, ones_ref[...]) + _dot(lo, ones_ref[...])) * (1.0 / HEAD_DIM)
            for t in range(width // LANES):
                mean_sq[c2 * (width // LANES) + t] = ms[:, t * LANES:(t + 1) * LANES]

    def rotate(c, gain, scale):
        a = acc[:, c * LANES:(c + 1) * LANES]
        if qk_norm:
            a = a * lax.rsqrt(mean_sq[c] + QK_NORM_EPS) * gain
        partner = jnp.where(first, pltpu.roll(a, LANES - half, 1), pltpu.roll(a, half, 1))
        r = a * cos + partner * sin
        if scale != 1.0:
            r = r * scale
        return r.astype(BF16)

    gq = gain_ref[0:1, :]
    gk = gain_ref[1:2, :]
    for c in range(nq // LANES):
        q_ref[:, c * LANES:(c + 1) * LANES] = rotate(c, gq, q_scale)
    for c in range(nk // LANES):
        k_ref[:, c * LANES:(c + 1) * LANES] = rotate(nq // LANES + c, gk, 1.0)
    if v_transposed:
        v_ref[0] = acc[:, nq + nk:].T.astype(BF16)
    else:
        v_ref[...] = acc[:, nq + nk:].astype(BF16)


def _qkv_project(x2, w, cos, sin, gains, ones_bd, *, n, nq, nk, nv, half, qk_norm, q_scale, seq,
                 v_transposed=False):
    d = x2.shape[1]
    tm = TOKEN_TILE
    tiles_per_seq = seq // tm
    kern = functools.partial(_qkv_kernel, nq=nq, nk=nk, half=half, qk_norm=qk_norm, q_scale=q_scale,
                             v_transposed=v_transposed)
    if v_transposed:
        v_spec = pl.BlockSpec((1, nv, tm), lambda i: (i, 0, 0))
        v_shape = jax.ShapeDtypeStruct((n // tm, nv, tm), BF16)
    else:
        v_spec = pl.BlockSpec((tm, nv), lambda i: (i, 0))
        v_shape = jax.ShapeDtypeStruct((n, nv), BF16)
    return pl.pallas_call(
        kern,
        grid=(n // tm,),
        in_specs=[
            pl.BlockSpec((tm, d), lambda i: (i, 0)),
            pl.BlockSpec((d, nq + nk + nv), lambda i: (0, 0)),
            pl.BlockSpec((tm, LANES), lambda i: (i % tiles_per_seq, 0)),
            pl.BlockSpec((tm, LANES), lambda i: (i % tiles_per_seq, 0)),
            pl.BlockSpec((2, LANES), lambda i: (0, 0)),
            pl.BlockSpec(ones_bd.shape, lambda i: (0, 0)),
        ],
        out_specs=[
            pl.BlockSpec((tm, nq), lambda i: (i, 0)),
            pl.BlockSpec((tm, nk), lambda i: (i, 0)),
            v_spec,
        ],
        out_shape=[
            jax.ShapeDtypeStruct((n, nq), BF16),
            jax.ShapeDtypeStruct((n, nk), BF16),
            v_shape,
        ],
        compiler_params=_params("parallel"),
        name="qkv_project",
    )(x2, w, cos, sin, gains, ones_bd)


def _qkv_dilated_kernel(x_ref, w_ref, cos_ref, sin_ref, *refs, dilations, cq, ck, q_scale):
    n_g = len(dilations)
    out_refs, acc_ref = refs[:3 * n_g], refs[3 * n_g]
    acc = _dot(x_ref[...].astype(BF16), w_ref[...])
    for c in range(acc_ref.shape[0]):
        acc_ref[c] = acc[:, c * LANES:(c + 1) * LANES]
    tm = acc_ref.shape[1]
    half = HEAD_DIM // 2
    gw = cq + 2 * ck

    for p, d in enumerate(dilations):
        q_ref, k_ref, v_ref = out_refs[3 * p:3 * p + 3]
        lane = lax.broadcasted_iota(jnp.int32, (tm // d, LANES), 1)
        first = (lane % (2 * half)) < half
        for r in range(d):
            rows = slice(None) if d == 1 else pl.ds(r, tm // d, stride=d)
            cos = cos_ref[rows, :]
            sin = sin_ref[rows, :]

            def put(ref, c, val):
                if d == 1:
                    ref[:, c * LANES:(c + 1) * LANES] = val.astype(BF16)
                else:
                    ref[0, r, :, c * LANES:(c + 1) * LANES] = val.astype(BF16)

            def rotate(a, scale):
                partner = jnp.where(first, pltpu.roll(a, LANES - half, 1), pltpu.roll(a, half, 1))
                return (a * cos + partner * sin) * scale

            for c in range(cq // LANES):
                put(q_ref, c, rotate(acc_ref[(p * gw) // LANES + c, rows, :], q_scale))
            for c in range(ck // LANES):
                put(k_ref, c, rotate(acc_ref[(p * gw + cq) // LANES + c, rows, :], 1.0))
                put(v_ref, c, acc_ref[(p * gw + cq + ck) // LANES + c, rows, :])


def _qkv_project_dilated(x2, w, cos, sin, *, n, seq, dilations, cq, ck, q_scale):
    d_model = x2.shape[1]
    tm = TOKEN_TILE
    tiles = seq // tm
    out_specs, out_shape = [], []
    for d in dilations:
        for cols in (cq, ck, ck):
            if d == 1:
                out_specs.append(pl.BlockSpec((tm, cols), lambda i: (i, 0)))
                out_shape.append(jax.ShapeDtypeStruct((n, cols), BF16))
            else:
                out_specs.append(pl.BlockSpec((1, d, tm // d, cols), lambda i: (i // tiles, 0, i % tiles, 0)))
                out_shape.append(jax.ShapeDtypeStruct((n // seq, d, seq // d, cols), BF16))
    kern = functools.partial(_qkv_dilated_kernel, dilations=tuple(dilations), cq=cq, ck=ck, q_scale=q_scale)
    res = pl.pallas_call(
        kern,
        grid=(n // tm,),
        in_specs=[
            pl.BlockSpec((tm, d_model), lambda i: (i, 0)),
            pl.BlockSpec(w.shape, lambda i: (0, 0)),
            pl.BlockSpec((tm, LANES), lambda i: (i % tiles, 0)),
            pl.BlockSpec((tm, LANES), lambda i: (i % tiles, 0)),
        ],
        out_specs=out_specs,
        out_shape=out_shape,
        scratch_shapes=[pltpu.VMEM((w.shape[1] // LANES, tm, LANES), F32)],
        compiler_params=_params("parallel"),
        name="qkv_project_dilated",
    )(x2, w, cos, sin)
    return [tuple(res[3 * p:3 * p + 3]) for p in range(len(dilations))]


def _banded_kernel(sink_ref, band_ref, q_ref, kp_ref, kc_ref, kn_ref, vp_ref, vc_ref, vn_ref, *out_refs,
                   tq, sub, w, n_q_heads, n_kv_heads, with_lse):
    o_ref = out_refs[0]
    j = pl.program_id(1)
    g_per = n_q_heads // n_kv_heads
    n_sub = tq // sub
    span = tq + 2 * w
    sub_span = sub + 2 * w
    m_lanes = g_per * sub
    key = lax.broadcasted_iota(jnp.int32, (span, 1), 0)
    off_seq = ((key < w) & (j == 0)) | ((key >= w + tq) & (j == pl.num_programs(1) - 1))
    edge = jnp.where(off_seq, NEG_INF, 0.0)
    biases = [band_ref[...] + edge[sb * sub:sb * sub + sub_span] for sb in range(n_sub)]
    lane_row = lax.broadcasted_iota(jnp.int32, (1, m_lanes), 1)
    half_rows = lax.broadcasted_iota(jnp.int32, (2 * HEAD_DIM, m_lanes), 0) >= HEAD_DIM
    ones = jnp.ones((HEAD_DIM, sub_span), BF16)

    vcat = jnp.concatenate([vp_ref[0], vc_ref[0], vn_ref[0]], axis=0).astype(F32)
    vt_all = jnp.concatenate([vcat[:, c * LANES:(c + 1) * LANES].T for c in range(vcat.shape[1] // LANES)],
                             axis=0).astype(BF16)
    kcats = [jnp.concatenate([kp_ref[0, :, kh * HEAD_DIM:(kh + 1) * HEAD_DIM],
                              kc_ref[0, :, kh * HEAD_DIM:(kh + 1) * HEAD_DIM],
                              kn_ref[0, :, kh * HEAD_DIM:(kh + 1) * HEAD_DIM]], axis=0)
             for kh in range(n_kv_heads)]

    def scores(kh, sb):
        r0 = sb * sub
        qs = jnp.concatenate(
            [q_ref[0, r0:r0 + sub, (kh * g_per + g) * HEAD_DIM:(kh * g_per + g + 1) * HEAD_DIM]
             for g in range(g_per)], axis=0)
        return _dot_nt(kcats[kh][r0:r0 + sub_span], qs) + biases[sb]

    tasks = [(kh, sb) for kh in range(n_kv_heads) for sb in range(n_sub)]
    o_parts = [[None] * n_sub for _ in range(n_q_heads)]
    l_parts = [[None] * n_sub for _ in range(n_q_heads)]
    st_next = scores(*tasks[0])
    for t, (kh, sb) in enumerate(tasks):
        r0 = sb * sub
        st = st_next
        if t + 1 < len(tasks):
            st_next = scores(*tasks[t + 1])
        sink = jnp.full((1, m_lanes), sink_ref[kh * g_per], F32)
        for g in range(1, g_per):
            sink = jnp.where(lane_row >= g * sub, sink_ref[kh * g_per + g], sink)
        sink = sink * LOG2_E
        m = jnp.maximum(jnp.max(st, axis=0, keepdims=True), sink)
        p = jnp.exp2(st - m).astype(BF16)
        vext = jnp.concatenate([vt_all[kh * HEAD_DIM:(kh + 1) * HEAD_DIM, r0:r0 + sub_span], ones], axis=0)
        acc = _dot(vext, p)
        acc = jnp.where(half_rows, acc + jnp.exp2(sink - m), acc)
        num, den = acc[:HEAD_DIM], acc[HEAD_DIM:]
        low = jnp.log(den) + m * LN_2 if with_lse else den
        res = jnp.concatenate([num / den, low], axis=0).T
        for g in range(g_per):
            o_parts[kh * g_per + g][sb] = res[g * sub:(g + 1) * sub, :HEAD_DIM]
            l_parts[kh * g_per + g][sb] = res[g * sub:(g + 1) * sub, HEAD_DIM:]
    o_ref[0] = jnp.concatenate([jnp.concatenate(parts, axis=0) for parts in o_parts], axis=-1).astype(o_ref.dtype)
    if with_lse:
        out_refs[1][0] = jnp.concatenate([jnp.concatenate(parts, axis=0) for parts in l_parts], axis=-1)


def _banded_attention(q, k, v, sink, *, batch, seq, n_q_heads, n_kv_heads, window, tq, with_lse):
    qw = n_q_heads * HEAD_DIM
    kw = n_kv_heads * HEAD_DIM
    w = window
    r_blocks = tq // w
    nb_w = seq // w
    q3 = q.reshape(batch, seq, qw)
    k3 = k.reshape(batch, seq, kw)
    v3 = v.reshape(batch, seq, kw)

    def q_map(b, j):
        return (b, j, 0)

    def prev_map(b, j):
        return (b, jnp.maximum(j * r_blocks - 1, 0), 0)

    def next_map(b, j):
        return (b, jnp.minimum((j + 1) * r_blocks, nb_w - 1), 0)

    g_per = n_q_heads // n_kv_heads
    sub = BANDED_SUB_BLOCK
    rel = np.arange(sub + 2 * w)[:, None] - w - (np.arange(g_per * sub) % sub)[None, :]
    band = jnp.asarray(np.where(np.abs(rel) <= w, 0.0, NEG_INF), F32)

    kern = functools.partial(_banded_kernel, tq=tq, sub=sub, w=w, n_q_heads=n_q_heads,
                             n_kv_heads=n_kv_heads, with_lse=with_lse)
    out_specs = [pl.BlockSpec((1, tq, qw), q_map)]
    out_shape = [jax.ShapeDtypeStruct((batch, seq, qw), BF16)]
    if with_lse:
        out_specs.append(pl.BlockSpec((1, tq, qw), q_map))
        out_shape.append(jax.ShapeDtypeStruct((batch, seq, qw), F32))
    res = pl.pallas_call(
        kern,
        grid=(batch, seq // tq),
        in_specs=[
            pl.BlockSpec(memory_space=pltpu.SMEM),
            pl.BlockSpec(band.shape, lambda b, j: (0, 0)),
            pl.BlockSpec((1, tq, qw), q_map),
            pl.BlockSpec((1, w, kw), prev_map),
            pl.BlockSpec((1, tq, kw), q_map),
            pl.BlockSpec((1, w, kw), next_map),
            pl.BlockSpec((1, w, kw), prev_map),
            pl.BlockSpec((1, tq, kw), q_map),
            pl.BlockSpec((1, w, kw), next_map),
        ],
        out_specs=out_specs,
        out_shape=out_shape,
        compiler_params=_params("parallel", "parallel"),
        name="banded_attention",
    )(sink, band, q3, k3, k3, k3, v3, v3, v3)
    return [r.reshape(batch * seq, qw) for r in res]


def _flash_kernel(q_ref, k_ref, vt_ref, o_ref, st_a, st_b, *, tq, tk, tv, seq, n_kv_heads, g_per):
    ones = jnp.ones((HEAD_DIM, tk), BF16)
    outs = [None] * (n_kv_heads * g_per)
    for kh in range(n_kv_heads):
        cs = slice(kh * HEAD_DIM, (kh + 1) * HEAD_DIM)
        qs = jnp.concatenate(
            [q_ref[0, :, (kh * g_per + g) * HEAD_DIM:(kh * g_per + g + 1) * HEAD_DIM] for g in range(g_per)],
            axis=0)

        def scores(c, st_ref):
            start = pl.multiple_of(c * tk, tk)
            st_ref[...] = _dot_nt(k_ref[0, pl.ds(start, tk), cs], qs)

        def update(c, st_ref, m, acc):
            vt = jnp.concatenate([vt_ref[c * (tk // tv) + t, cs, :] for t in range(tk // tv)], axis=-1)
            vext = jnp.concatenate([vt, ones], axis=0)
            st = st_ref[...]
            m_new = jnp.maximum(m, jnp.max(st, axis=0, keepdims=True))
            alpha = jnp.exp2(m - m_new)
            p = jnp.exp2(st - m_new).astype(BF16)
            return m_new, alpha * acc + _dot(vext, p)

        bufs = (st_a, st_b)

        def group(cg, carry, last):
            m, acc = carry
            for t in range(FLASH_CHUNKS_PER_BODY):
                c = cg * FLASH_CHUNKS_PER_BODY + t
                if not (last and t == FLASH_CHUNKS_PER_BODY - 1):
                    scores(c + 1, bufs[(t + 1) % 2])
                m, acc = update(c, bufs[t % 2], m, acc)
            return m, acc

        n_groups = seq // (FLASH_CHUNKS_PER_BODY * tk)
        m0 = jnp.full((1, g_per * tq), NEG_INF, F32)
        acc0 = jnp.zeros((2 * HEAD_DIM, g_per * tq), F32)
        scores(0, st_a)
        carry = lax.fori_loop(0, n_groups - 1, lambda cg, cr: group(cg, cr, False), (m0, acc0))
        _, acc = group(n_groups - 1, carry, True)
        acc = acc.T
        o = acc[:, :HEAD_DIM] / acc[:, HEAD_DIM:]
        for g in range(g_per):
            outs[kh * g_per + g] = o[g * tq:(g + 1) * tq]
    o_ref[0] = jnp.concatenate(outs, axis=-1).astype(o_ref.dtype)


def _flash_attention(q, k, vt, *, batch, seq, n_q_heads, n_kv_heads, tq, tk):
    qw = n_q_heads * HEAD_DIM
    kw = n_kv_heads * HEAD_DIM
    tv = vt.shape[2]
    chunks = seq // tv
    kern = functools.partial(_flash_kernel, tq=tq, tk=tk, tv=tv, seq=seq, n_kv_heads=n_kv_heads,
                             g_per=n_q_heads // n_kv_heads)
    out = pl.pallas_call(
        kern,
        grid=(batch, seq // tq),
        in_specs=[
            pl.BlockSpec((1, tq, qw), lambda b, j: (b, j, 0)),
            pl.BlockSpec((1, seq, kw), lambda b, j: (b, 0, 0)),
            pl.BlockSpec((chunks, kw, tv), lambda b, j: (b, 0, 0)),
        ],
        out_specs=pl.BlockSpec((1, tq, qw), lambda b, j: (b, j, 0)),
        out_shape=jax.ShapeDtypeStruct((batch, seq, qw), BF16),
        scratch_shapes=[pltpu.VMEM((tk, (n_q_heads // n_kv_heads) * tq), F32) for _ in range(2)],
        compiler_params=_params("parallel", "arbitrary"),
        name="dense_attention",
    )(q.reshape(batch, seq, qw), k.reshape(batch, seq, kw), vt)
    return out.reshape(batch * seq, qw)


def _merge_outproj_kernel(*refs, dilations):
    n_g = len(dilations)
    o_refs, l_refs = refs[:n_g], refs[n_g:2 * n_g]
    w_ref, x_ref, g_ref, b_ref, y_ref = refs[2 * n_g:2 * n_g + 5]
    stage = refs[2 * n_g + 5:]
    tm = y_ref.shape[0]

    def token_order(ref, d, buf):
        if d == 1:
            return ref[...].astype(F32)
        for r in range(d):
            rows = ref[0, r].astype(F32)
            for c in range(buf.shape[0]):
                buf[c, pl.ds(r, tm // d, stride=d), :] = rows[:, c * LANES:(c + 1) * LANES]
        return jnp.concatenate([buf[c] for c in range(buf.shape[0])], axis=-1)

    os_, ls_, used = [], [], 0
    for p, d in enumerate(dilations):
        os_.append(token_order(o_refs[p], d, stage[used] if d > 1 else None))
        ls_.append(token_order(l_refs[p], d, stage[used + 1] if d > 1 else None))
        used += 2 if d > 1 else 0
    m = functools.reduce(jnp.maximum, ls_)
    es = [jnp.exp(l - m) for l in ls_]
    o = sum(e * o_p for e, o_p in zip(es, os_)) / sum(es)
    h = _dot(o.astype(BF16), w_ref[...])
    y_ref[...] = _layer_norm(DEEPNORM_ALPHA * x_ref[...] + h, g_ref[...], b_ref[...])


def _merge_outproj_ln(os_, lses, w, x2, g, b, *, n, seq, dilations):
    d_model = x2.shape[1]
    tm = TOKEN_TILE
    tiles = seq // tm
    kdim = w.shape[0]
    row = lambda i: (i, 0)
    fixed = lambda i: (0, 0)
    acts = list(os_) + list(lses)
    kern = functools.partial(_merge_outproj_kernel, dilations=tuple(dilations))
    act_specs = []
    for d in list(dilations) * 2:
        if d == 1:
            act_specs.append(pl.BlockSpec((tm, kdim), row))
        else:
            act_specs.append(pl.BlockSpec((1, d, tm // d, kdim), lambda i: (i // tiles, 0, i % tiles, 0)))
    scratch = [pltpu.VMEM((kdim // LANES, tm, LANES), F32) for d in dilations if d > 1 for _ in range(2)]
    return pl.pallas_call(
        kern,
        grid=(n // tm,),
        in_specs=act_specs + [
            pl.BlockSpec((kdim, d_model), fixed),
            pl.BlockSpec((tm, d_model), row),
            pl.BlockSpec((1, d_model), fixed),
            pl.BlockSpec((1, d_model), fixed),
        ],
        out_specs=pl.BlockSpec((tm, d_model), row),
        out_shape=jax.ShapeDtypeStruct((n, d_model), F32),
        scratch_shapes=scratch,
        compiler_params=_params("parallel"),
        name="outproj_ln",
    )(*acts, w, x2, g, b)


def _matmul_kernel(x_ref, w_ref, o_ref):
    o_ref[...] = _dot(x_ref[...].astype(BF16), w_ref[...]).astype(o_ref.dtype)


def _matmul(x2, w, out_dtype, tm):
    n, kdim = x2.shape
    cols = w.shape[1]
    return pl.pallas_call(
        _matmul_kernel,
        grid=(n // tm,),
        in_specs=[pl.BlockSpec((tm, kdim), lambda i: (i, 0)), pl.BlockSpec((kdim, cols), lambda i: (0, 0))],
        out_specs=pl.BlockSpec((tm, cols), lambda i: (i, 0)),
        out_shape=jax.ShapeDtypeStruct((n, cols), out_dtype),
        compiler_params=_params("parallel"),
        name="matmul",
    )(x2, w)


def _memattn_kernel(*refs, with_outproj):
    if with_outproj:
        o_ref, wmix_ref, gmix_ref, bmix_ref = refs[:4]
        refs = refs[4:]
    x_ref, wq_ref, kv_ref, wo_ref, g_ref, b_ref, rw_ref, rb_ref, y_ref, cls_ref, gate_ref = refs
    x = x_ref[...]
    if with_outproj:
        x = _layer_norm(DEEPNORM_ALPHA * x + _dot(o_ref[...], wmix_ref[...]), gmix_ref[...], bmix_ref[...])
    q = (_dot(x.astype(BF16), wq_ref[...]) * (MEM_HEAD_DIM ** -0.5)).astype(BF16)
    outs = []
    for h in range(MEM_HEADS):
        cs = slice(h * MEM_HEAD_DIM, (h + 1) * MEM_HEAD_DIM)
        kh = kv_ref[0, :, cs]
        vh = kv_ref[0, :, D_MODEL + h * MEM_HEAD_DIM:D_MODEL + (h + 1) * MEM_HEAD_DIM]
        s = _dot_nt(q[:, cs], kh)
        m = jnp.max(s, axis=-1, keepdims=True)
        p = jnp.exp(s - m)
        denom = jnp.sum(p, axis=-1, keepdims=True)
        outs.append(_dot(p.astype(BF16), vh) / denom)
    o = jnp.concatenate(outs, axis=-1).astype(BF16)
    h_out = _dot(o, wo_ref[...])
    y = _layer_norm(DEEPNORM_ALPHA * x + h_out, g_ref[...], b_ref[...])
    y_ref[...] = y
    cls, gates = _route(y, rw_ref, rb_ref)
    cls_ref[0] = cls
    gate_ref[0] = gates


def _memory_attention(x2, kv, wq, wo, g, b, rw_cat, rb_t, *, seq, mixer=None):
    d = x2.shape[1]
    batch, n_mem = kv.shape[0], kv.shape[1]
    tm = TOKEN_TILE
    tiles = seq // tm
    n = batch * seq
    fixed = lambda bi, i: (0, 0)
    row = lambda bi, i: (bi * tiles + i, 0)
    mixer_args, mixer_specs = [], []
    if mixer is not None:
        o, w_mix, g_mix, b_mix = mixer
        mixer_args = [o, w_mix, g_mix, b_mix]
        mixer_specs = [pl.BlockSpec((tm, o.shape[1]), row), pl.BlockSpec(w_mix.shape, fixed),
                       pl.BlockSpec((1, d), fixed), pl.BlockSpec((1, d), fixed)]
    y, cls, gates = pl.pallas_call(
        functools.partial(_memattn_kernel, with_outproj=mixer is not None),
        grid=(batch, tiles),
        in_specs=mixer_specs + [
            pl.BlockSpec((tm, d), row),
            pl.BlockSpec((d, d), fixed),
            pl.BlockSpec((1, n_mem, 2 * d), lambda bi, i: (bi, 0, 0)),
            pl.BlockSpec((d, d), fixed),
            pl.BlockSpec((1, d), fixed),
            pl.BlockSpec((1, d), fixed),
            pl.BlockSpec((d, 2 * LANES), fixed),
            pl.BlockSpec((N_EXPERTS, tm), fixed),
        ],
        out_specs=[
            pl.BlockSpec((tm, d), lambda bi, i: (bi * tiles + i, 0)),
            pl.BlockSpec((1, 1, tm), lambda bi, i: (bi * tiles + i, 0, 0)),
            pl.BlockSpec((1, 2, tm), lambda bi, i: (bi * tiles + i, 0, 0)),
        ],
        out_shape=[
            jax.ShapeDtypeStruct((n, d), F32),
            jax.ShapeDtypeStruct((n // tm, 1, tm), jnp.int32),
            jax.ShapeDtypeStruct((n // tm, 2, tm), F32),
        ],
        compiler_params=_params("parallel", "parallel"),
        name="memory_attention",
    )(*mixer_args, x2, wq, kv, wo, g, b, rw_cat, rb_t)
    return y, cls.reshape(n), gates[:, 0, :].reshape(n), gates[:, 1, :].reshape(n)


def _route(x, w_ref, bias_ref):
    xh = x.astype(BF16)
    xl = (x - xh.astype(F32)).astype(BF16)
    both = _dot(xh, w_ref[...])
    logits = both[:, :LANES] + both[:, LANES:] + _dot(xl, w_ref[:, :LANES])
    lt = logits.T[:N_EXPERTS]
    scores = 1.0 / (1.0 + jnp.exp(-lt))
    biased = scores + bias_ref[...]
    ng = N_EXPERT_GROUPS
    a = [biased[l * ng:(l + 1) * ng] for l in range(EXPERTS_PER_GROUP)]
    sc = [scores[l * ng:(l + 1) * ng] for l in range(EXPERTS_PER_GROUP)]
    gs = None
    for i in range(EXPERTS_PER_GROUP):
        for k in range(i + 1, EXPERTS_PER_GROUP):
            pair = a[i] + a[k]
            gs = pair if gs is None else jnp.maximum(gs, pair)
    gidx = lax.broadcasted_iota(jnp.int32, gs.shape, 0).astype(F32)
    gmax = jnp.max(gs, axis=0, keepdims=True)
    grp_f = jnp.min(jnp.where(gs == gmax, gidx, float(ng)), axis=0, keepdims=True)
    sel = gidx == grp_f
    grp = grp_f.astype(jnp.int32)
    v = [jnp.sum(jnp.where(sel, a[l], 0.0), axis=0, keepdims=True) for l in range(EXPERTS_PER_GROUP)]
    sv = [jnp.sum(jnp.where(sel, sc[l], 0.0), axis=0, keepdims=True) for l in range(EXPERTS_PER_GROUP)]

    def first_argmax(vals, excluded):
        best = None
        for l, val in enumerate(vals):
            cand = val if excluded is None else jnp.where(excluded == l, NEG_INF, val)
            best = cand if best is None else jnp.maximum(best, cand)
        idx = jnp.full(best.shape, EXPERTS_PER_GROUP - 1, jnp.int32)
        for l in range(EXPERTS_PER_GROUP - 1, -1, -1):
            hit = vals[l] == best
            if excluded is not None:
                hit = hit & (excluded != l)
            idx = jnp.where(hit, l, idx)
        return idx

    i1 = first_argmax(v, None)
    i2 = first_argmax(v, i1)

    def pick(vals, idx):
        out = vals[0]
        for l in range(1, EXPERTS_PER_GROUP):
            out = jnp.where(idx == l, vals[l], out)
        return out

    s1 = pick(sv, i1)
    s2 = pick(sv, i2)
    tot = s1 + s2
    lo = jnp.minimum(i1, i2)
    hi = jnp.maximum(i1, i2)
    first_is_lo = i1 < i2
    g_lo = jnp.where(first_is_lo, s1, s2) / tot
    g_hi = jnp.where(first_is_lo, s2, s1) / tot
    pair_idx = jnp.right_shift(lo * (7 - lo), 1) + (hi - lo - 1)
    return grp * N_PAIRS + pair_idx, jnp.concatenate([g_lo, g_hi], axis=0)


def _moe_kernel(tok_ref, lo_ref, hi_ref, nblk_ref,
                x_hbm, gate_ref, gu_lo_ref, gu_hi_ref, dn_lo_ref, dn_hi_ref, g_ref, b_ref,
                y_hbm, xbuf, ybuf, gsem, ssem, *, n_tokens):
    blk = pl.program_id(0)
    n_used = nblk_ref[0]
    rows = MOE_BLOCK

    def gather_copy(block, i, slot):
        tok = 0 if block is None else jnp.maximum(tok_ref[block * rows + i], 0)
        return pltpu.make_async_copy(x_hbm.at[pl.ds(tok, 1)], xbuf.at[slot, pl.ds(i, 1)], gsem.at[slot])

    def scatter_copy(block, i, slot):
        if block is None:
            dst = 0
        else:
            tok = tok_ref[block * rows + i]
            dst = jnp.where(tok < 0, n_tokens + slot * rows + i, tok)
        return pltpu.make_async_copy(ybuf.at[slot, pl.ds(i, 1)], y_hbm.at[pl.ds(dst, 1)], ssem.at[slot])

    def start_all(copy, block, slot):
        for i in range(rows):
            copy(block, i, slot).start()

    def wait_all(copy, slot):
        for i in range(rows):
            copy(None, i, slot).wait()

    @pl.when(blk < n_used)
    def _():
        slot = blk % 2
        nxt = jnp.minimum(blk + 1, n_used - 1)

        @pl.when(blk == 0)
        def _():
            start_all(gather_copy, 0, 0)
            ybuf[1] = jnp.zeros(ybuf.shape[1:], ybuf.dtype)
            for part in range(2):
                spare = pltpu.make_async_copy(ybuf.at[1], y_hbm.at[pl.ds(n_tokens + part * rows, rows)], ssem.at[1])
                spare.start()
                spare.wait()

        @pl.when(blk >= 2)
        def _():
            wait_all(scatter_copy, slot)

        wait_all(gather_copy, slot)
        start_all(gather_copy, nxt, 1 - slot)

        x = xbuf[slot]
        xb = x.astype(BF16)

        def expert(gu_ref, dn_ref):
            gu = _dot(xb, gu_ref[0])
            gate_act = gu[:, :D_EXPERT]
            act = gate_act / (1.0 + jnp.exp(-gate_act)) * gu[:, D_EXPERT:]
            return _dot(act.astype(BF16), dn_ref[0])

        r_i = lax.broadcasted_iota(jnp.int32, (rows, rows), 0)
        c_i = lax.broadcasted_iota(jnp.int32, (rows, rows), 1)
        diag = r_i == c_i
        gates = gate_ref[0]
        g_lo = jnp.sum(jnp.where(diag, gates[0:1, :], 0.0), axis=1, keepdims=True)
        g_hi = jnp.sum(jnp.where(diag, gates[1:2, :], 0.0), axis=1, keepdims=True)
        h = g_lo * expert(gu_lo_ref, dn_lo_ref) + g_hi * expert(gu_hi_ref, dn_hi_ref)
        ybuf[slot] = _layer_norm(DEEPNORM_ALPHA * x + h, g_ref[...], b_ref[...])
        start_all(scatter_copy, blk, slot)

        @pl.when(blk == n_used - 1)
        def _():
            wait_all(gather_copy, 1 - slot)

            @pl.when(blk >= 1)
            def _():
                wait_all(scatter_copy, 1 - slot)
            wait_all(scatter_copy, slot)


def _moe_experts(x2, slot_tok, blk_lo, blk_hi, n_used, slot_gates, w_gu, w_dn, g, b, *, n):
    d = x2.shape[1]
    n_blocks = blk_lo.shape[0]
    rows = MOE_BLOCK
    grid_spec = pltpu.PrefetchScalarGridSpec(
        num_scalar_prefetch=4,
        grid=(n_blocks,),
        in_specs=[
            pl.BlockSpec(memory_space=pl.ANY),
            pl.BlockSpec((1, 2, rows), lambda i, *_: (i, 0, 0)),
            pl.BlockSpec((1, d, 2 * D_EXPERT), lambda i, tok, lo, hi, nb: (lo[i], 0, 0)),
            pl.BlockSpec((1, d, 2 * D_EXPERT), lambda i, tok, lo, hi, nb: (hi[i], 0, 0)),
            pl.BlockSpec((1, D_EXPERT, d), lambda i, tok, lo, hi, nb: (lo[i], 0, 0)),
            pl.BlockSpec((1, D_EXPERT, d), lambda i, tok, lo, hi, nb: (hi[i], 0, 0)),
            pl.BlockSpec((1, d), lambda i, *_: (0, 0)),
            pl.BlockSpec((1, d), lambda i, *_: (0, 0)),
        ],
        out_specs=pl.BlockSpec(memory_space=pl.ANY),
        scratch_shapes=[
            pltpu.VMEM((2, rows, d), F32),
            pltpu.VMEM((2, rows, d), F32),
            pltpu.SemaphoreType.DMA((2,)),
            pltpu.SemaphoreType.DMA((2,)),
        ],
    )
    return pl.pallas_call(
        functools.partial(_moe_kernel, n_tokens=n),
        grid_spec=grid_spec,
        out_shape=jax.ShapeDtypeStruct((n + 2 * rows, d), F32),
        compiler_params=_params("arbitrary"),
        name="moe_experts",
    )(slot_tok, blk_lo, blk_hi, n_used, x2, slot_gates, w_gu, w_gu, w_dn, w_dn, g, b)


_PAIR_LO = np.array([0, 0, 0, 1, 1, 2], np.int32)
_PAIR_HI = np.array([1, 2, 3, 2, 3, 3], np.int32)


def _dispatch_tables(cls, g_lo, g_hi):
    n = cls.shape[0]
    rows = MOE_BLOCK
    n_blocks = n // rows + N_CLASSES
    i32 = jnp.int32
    cls_ids = jnp.arange(N_CLASSES, dtype=i32)
    _, order, glo_s, ghi_s = lax.sort((cls, jnp.arange(n, dtype=i32), g_lo, g_hi), num_keys=1)
    counts = jnp.sum((cls[:, None] == cls_ids[None, :]).astype(i32), axis=0)
    cls_blocks = (counts + rows - 1) // rows
    blk_end = jnp.cumsum(cls_blocks)
    blk_begin = blk_end - cls_blocks
    start = jnp.cumsum(counts) - counts
    n_used = blk_end[-1]
    blk = jnp.arange(n_blocks, dtype=i32)
    blk_eff = jnp.minimum(blk, n_used - 1)
    blk_cls = jnp.sum((blk_end[None, :] <= blk_eff[:, None]).astype(i32), axis=1)
    pick = blk_cls[:, None] == cls_ids[None, :]
    table = lambda t: jnp.sum(jnp.where(pick, t[None, :], 0), axis=1)
    blk_rank = (blk_eff - table(blk_begin)) * rows
    blk_nvalid = jnp.where(blk < n_used, jnp.clip(table(counts) - blk_rank, 0, rows), 0)
    row = jnp.arange(rows, dtype=i32)
    valid = row[None, :] < blk_nvalid[:, None]
    src = jnp.clip((table(start) + blk_rank)[:, None] + row[None, :], 0, n - 1)
    slot_tok = jnp.where(valid, order[src], -1).reshape(n_blocks * rows)
    slot_gates = jnp.stack([jnp.where(valid, glo_s[src], 0.0), jnp.where(valid, ghi_s[src], 0.0)], axis=1)
    grp = blk_cls // N_PAIRS
    pair = blk_cls % N_PAIRS
    pair_ids = jnp.arange(N_PAIRS, dtype=i32)
    pair_pick = pair[:, None] == pair_ids[None, :]
    blk_lo = grp * EXPERTS_PER_GROUP + jnp.sum(jnp.where(pair_pick, jnp.asarray(_PAIR_LO)[None, :], 0), axis=1)
    blk_hi = grp * EXPERTS_PER_GROUP + jnp.sum(jnp.where(pair_pick, jnp.asarray(_PAIR_HI)[None, :], 0), axis=1)
    return slot_tok, blk_lo.astype(i32), blk_hi.astype(i32), n_used.reshape(1).astype(i32), slot_gates


def _tile_heads(t64):
    return jnp.concatenate([t64] * (LANES // HEAD_DIM), axis=-1)


def _rope_tables(seq):
    t = jnp.arange(seq, dtype=jnp.int32)

    def tables(pos, dim):
        inv_freq = ROPE_THETA ** (-jnp.arange(0, dim, 2, dtype=F32) / dim)
        ang = pos.astype(F32)[:, None] * inv_freq[None, :]
        return jnp.cos(ang), jnp.sin(ang)

    c1, s1 = tables(t, HEAD_DIM)
    cos_1d = _tile_heads(jnp.concatenate([c1, c1], axis=-1))
    sin_1d = _tile_heads(jnp.concatenate([-s1, s1], axis=-1))
    cr, sr = tables(t // GRID_W, HEAD_DIM // 2)
    cc, sc = tables(t % GRID_W, HEAD_DIM // 2)
    cos_ax = _tile_heads(jnp.concatenate([cr, cr, cc, cc], axis=-1))
    sin_ax = _tile_heads(jnp.concatenate([-sr, sr, -sc, sc], axis=-1))
    return (cos_1d, sin_1d), (cos_ax, sin_ax)


def _block_diag_ones():
    i = np.arange(2 * LANES)
    return jnp.asarray((i[:, None] // HEAD_DIM) == (i[None, :] // HEAD_DIM), BF16)


def kernel(x_prompt, x_sample, mem_prompt, mem_sample, a_w_in, a_sink, a_w_out, b_w_in, b_q_norm, b_k_norm,
           b_w_out, c_w_in, c_w_out, m_w_q, m_w_kv, m_w_out, ln_g, ln_b, router_w, router_bias, e_w_gu, e_w_down):
    seq = x_prompt.shape[1]
    assert x_sample.shape[1] == seq
    x = jnp.concatenate([x_prompt, x_sample], axis=0)
    mem = jnp.concatenate([mem_prompt, mem_sample], axis=0)
    batch = x.shape[0]
    n = batch * seq
    n_mem = mem.shape[1]
    x2 = x.reshape(n, D_MODEL)

    rope_1d, rope_ax = _rope_tables(seq)
    ones_bd = _block_diag_ones()
    unit_gains = jnp.ones((2, LANES), F32)
    scale = HEAD_DIM ** -0.5

    rw = router_w.astype(F32).reshape(D_MODEL, N_EXPERT_GROUPS, EXPERTS_PER_GROUP).transpose(0, 2, 1)
    rw = jnp.pad(rw.reshape(D_MODEL, N_EXPERTS), ((0, 0), (0, LANES - N_EXPERTS)))
    rw_hi = rw.astype(BF16)
    rw_cat = jnp.concatenate([rw_hi, (rw - rw_hi.astype(F32)).astype(BF16)], axis=1)
    rb = router_bias.astype(F32).reshape(N_EXPERT_GROUPS, EXPERTS_PER_GROUP).T.reshape(N_EXPERTS, 1)
    rb_t = jnp.broadcast_to(rb, (N_EXPERTS, TOKEN_TILE))

    cq, ck = C_HEADS * HEAD_DIM, C_KV_HEADS * HEAD_DIM

    for i in range(DEPTH):
        kind, j = i % N_MIXERS, i // N_MIXERS
        lg = lambda s_: ln_g[i, s_].astype(F32).reshape(1, D_MODEL)
        lb = lambda s_: ln_b[i, s_].astype(F32).reshape(1, D_MODEL)
        if kind == 0:
            nq, nk = A_HEADS * HEAD_DIM, A_KV_HEADS * HEAD_DIM
            q, k, v = _qkv_project(x2, a_w_in[j].astype(BF16), rope_1d[0], rope_1d[1], unit_gains, ones_bd, n=n,
                                   nq=nq, nk=nk, nv=nk, half=HEAD_DIM // 2, qk_norm=False,
                                   q_scale=scale * LOG2_E, seq=seq)
            o, = _banded_attention(q, k, v, a_sink[j].astype(F32), batch=batch, seq=seq, n_q_heads=A_HEADS,
                                   n_kv_heads=A_KV_HEADS, window=A_WINDOW, tq=BANDED_QUERY_BLOCK, with_lse=False)
            mixer = (o, a_w_out[j].astype(BF16), lg(0), lb(0))
        elif kind == 1:
            nq, nk = B_HEADS * HEAD_DIM, B_KV_HEADS * HEAD_DIM
            gains = jnp.stack([_tile_heads(b_q_norm[j].astype(F32)), _tile_heads(b_k_norm[j].astype(F32))], axis=0)
            q, k, v = _qkv_project(x2, b_w_in[j].astype(BF16), rope_ax[0], rope_ax[1], gains, ones_bd, n=n,
                                   nq=nq, nk=nk, nv=nk, half=HEAD_DIM // 4, qk_norm=True,
                                   q_scale=scale * LOG2_E, seq=seq, v_transposed=True)
            o = _flash_attention(q, k, v, batch=batch, seq=seq, n_q_heads=B_HEADS, n_kv_heads=B_KV_HEADS,
                                 tq=256, tk=1024)
            mixer = (o, b_w_out[j].astype(BF16), lg(0), lb(0))
        else:
            dilations = [dil for _, dil in C_PATTERNS]
            qkv = _qkv_project_dilated(x2, c_w_in[j].astype(BF16), rope_1d[0], rope_1d[1], n=n, seq=seq,
                                       dilations=dilations, cq=cq, ck=ck, q_scale=scale * LOG2_E)
            no_sink = jnp.full((C_HEADS,), NEG_INF, F32)
            os_, lses = [], []
            for (window, dil), (q, k, v) in zip(C_PATTERNS, qkv):
                half_w = (window // 2) // dil
                o, lse = _banded_attention(q.reshape(n, cq), k.reshape(n, ck), v.reshape(n, ck), no_sink,
                                           batch=batch * dil, seq=seq // dil, n_q_heads=C_HEADS,
                                           n_kv_heads=C_KV_HEADS, window=half_w, tq=BANDED_QUERY_BLOCK,
                                           with_lse=True)
                shape = (n, cq) if dil == 1 else (batch, dil, seq // dil, cq)
                os_.append(o.reshape(shape))
                lses.append(lse.reshape(shape))
            x2 = _merge_outproj_ln(os_, lses, c_w_out[j].astype(BF16), x2, lg(0), lb(0), n=n, seq=seq,
                                   dilations=dilations)
            mixer = None

        kv = _matmul(mem.reshape(batch * n_mem, D_MODEL), m_w_kv[i].astype(BF16), BF16, tm=n_mem)
        x2, cls, g_lo, g_hi = _memory_attention(x2, kv.reshape(batch, n_mem, 2 * D_MODEL), m_w_q[i].astype(BF16),
                                                m_w_out[i].astype(BF16), lg(1), lb(1), rw_cat, rb_t, seq=seq,
                                                mixer=mixer)
        slot_tok, blk_lo, blk_hi, n_used, slot_gates = _dispatch_tables(cls, g_lo, g_hi)
        x2 = _moe_experts(x2, slot_tok, blk_lo, blk_hi, n_used, slot_gates,
                          e_w_gu[i].astype(BF16), e_w_down[i].astype(BF16), lg(2), lb(2), n=n)

    n_prompt = x_prompt.shape[0] * seq
    return (x2[:n_prompt].reshape(x_prompt.shape), x2[n_prompt:n].reshape(x_sample.shape))
```

```python
import functools

import jax
import jax.numpy as jnp
import numpy as np
from jax import lax
from jax.experimental import pallas as pl
from jax.experimental.pallas import tpu as pltpu

F32 = jnp.float32
BF16 = jnp.bfloat16

D_MODEL = 1024
DEPTH = 4
N_MIXERS = 3
HEAD_DIM = 64
ROPE_THETA = 10000.0
GRID_W = 64

A_HEADS, A_KV_HEADS, A_WINDOW = 16, 4, 128
B_HEADS, B_KV_HEADS = 16, 4
C_PATTERNS = ((128, 1), (512, 4), (2048, 16))
C_HEADS, C_KV_HEADS = 8, 2

MEM_HEADS = 4
MEM_HEAD_DIM = D_MODEL // MEM_HEADS

N_EXPERTS = 32
N_EXPERT_GROUPS = 8
EXPERTS_PER_GROUP = 4
D_EXPERT = D_MODEL // 4
N_PAIRS = 6
N_CLASSES = N_EXPERT_GROUPS * N_PAIRS
MOE_BLOCK = 128

LN_EPS = 1e-5
QK_NORM_EPS = 1e-6
DEEPNORM_ALPHA = (2 * DEPTH) ** 0.25

LANES = 128
TOKEN_TILE = 512
BANDED_QUERY_BLOCK = 256
BANDED_SUB_BLOCK = 128
FLASH_CHUNKS_PER_BODY = 4
VMEM_LIMIT_BYTES = 48 * 1024 * 1024

NEG_INF = float("-inf")
LOG2_E = 1.4426950408889634
LN_2 = 0.6931471805599453


def _params(*semantics):
    return pltpu.CompilerParams(dimension_semantics=semantics, vmem_limit_bytes=VMEM_LIMIT_BYTES)


def _layer_norm(z, g, b):
    mu = jnp.mean(z, axis=-1, keepdims=True)
    zc = z - mu
    var = jnp.mean(zc * zc, axis=-1, keepdims=True)
    return zc * lax.rsqrt(var + LN_EPS) * g + b


def _dot(a, b):
    return jnp.dot(a, b, preferred_element_type=F32)


def _dot_nt(a, b):
    return lax.dot_general(a, b, (((1,), (1,)), ((), ())), preferred_element_type=F32)


def _qkv_kernel(x_ref, w_ref, cos_ref, sin_ref, gain_ref, ones_ref, q_ref, k_ref, v_ref, *,
                nq, nk, half, qk_norm, q_scale, v_transposed):
    acc = _dot(x_ref[...].astype(BF16), w_ref[...])
    cos = cos_ref[...]
    sin = sin_ref[...]
    lane = lax.broadcasted_iota(jnp.int32, cos.shape, 1)
    first = (lane % (2 * half)) < half

    mean_sq = {}
    if qk_norm:
        width = ones_ref.shape[0]
        for c2 in range((nq + nk) // width):
            a2 = jnp.square(acc[:, c2 * width:(c2 + 1) * width])
            hi = a2.astype(BF16)
            lo = (a2 - hi.astype(F32)).astype(BF16)
            ms = (_dot(hi, ones_ref[...]) + _dot(lo, ones_ref[...])) * (1.0 / HEAD_DIM)
            for t in range(width // LANES):
                mean_sq[c2 * (width // LANES) + t] = ms[:, t * LANES:(t + 1) * LANES]

    def rotate(c, gain, scale):
        a = acc[:, c * LANES:(c + 1) * LANES]
        if qk_norm:
            a = a * lax.rsqrt(mean_sq[c] + QK_NORM_EPS) * gain
        partner = jnp.where(first, pltpu.roll(a, LANES - half, 1), pltpu.roll(a, half, 1))
        r = a * cos + partner * sin
        if scale != 1.0:
            r = r * scale
        return r.astype(BF16)

    gq = gain_ref[0:1, :]
    gk = gain_ref[1:2, :]
    for c in range(nq // LANES):
        q_ref[:, c * LANES:(c + 1) * LANES] = rotate(c, gq, q_scale)
    for c in range(nk // LANES):
        k_ref[:, c * LANES:(c + 1) * LANES] = rotate(nq // LANES + c, gk, 1.0)
    if v_transposed:
        v_ref[0] = acc[:, nq + nk:].T.astype(BF16)
    else:
        v_ref[...] = acc[:, nq + nk:].astype(BF16)


def _qkv_project(x2, w, cos, sin, gains, ones_bd, *, n, nq, nk, nv, half, qk_norm, q_scale, seq,
                 v_transposed=False):
    d = x2.shape[1]
    tm = TOKEN_TILE
    tiles_per_seq = seq // tm
    kern = functools.partial(_qkv_kernel, nq=nq, nk=nk, half=half, qk_norm=qk_norm, q_scale=q_scale,
                             v_transposed=v_transposed)
    if v_transposed:
        v_spec = pl.BlockSpec((1, nv, tm), lambda i: (i, 0, 0))
        v_shape = jax.ShapeDtypeStruct((n // tm, nv, tm), BF16)
    else:
        v_spec = pl.BlockSpec((tm, nv), lambda i: (i, 0))
        v_shape = jax.ShapeDtypeStruct((n, nv), BF16)
    return pl.pallas_call(
        kern,
        grid=(n // tm,),
        in_specs=[
            pl.BlockSpec((tm, d), lambda i: (i, 0)),
            pl.BlockSpec((d, nq + nk + nv), lambda i: (0, 0)),
            pl.BlockSpec((tm, LANES), lambda i: (i % tiles_per_seq, 0)),
            pl.BlockSpec((tm, LANES), lambda i: (i % tiles_per_seq, 0)),
            pl.BlockSpec((2, LANES), lambda i: (0, 0)),
            pl.BlockSpec(ones_bd.shape, lambda i: (0, 0)),
        ],
        out_specs=[
            pl.BlockSpec((tm, nq), lambda i: (i, 0)),
            pl.BlockSpec((tm, nk), lambda i: (i, 0)),
            v_spec,
        ],
        out_shape=[
            jax.ShapeDtypeStruct((n, nq), BF16),
            jax.ShapeDtypeStruct((n, nk), BF16),
            v_shape,
        ],
        compiler_params=_params("parallel"),
        name="qkv_project",
    )(x2, w, cos, sin, gains, ones_bd)


def _qkv_dilated_kernel(x_ref, w_ref, cos_ref, sin_ref, *refs, dilations, cq, ck, q_scale):
    n_g = len(dilations)
    out_refs, acc_ref = refs[:3 * n_g], refs[3 * n_g]
    acc = _dot(x_ref[...].astype(BF16), w_ref[...])
    for c in range(acc_ref.shape[0]):
        acc_ref[c] = acc[:, c * LANES:(c + 1) * LANES]
    tm = acc_ref.shape[1]
    half = HEAD_DIM // 2
    gw = cq + 2 * ck

    for p, d in enumerate(dilations):
        q_ref, k_ref, v_ref = out_refs[3 * p:3 * p + 3]
        lane = lax.broadcasted_iota(jnp.int32, (tm // d, LANES), 1)
        first = (lane % (2 * half)) < half
        for r in range(d):
            rows = slice(None) if d == 1 else pl.ds(r, tm // d, stride=d)
            cos = cos_ref[rows, :]
            sin = sin_ref[rows, :]

            def put(ref, c, val):
                if d == 1:
                    ref[:, c * LANES:(c + 1) * LANES] = val.astype(BF16)
                else:
                    ref[0, r, :, c * LANES:(c + 1) * LANES] = val.astype(BF16)

            def rotate(a, scale):
                partner = jnp.where(first, pltpu.roll(a, LANES - half, 1), pltpu.roll(a, half, 1))
                return (a * cos + partner * sin) * scale

            for c in range(cq // LANES):
                put(q_ref, c, rotate(acc_ref[(p * gw) // LANES + c, rows, :], q_scale))
            for c in range(ck // LANES):
                put(k_ref, c, rotate(acc_ref[(p * gw + cq) // LANES + c, rows, :], 1.0))
                put(v_ref, c, acc_ref[(p * gw + cq + ck) // LANES + c, rows, :])


def _qkv_project_dilated(x2, w, cos, sin, *, n, seq, dilations, cq, ck, q_scale):
    d_model = x2.shape[1]
    tm = TOKEN_TILE
    tiles = seq // tm
    out_specs, out_shape = [], []
    for d in dilations:
        for cols in (cq, ck, ck):
            if d == 1:
                out_specs.append(pl.BlockSpec((tm, cols), lambda i: (i, 0)))
                out_shape.append(jax.ShapeDtypeStruct((n, cols), BF16))
            else:
                out_specs.append(pl.BlockSpec((1, d, tm // d, cols), lambda i: (i // tiles, 0, i % tiles, 0)))
                out_shape.append(jax.ShapeDtypeStruct((n // seq, d, seq // d, cols), BF16))
    kern = functools.partial(_qkv_dilated_kernel, dilations=tuple(dilations), cq=cq, ck=ck, q_scale=q_scale)
    res = pl.pallas_call(
        kern,
        grid=(n // tm,),
        in_specs=[
            pl.BlockSpec((tm, d_model), lambda i: (i, 0)),
            pl.BlockSpec(w.shape, lambda i: (0, 0)),
            pl.BlockSpec((tm, LANES), lambda i: (i % tiles, 0)),
            pl.BlockSpec((tm, LANES), lambda i: (i % tiles, 0)),
        ],
        out_specs=out_specs,
        out_shape=out_shape,
        scratch_shapes=[pltpu.VMEM((w.shape[1] // LANES, tm, LANES), F32)],
        compiler_params=_params("parallel"),
        name="qkv_project_dilated",
    )(x2, w, cos, sin)
    return [tuple(res[3 * p:3 * p + 3]) for p in range(len(dilations))]


def _banded_kernel(sink_ref, band_ref, q_ref, kp_ref, kc_ref, kn_ref, vp_ref, vc_ref, vn_ref, *out_refs,
                   tq, sub, w, n_q_heads, n_kv_heads, with_lse):
    o_ref = out_refs[0]
    j = pl.program_id(1)
    g_per = n_q_heads // n_kv_heads
    n_sub = tq // sub
    span = tq + 2 * w
    sub_span = sub + 2 * w
    m_lanes = g_per * sub
    key = lax.broadcasted_iota(jnp.int32, (span, 1), 0)
    off_seq = ((key < w) & (j == 0)) | ((key >= w + tq) & (j == pl.num_programs(1) - 1))
    edge = jnp.where(off_seq, NEG_INF, 0.0)
    biases = [band_ref[...] + edge[sb * sub:sb * sub + sub_span] for sb in range(n_sub)]
    lane_row = lax.broadcasted_iota(jnp.int32, (1, m_lanes), 1)
    half_rows = lax.broadcasted_iota(jnp.int32, (2 * HEAD_DIM, m_lanes), 0) >= HEAD_DIM
    ones = jnp.ones((HEAD_DIM, sub_span), BF16)

    vcat = jnp.concatenate([vp_ref[0], vc_ref[0], vn_ref[0]], axis=0).astype(F32)
    vt_all = jnp.concatenate([vcat[:, c * LANES:(c + 1) * LANES].T for c in range(vcat.shape[1] // LANES)],
                             axis=0).astype(BF16)
    kcats = [jnp.concatenate([kp_ref[0, :, kh * HEAD_DIM:(kh + 1) * HEAD_DIM],
                              kc_ref[0, :, kh * HEAD_DIM:(kh + 1) * HEAD_DIM],
                              kn_ref[0, :, kh * HEAD_DIM:(kh + 1) * HEAD_DIM]], axis=0)
             for kh in range(n_kv_heads)]

    def scores(kh, sb):
        r0 = sb * sub
        qs = jnp.concatenate(
            [q_ref[0, r0:r0 + sub, (kh * g_per + g) * HEAD_DIM:(kh * g_per + g + 1) * HEAD_DIM]
             for g in range(g_per)], axis=0)
        return _dot_nt(kcats[kh][r0:r0 + sub_span], qs) + biases[sb]

    tasks = [(kh, sb) for kh in range(n_kv_heads) for sb in range(n_sub)]
    o_parts = [[None] * n_sub for _ in range(n_q_heads)]
    l_parts = [[None] * n_sub for _ in range(n_q_heads)]
    st_next = scores(*tasks[0])
    for t, (kh, sb) in enumerate(tasks):
        r0 = sb * sub
        st = st_next
        if t + 1 < len(tasks):
            st_next = scores(*tasks[t + 1])
        sink = jnp.full((1, m_lanes), sink_ref[kh * g_per], F32)
        for g in range(1, g_per):
            sink = jnp.where(lane_row >= g * sub, sink_ref[kh * g_per + g], sink)
        sink = sink * LOG2_E
        m = jnp.maximum(jnp.max(st, axis=0, keepdims=True), sink)
        p = jnp.exp2(st - m).astype(BF16)
        vext = jnp.concatenate([vt_all[kh * HEAD_DIM:(kh + 1) * HEAD_DIM, r0:r0 + sub_span], ones], axis=0)
        acc = _dot(vext, p)
        acc = jnp.where(half_rows, acc + jnp.exp2(sink - m), acc)
        num, den = acc[:HEAD_DIM], acc[HEAD_DIM:]
        low = jnp.log(den) + m * LN_2 if with_lse else den
        res = jnp.concatenate([num / den, low], axis=0).T
        for g in range(g_per):
            o_parts[kh * g_per + g][sb] = res[g * sub:(g + 1) * sub, :HEAD_DIM]
            l_parts[kh * g_per + g][sb] = res[g * sub:(g + 1) * sub, HEAD_DIM:]
    o_ref[0] = jnp.concatenate([jnp.concatenate(parts, axis=0) for parts in o_parts], axis=-1).astype(o_ref.dtype)
    if with_lse:
        out_refs[1][0] = jnp.concatenate([jnp.concatenate(parts, axis=0) for parts in l_parts], axis=-1)


def _banded_attention(q, k, v, sink, *, batch, seq, n_q_heads, n_kv_heads, window, tq, with_lse):
    qw = n_q_heads * HEAD_DIM
    kw = n_kv_heads * HEAD_DIM
    w = window
    r_blocks = tq // w
    nb_w = seq // w
    q3 = q.reshape(batch, seq, qw)
    k3 = k.reshape(batch, seq, kw)
    v3 = v.reshape(batch, seq, kw)

    def q_map(b, j):
        return (b, j, 0)

    def prev_map(b, j):
        return (b, jnp.maximum(j * r_blocks - 1, 0), 0)

    def next_map(b, j):
        return (b, jnp.minimum((j + 1) * r_blocks, nb_w - 1), 0)

    g_per = n_q_heads // n_kv_heads
    sub = BANDED_SUB_BLOCK
    rel = np.arange(sub + 2 * w)[:, None] - w - (np.arange(g_per * sub) % sub)[None, :]
    band = jnp.asarray(np.where(np.abs(rel) <= w, 0.0, NEG_INF), F32)

    kern = functools.partial(_banded_kernel, tq=tq, sub=sub, w=w, n_q_heads=n_q_heads,
                             n_kv_heads=n_kv_heads, with_lse=with_lse)
    out_specs = [pl.BlockSpec((1, tq, qw), q_map)]
    out_shape = [jax.ShapeDtypeStruct((batch, seq, qw), BF16)]
    if with_lse:
        out_specs.append(pl.BlockSpec((1, tq, qw), q_map))
        out_shape.append(jax.ShapeDtypeStruct((batch, seq, qw), F32))
    res = pl.pallas_call(
        kern,
        grid=(batch, seq // tq),
        in_specs=[
            pl.BlockSpec(memory_space=pltpu.SMEM),
            pl.BlockSpec(band.shape, lambda b, j: (0, 0)),
            pl.BlockSpec((1, tq, qw), q_map),
            pl.BlockSpec((1, w, kw), prev_map),
            pl.BlockSpec((1, tq, kw), q_map),
            pl.BlockSpec((1, w, kw), next_map),
            pl.BlockSpec((1, w, kw), prev_map),
            pl.BlockSpec((1, tq, kw), q_map),
            pl.BlockSpec((1, w, kw), next_map),
        ],
        out_specs=out_specs,
        out_shape=out_shape,
        compiler_params=_params("parallel", "parallel"),
        name="banded_attention",
    )(sink, band, q3, k3, k3, k3, v3, v3, v3)
    return [r.reshape(batch * seq, qw) for r in res]


def _flash_kernel(q_ref, k_ref, vt_ref, o_ref, st_a, st_b, *, tq, tk, tv, seq, n_kv_heads, g_per):
    ones = jnp.ones((HEAD_DIM, tk), BF16)
    outs = [None] * (n_kv_heads * g_per)
    for kh in range(n_kv_heads):
        cs = slice(kh * HEAD_DIM, (kh + 1) * HEAD_DIM)
        qs = jnp.concatenate(
            [q_ref[0, :, (kh * g_per + g) * HEAD_DIM:(kh * g_per + g + 1) * HEAD_DIM] for g in range(g_per)],
            axis=0)

        def scores(c, st_ref):
            start = pl.multiple_of(c * tk, tk)
            st_ref[...] = _dot_nt(k_ref[0, pl.ds(start, tk), cs], qs)

        def update(c, st_ref, m, acc):
            vt = jnp.concatenate([vt_ref[c * (tk // tv) + t, cs, :] for t in range(tk // tv)], axis=-1)
            vext = jnp.concatenate([vt, ones], axis=0)
            st = st_ref[...]
            m_new = jnp.maximum(m, jnp.max(st, axis=0, keepdims=True))
            alpha = jnp.exp2(m - m_new)
            p = jnp.exp2(st - m_new).astype(BF16)
            return m_new, alpha * acc + _dot(vext, p)

        bufs = (st_a, st_b)

        def group(cg, carry, last):
            m, acc = carry
            for t in range(FLASH_CHUNKS_PER_BODY):
                c = cg * FLASH_CHUNKS_PER_BODY + t
                if not (last and t == FLASH_CHUNKS_PER_BODY - 1):
                    scores(c + 1, bufs[(t + 1) % 2])
                m, acc = update(c, bufs[t % 2], m, acc)
            return m, acc

        n_groups = seq // (FLASH_CHUNKS_PER_BODY * tk)
        m0 = jnp.full((1, g_per * tq), NEG_INF, F32)
        acc0 = jnp.zeros((2 * HEAD_DIM, g_per * tq), F32)
        scores(0, st_a)
        carry = lax.fori_loop(0, n_groups - 1, lambda cg, cr: group(cg, cr, False), (m0, acc0))
        _, acc = group(n_groups - 1, carry, True)
        acc = acc.T
        o = acc[:, :HEAD_DIM] / acc[:, HEAD_DIM:]
        for g in range(g_per):
            outs[kh * g_per + g] = o[g * tq:(g + 1) * tq]
    o_ref[0] = jnp.concatenate(outs, axis=-1).astype(o_ref.dtype)


def _flash_attention(q, k, vt, *, batch, seq, n_q_heads, n_kv_heads, tq, tk):
    qw = n_q_heads * HEAD_DIM
    kw = n_kv_heads * HEAD_DIM
    tv = vt.shape[2]
    chunks = seq // tv
    kern = functools.partial(_flash_kernel, tq=tq, tk=tk, tv=tv, seq=seq, n_kv_heads=n_kv_heads,
                             g_per=n_q_heads // n_kv_heads)
    out = pl.pallas_call(
        kern,
        grid=(batch, seq // tq),
        in_specs=[
            pl.BlockSpec((1, tq, qw), lambda b, j: (b, j, 0)),
            pl.BlockSpec((1, seq, kw), lambda b, j: (b, 0, 0)),
            pl.BlockSpec((chunks, kw, tv), lambda b, j: (b, 0, 0)),
        ],
        out_specs=pl.BlockSpec((1, tq, qw), lambda b, j: (b, j, 0)),
        out_shape=jax.ShapeDtypeStruct((batch, seq, qw), BF16),
        scratch_shapes=[pltpu.VMEM((tk, (n_q_heads // n_kv_heads) * tq), F32) for _ in range(2)],
        compiler_params=_params("parallel", "arbitrary"),
        name="dense_attention",
    )(q.reshape(batch, seq, qw), k.reshape(batch, seq, kw), vt)
    return out.reshape(batch * seq, qw)


def _merge_outproj_kernel(*refs, dilations):
    n_g = len(dilations)
    o_refs, l_refs = refs[:n_g], refs[n_g:2 * n_g]
    w_ref, x_ref, g_ref, b_ref, y_ref = refs[2 * n_g:2 * n_g + 5]
    stage = refs[2 * n_g + 5:]
    tm = y_ref.shape[0]

    def token_order(ref, d, buf):
        if d == 1:
            return ref[...].astype(F32)
        for r in range(d):
            rows = ref[0, r].astype(F32)
            for c in range(buf.shape[0]):
                buf[c, pl.ds(r, tm // d, stride=d), :] = rows[:, c * LANES:(c + 1) * LANES]
        return jnp.concatenate([buf[c] for c in range(buf.shape[0])], axis=-1)

    os_, ls_, used = [], [], 0
    for p, d in enumerate(dilations):
        os_.append(token_order(o_refs[p], d, stage[used] if d > 1 else None))
        ls_.append(token_order(l_refs[p], d, stage[used + 1] if d > 1 else None))
        used += 2 if d > 1 else 0
    m = functools.reduce(jnp.maximum, ls_)
    es = [jnp.exp(l - m) for l in ls_]
    o = sum(e * o_p for e, o_p in zip(es, os_)) / sum(es)
    h = _dot(o.astype(BF16), w_ref[...])
    y_ref[...] = _layer_norm(DEEPNORM_ALPHA * x_ref[...] + h, g_ref[...], b_ref[...])


def _merge_outproj_ln(os_, lses, w, x2, g, b, *, n, seq, dilations):
    d_model = x2.shape[1]
    tm = TOKEN_TILE
    tiles = seq // tm
    kdim = w.shape[0]
    row = lambda i: (i, 0)
    fixed = lambda i: (0, 0)
    acts = list(os_) + list(lses)
    kern = functools.partial(_merge_outproj_kernel, dilations=tuple(dilations))
    act_specs = []
    for d in list(dilations) * 2:
        if d == 1:
            act_specs.append(pl.BlockSpec((tm, kdim), row))
        else:
            act_specs.append(pl.BlockSpec((1, d, tm // d, kdim), lambda i: (i // tiles, 0, i % tiles, 0)))
    scratch = [pltpu.VMEM((kdim // LANES, tm, LANES), F32) for d in dilations if d > 1 for _ in range(2)]
    return pl.pallas_call(
        kern,
        grid=(n // tm,),
        in_specs=act_specs + [
            pl.BlockSpec((kdim, d_model), fixed),
            pl.BlockSpec((tm, d_model), row),
            pl.BlockSpec((1, d_model), fixed),
            pl.BlockSpec((1, d_model), fixed),
        ],
        out_specs=pl.BlockSpec((tm, d_model), row),
        out_shape=jax.ShapeDtypeStruct((n, d_model), F32),
        scratch_shapes=scratch,
        compiler_params=_params("parallel"),
        name="outproj_ln",
    )(*acts, w, x2, g, b)


def _matmul_kernel(x_ref, w_ref, o_ref):
    o_ref[...] = _dot(x_ref[...].astype(BF16), w_ref[...]).astype(o_ref.dtype)


def _matmul(x2, w, out_dtype, tm):
    n, kdim = x2.shape
    cols = w.shape[1]
    return pl.pallas_call(
        _matmul_kernel,
        grid=(n // tm,),
        in_specs=[pl.BlockSpec((tm, kdim), lambda i: (i, 0)), pl.BlockSpec((kdim, cols), lambda i: (0, 0))],
        out_specs=pl.BlockSpec((tm, cols), lambda i: (i, 0)),
        out_shape=jax.ShapeDtypeStruct((n, cols), out_dtype),
        compiler_params=_params("parallel"),
        name="matmul",
    )(x2, w)


def _memattn_kernel(*refs, with_outproj):
    if with_outproj:
        o_ref, wmix_ref, gmix_ref, bmix_ref = refs[:4]
        refs = refs[4:]
    x_ref, wq_ref, kv_ref, wo_ref, g_ref, b_ref, rw_ref, rb_ref, y_ref, cls_ref, gate_ref = refs
    x = x_ref[...]
    if with_outproj:
        x = _layer_norm(DEEPNORM_ALPHA * x + _dot(o_ref[...], wmix_ref[...]), gmix_ref[...], bmix_ref[...])
    q = (_dot(x.astype(BF16), wq_ref[...]) * (MEM_HEAD_DIM ** -0.5)).astype(BF16)
    outs = []
    for h in range(MEM_HEADS):
        cs = slice(h * MEM_HEAD_DIM, (h + 1) * MEM_HEAD_DIM)
        kh = kv_ref[0, :, cs]
        vh = kv_ref[0, :, D_MODEL + h * MEM_HEAD_DIM:D_MODEL + (h + 1) * MEM_HEAD_DIM]
        s = _dot_nt(q[:, cs], kh)
        m = jnp.max(s, axis=-1, keepdims=True)
        p = jnp.exp(s - m)
        denom = jnp.sum(p, axis=-1, keepdims=True)
        outs.append(_dot(p.astype(BF16), vh) / denom)
    o = jnp.concatenate(outs, axis=-1).astype(BF16)
    h_out = _dot(o, wo_ref[...])
    y = _layer_norm(DEEPNORM_ALPHA * x + h_out, g_ref[...], b_ref[...])
    y_ref[...] = y
    cls, gates = _route(y, rw_ref, rb_ref)
    cls_ref[0] = cls
    gate_ref[0] = gates


def _memory_attention(x2, kv, wq, wo, g, b, rw_cat, rb_t, *, seq, mixer=None):
    d = x2.shape[1]
    batch, n_mem = kv.shape[0], kv.shape[1]
    tm = TOKEN_TILE
    tiles = seq // tm
    n = batch * seq
    fixed = lambda bi, i: (0, 0)
    row = lambda bi, i: (bi * tiles + i, 0)
    mixer_args, mixer_specs = [], []
    if mixer is not None:
        o, w_mix, g_mix, b_mix = mixer
        mixer_args = [o, w_mix, g_mix, b_mix]
        mixer_specs = [pl.BlockSpec((tm, o.shape[1]), row), pl.BlockSpec(w_mix.shape, fixed),
                       pl.BlockSpec((1, d), fixed), pl.BlockSpec((1, d), fixed)]
    y, cls, gates = pl.pallas_call(
        functools.partial(_memattn_kernel, with_outproj=mixer is not None),
        grid=(batch, tiles),
        in_specs=mixer_specs + [
            pl.BlockSpec((tm, d), row),
            pl.BlockSpec((d, d), fixed),
            pl.BlockSpec((1, n_mem, 2 * d), lambda bi, i: (bi, 0, 0)),
            pl.BlockSpec((d, d), fixed),
            pl.BlockSpec((1, d), fixed),
            pl.BlockSpec((1, d), fixed),
            pl.BlockSpec((d, 2 * LANES), fixed),
            pl.BlockSpec((N_EXPERTS, tm), fixed),
        ],
        out_specs=[
            pl.BlockSpec((tm, d), lambda bi, i: (bi * tiles + i, 0)),
            pl.BlockSpec((1, 1, tm), lambda bi, i: (bi * tiles + i, 0, 0)),
            pl.BlockSpec((1, 2, tm), lambda bi, i: (bi * tiles + i, 0, 0)),
        ],
        out_shape=[
            jax.ShapeDtypeStruct((n, d), F32),
            jax.ShapeDtypeStruct((n // tm, 1, tm), jnp.int32),
            jax.ShapeDtypeStruct((n // tm, 2, tm), F32),
        ],
        compiler_params=_params("parallel", "parallel"),
        name="memory_attention",
    )(*mixer_args, x2, wq, kv, wo, g, b, rw_cat, rb_t)
    return y, cls.reshape(n), gates[:, 0, :].reshape(n), gates[:, 1, :].reshape(n)


def _route(x, w_ref, bias_ref):
    xh = x.astype(BF16)
    xl = (x - xh.astype(F32)).astype(BF16)
    both = _dot(xh, w_ref[...])
    logits = both[:, :LANES] + both[:, LANES:] + _dot(xl, w_ref[:, :LANES])
    lt = logits.T[:N_EXPERTS]
    scores = 1.0 / (1.0 + jnp.exp(-lt))
    biased = scores + bias_ref[...]
    ng = N_EXPERT_GROUPS
    a = [biased[l * ng:(l + 1) * ng] for l in range(EXPERTS_PER_GROUP)]
    sc = [scores[l * ng:(l + 1) * ng] for l in range(EXPERTS_PER_GROUP)]
    gs = None
    for i in range(EXPERTS_PER_GROUP):
        for k in range(i + 1, EXPERTS_PER_GROUP):
            pair = a[i] + a[k]
            gs = pair if gs is None else jnp.maximum(gs, pair)
    gidx = lax.broadcasted_iota(jnp.int32, gs.shape, 0).astype(F32)
    gmax = jnp.max(gs, axis=0, keepdims=True)
    grp_f = jnp.min(jnp.where(gs == gmax, gidx, float(ng)), axis=0, keepdims=True)
    sel = gidx == grp_f
    grp = grp_f.astype(jnp.int32)
    v = [jnp.sum(jnp.where(sel, a[l], 0.0), axis=0, keepdims=True) for l in range(EXPERTS_PER_GROUP)]
    sv = [jnp.sum(jnp.where(sel, sc[l], 0.0), axis=0, keepdims=True) for l in range(EXPERTS_PER_GROUP)]

    def first_argmax(vals, excluded):
        best = None
        for l, val in enumerate(vals):
            cand = val if excluded is None else jnp.where(excluded == l, NEG_INF, val)
            best = cand if best is None else jnp.maximum(best, cand)
        idx = jnp.full(best.shape, EXPERTS_PER_GROUP - 1, jnp.int32)
        for l in range(EXPERTS_PER_GROUP - 1, -1, -1):
            hit = vals[l] == best
            if excluded is not None:
                hit = hit & (excluded != l)
            idx = jnp.where(hit, l, idx)
        return idx

    i1 = first_argmax(v, None)
    i2 = first_argmax(v, i1)

    def pick(vals, idx):
        out = vals[0]
        for l in range(1, EXPERTS_PER_GROUP):
            out = jnp.where(idx == l, vals[l], out)
        return out

    s1 = pick(sv, i1)
    s2 = pick(sv, i2)
    tot = s1 + s2
    lo = jnp.minimum(i1, i2)
    hi = jnp.maximum(i1, i2)
    first_is_lo = i1 < i2
    g_lo = jnp.where(first_is_lo, s1, s2) / tot
    g_hi = jnp.where(first_is_lo, s2, s1) / tot
    pair_idx = jnp.right_shift(lo * (7 - lo), 1) + (hi - lo - 1)
    return grp * N_PAIRS + pair_idx, jnp.concatenate([g_lo, g_hi], axis=0)


def _moe_kernel(src_ref, dst_ref, lo_ref, hi_ref, nblk_ref,
                x_hbm, gate_ref, gu_lo_ref, gu_hi_ref, dn_lo_ref, dn_hi_ref, g_ref, b_ref,
                y_hbm, xbuf, ybuf, gsem, ssem, *, n_tokens):
    blk = pl.program_id(0)
    n_used = nblk_ref[0]
    rows = MOE_BLOCK

    def gather_copy(block, i, slot):
        row = src_ref[block * rows + i]
        return pltpu.make_async_copy(x_hbm.at[pl.ds(row, 1)], xbuf.at[slot, pl.ds(i, 1)], gsem.at[slot])

    def scatter_copy(block, i, slot):
        row = dst_ref[block * rows + i]
        return pltpu.make_async_copy(ybuf.at[slot, pl.ds(i, 1)], y_hbm.at[pl.ds(row, 1)], ssem.at[slot])

    def start_all(copy, block, slot):
        for i in range(rows):
            copy(block, i, slot).start()

    def wait_all(buf, sem, slot):
        pltpu.make_async_copy(buf.at[slot], buf.at[slot], sem.at[slot]).wait()

    @pl.when(blk < n_used)
    def _():
        slot = blk % 2
        nxt = jnp.minimum(blk + 1, n_used - 1)

        @pl.when(blk == 0)
        def _():
            start_all(gather_copy, 0, 0)
            ybuf[1] = jnp.zeros(ybuf.shape[1:], ybuf.dtype)
            for part in range(2):
                spare = pltpu.make_async_copy(ybuf.at[1], y_hbm.at[pl.ds(n_tokens + part * rows, rows)], ssem.at[1])
                spare.start()
                spare.wait()

        @pl.when(blk >= 2)
        def _():
            wait_all(ybuf, ssem, slot)

        wait_all(xbuf, gsem, slot)
        start_all(gather_copy, nxt, 1 - slot)

        x = xbuf[slot]
        xb = x.astype(BF16)

        def expert(gu_ref, dn_ref):
            gu = _dot(xb, gu_ref[0])
            gate_act = gu[:, :D_EXPERT]
            act = gate_act / (1.0 + jnp.exp(-gate_act)) * gu[:, D_EXPERT:]
            return _dot(act.astype(BF16), dn_ref[0])

        r_i = lax.broadcasted_iota(jnp.int32, (rows, rows), 0)
        c_i = lax.broadcasted_iota(jnp.int32, (rows, rows), 1)
        diag = r_i == c_i
        gates = gate_ref[0]
        g_lo = jnp.sum(jnp.where(diag, gates[0:1, :], 0.0), axis=1, keepdims=True)
        g_hi = jnp.sum(jnp.where(diag, gates[1:2, :], 0.0), axis=1, keepdims=True)
        h = g_lo * expert(gu_lo_ref, dn_lo_ref) + g_hi * expert(gu_hi_ref, dn_hi_ref)
        ybuf[slot] = _layer_norm(DEEPNORM_ALPHA * x + h, g_ref[...], b_ref[...])
        start_all(scatter_copy, blk, slot)

        @pl.when(blk == n_used - 1)
        def _():
            wait_all(xbuf, gsem, 1 - slot)

            @pl.when(blk >= 1)
            def _():
                wait_all(ybuf, ssem, 1 - slot)
            wait_all(ybuf, ssem, slot)


def _moe_experts(x2, slot_src, slot_dst, blk_lo, blk_hi, n_used, slot_gates, w_gu, w_dn, g, b, *, n):
    d = x2.shape[1]
    n_blocks = blk_lo.shape[0]
    rows = MOE_BLOCK
    grid_spec = pltpu.PrefetchScalarGridSpec(
        num_scalar_prefetch=5,
        grid=(n_blocks,),
        in_specs=[
            pl.BlockSpec(memory_space=pl.ANY),
            pl.BlockSpec((1, 2, rows), lambda i, *_: (i, 0, 0)),
            pl.BlockSpec((1, d, 2 * D_EXPERT), lambda i, src, dst, lo, hi, nb: (lo[i], 0, 0)),
            pl.BlockSpec((1, d, 2 * D_EXPERT), lambda i, src, dst, lo, hi, nb: (hi[i], 0, 0)),
            pl.BlockSpec((1, D_EXPERT, d), lambda i, src, dst, lo, hi, nb: (lo[i], 0, 0)),
            pl.BlockSpec((1, D_EXPERT, d), lambda i, src, dst, lo, hi, nb: (hi[i], 0, 0)),
            pl.BlockSpec((1, d), lambda i, *_: (0, 0)),
            pl.BlockSpec((1, d), lambda i, *_: (0, 0)),
        ],
        out_specs=pl.BlockSpec(memory_space=pl.ANY),
        scratch_shapes=[
            pltpu.VMEM((2, rows, d), F32),
            pltpu.VMEM((2, rows, d), F32),
            pltpu.SemaphoreType.DMA((2,)),
            pltpu.SemaphoreType.DMA((2,)),
        ],
    )
    return pl.pallas_call(
        functools.partial(_moe_kernel, n_tokens=n),
        grid_spec=grid_spec,
        out_shape=jax.ShapeDtypeStruct((n + 2 * rows, d), F32),
        compiler_params=_params("arbitrary"),
        name="moe_experts",
    )(slot_src, slot_dst, blk_lo, blk_hi, n_used, x2, slot_gates, w_gu, w_gu, w_dn, w_dn, g, b)


_PAIR_LO = np.array([0, 0, 0, 1, 1, 2], np.int32)
_PAIR_HI = np.array([1, 2, 3, 2, 3, 3], np.int32)


def _dispatch_tables(cls, g_lo, g_hi):
    n = cls.shape[0]
    rows = MOE_BLOCK
    n_blocks = n // rows + N_CLASSES
    i32 = jnp.int32
    cls_ids = jnp.arange(N_CLASSES, dtype=i32)
    _, order, glo_s, ghi_s = lax.sort((cls, jnp.arange(n, dtype=i32), g_lo, g_hi), num_keys=1)
    counts = jnp.sum((cls[:, None] == cls_ids[None, :]).astype(i32), axis=0)
    cls_blocks = (counts + rows - 1) // rows
    blk_end = jnp.cumsum(cls_blocks)
    blk_begin = blk_end - cls_blocks
    start = jnp.cumsum(counts) - counts
    n_used = blk_end[-1]
    blk = jnp.arange(n_blocks, dtype=i32)
    blk_eff = jnp.minimum(blk, n_used - 1)
    blk_cls = jnp.sum((blk_end[None, :] <= blk_eff[:, None]).astype(i32), axis=1)
    pick = blk_cls[:, None] == cls_ids[None, :]
    table = lambda t: jnp.sum(jnp.where(pick, t[None, :], 0), axis=1)
    blk_rank = (blk_eff - table(blk_begin)) * rows
    blk_nvalid = jnp.where(blk < n_used, jnp.clip(table(counts) - blk_rank, 0, rows), 0)
    row = jnp.arange(rows, dtype=i32)
    valid = row[None, :] < blk_nvalid[:, None]
    src = jnp.clip((table(start) + blk_rank)[:, None] + row[None, :], 0, n - 1)
    tok = order[src]
    spare = n + (blk % 2)[:, None] * rows + row[None, :]
    slot_src = jnp.where(valid, tok, 0).reshape(n_blocks * rows)
    slot_dst = jnp.where(valid, tok, spare).reshape(n_blocks * rows)
    slot_gates = jnp.stack([jnp.where(valid, glo_s[src], 0.0), jnp.where(valid, ghi_s[src], 0.0)], axis=1)
    grp = blk_cls // N_PAIRS
    pair = blk_cls % N_PAIRS
    pair_ids = jnp.arange(N_PAIRS, dtype=i32)
    pair_pick = pair[:, None] == pair_ids[None, :]
    blk_lo = grp * EXPERTS_PER_GROUP + jnp.sum(jnp.where(pair_pick, jnp.asarray(_PAIR_LO)[None, :], 0), axis=1)
    blk_hi = grp * EXPERTS_PER_GROUP + jnp.sum(jnp.where(pair_pick, jnp.asarray(_PAIR_HI)[None, :], 0), axis=1)
    return (slot_src.astype(i32), slot_dst.astype(i32), blk_lo.astype(i32), blk_hi.astype(i32),
            n_used.reshape(1).astype(i32), slot_gates)


def _tile_heads(t64):
    return jnp.concatenate([t64] * (LANES // HEAD_DIM), axis=-1)


def _rope_tables(seq):
    t = jnp.arange(seq, dtype=jnp.int32)

    def tables(pos, dim):
        inv_freq = ROPE_THETA ** (-jnp.arange(0, dim, 2, dtype=F32) / dim)
        ang = pos.astype(F32)[:, None] * inv_freq[None, :]
        return jnp.cos(ang), jnp.sin(ang)

    c1, s1 = tables(t, HEAD_DIM)
    cos_1d = _tile_heads(jnp.concatenate([c1, c1], axis=-1))
    sin_1d = _tile_heads(jnp.concatenate([-s1, s1], axis=-1))
    cr, sr = tables(t // GRID_W, HEAD_DIM // 2)
    cc, sc = tables(t % GRID_W, HEAD_DIM // 2)
    cos_ax = _tile_heads(jnp.concatenate([cr, cr, cc, cc], axis=-1))
    sin_ax = _tile_heads(jnp.concatenate([-sr, sr, -sc, sc], axis=-1))
    return (cos_1d, sin_1d), (cos_ax, sin_ax)


def _block_diag_ones():
    i = np.arange(2 * LANES)
    return jnp.asarray((i[:, None] // HEAD_DIM) == (i[None, :] // HEAD_DIM), BF16)


def kernel(x_prompt, x_sample, mem_prompt, mem_sample, a_w_in, a_sink, a_w_out, b_w_in, b_q_norm, b_k_norm,
           b_w_out, c_w_in, c_w_out, m_w_q, m_w_kv, m_w_out, ln_g, ln_b, router_w, router_bias, e_w_gu, e_w_down):
    seq = x_prompt.shape[1]
    assert x_sample.shape[1] == seq
    x = jnp.concatenate([x_prompt, x_sample], axis=0)
    mem = jnp.concatenate([mem_prompt, mem_sample], axis=0)
    batch = x.shape[0]
    n = batch * seq
    n_mem = mem.shape[1]
    x2 = x.reshape(n, D_MODEL)

    rope_1d, rope_ax = _rope_tables(seq)
    ones_bd = _block_diag_ones()
    unit_gains = jnp.ones((2, LANES), F32)
    scale = HEAD_DIM ** -0.5

    rw = router_w.astype(F32).reshape(D_MODEL, N_EXPERT_GROUPS, EXPERTS_PER_GROUP).transpose(0, 2, 1)
    rw = jnp.pad(rw.reshape(D_MODEL, N_EXPERTS), ((0, 0), (0, LANES - N_EXPERTS)))
    rw_hi = rw.astype(BF16)
    rw_cat = jnp.concatenate([rw_hi, (rw - rw_hi.astype(F32)).astype(BF16)], axis=1)
    rb = router_bias.astype(F32).reshape(N_EXPERT_GROUPS, EXPERTS_PER_GROUP).T.reshape(N_EXPERTS, 1)
    rb_t = jnp.broadcast_to(rb, (N_EXPERTS, TOKEN_TILE))

    cq, ck = C_HEADS * HEAD_DIM, C_KV_HEADS * HEAD_DIM

    for i in range(DEPTH):
        kind, j = i % N_MIXERS, i // N_MIXERS
        lg = lambda s_: ln_g[i, s_].astype(F32).reshape(1, D_MODEL)
        lb = lambda s_: ln_b[i, s_].astype(F32).reshape(1, D_MODEL)
        if kind == 0:
            nq, nk = A_HEADS * HEAD_DIM, A_KV_HEADS * HEAD_DIM
            q, k, v = _qkv_project(x2, a_w_in[j].astype(BF16), rope_1d[0], rope_1d[1], unit_gains, ones_bd, n=n,
                                   nq=nq, nk=nk, nv=nk, half=HEAD_DIM // 2, qk_norm=False,
                                   q_scale=scale * LOG2_E, seq=seq)
            o, = _banded_attention(q, k, v, a_sink[j].astype(F32), batch=batch, seq=seq, n_q_heads=A_HEADS,
                                   n_kv_heads=A_KV_HEADS, window=A_WINDOW, tq=BANDED_QUERY_BLOCK, with_lse=False)
            mixer = (o, a_w_out[j].astype(BF16), lg(0), lb(0))
        elif kind == 1:
            nq, nk = B_HEADS * HEAD_DIM, B_KV_HEADS * HEAD_DIM
            gains = jnp.stack([_tile_heads(b_q_norm[j].astype(F32)), _tile_heads(b_k_norm[j].astype(F32))], axis=0)
            q, k, v = _qkv_project(x2, b_w_in[j].astype(BF16), rope_ax[0], rope_ax[1], gains, ones_bd, n=n,
                                   nq=nq, nk=nk, nv=nk, half=HEAD_DIM // 4, qk_norm=True,
                                   q_scale=scale * LOG2_E, seq=seq, v_transposed=True)
            o = _flash_attention(q, k, v, batch=batch, seq=seq, n_q_heads=B_HEADS, n_kv_heads=B_KV_HEADS,
                                 tq=256, tk=1024)
            mixer = (o, b_w_out[j].astype(BF16), lg(0), lb(0))
        else:
            dilations = [dil for _, dil in C_PATTERNS]
            qkv = _qkv_project_dilated(x2, c_w_in[j].astype(BF16), rope_1d[0], rope_1d[1], n=n, seq=seq,
                                       dilations=dilations, cq=cq, ck=ck, q_scale=scale * LOG2_E)
            no_sink = jnp.full((C_HEADS,), NEG_INF, F32)
            os_, lses = [], []
            for (window, dil), (q, k, v) in zip(C_PATTERNS, qkv):
                half_w = (window // 2) // dil
                o, lse = _banded_attention(q.reshape(n, cq), k.reshape(n, ck), v.reshape(n, ck), no_sink,
                                           batch=batch * dil, seq=seq // dil, n_q_heads=C_HEADS,
                                           n_kv_heads=C_KV_HEADS, window=half_w, tq=BANDED_QUERY_BLOCK,
                                           with_lse=True)
                shape = (n, cq) if dil == 1 else (batch, dil, seq // dil, cq)
                os_.append(o.reshape(shape))
                lses.append(lse.reshape(shape))
            x2 = _merge_outproj_ln(os_, lses, c_w_out[j].astype(BF16), x2, lg(0), lb(0), n=n, seq=seq,
                                   dilations=dilations)
            mixer = None

        kv = _matmul(mem.reshape(batch * n_mem, D_MODEL), m_w_kv[i].astype(BF16), BF16, tm=n_mem)
        x2, cls, g_lo, g_hi = _memory_attention(x2, kv.reshape(batch, n_mem, 2 * D_MODEL), m_w_q[i].astype(BF16),
                                                m_w_out[i].astype(BF16), lg(1), lb(1), rw_cat, rb_t, seq=seq,
                                                mixer=mixer)
        slot_src, slot_dst, blk_lo, blk_hi, n_used, slot_gates = _dispatch_tables(cls, g_lo, g_hi)
        x2 = _moe_experts(x2, slot_src, slot_dst, blk_lo, blk_hi, n_used, slot_gates,
                          e_w_gu[i].astype(BF16), e_w_down[i].astype(BF16), lg(2), lb(2), n=n)

    n_prompt = x_prompt.shape[0] * seq
    return (x2[:n_prompt].reshape(x_prompt.shape), x2[n_prompt:n].reshape(x_sample.shape))
```

```python
import functools

import jax
import jax.numpy as jnp
import numpy as np
from jax import lax
from jax.experimental import pallas as pl
from jax.experimental.pallas import tpu as pltpu

F32 = jnp.float32
BF16 = jnp.bfloat16

D_MODEL = 1024
DEPTH = 4
N_MIXERS = 3
HEAD_DIM = 64
ROPE_THETA = 10000.0
GRID_W = 64

A_HEADS, A_KV_HEADS, A_WINDOW = 16, 4, 128
B_HEADS, B_KV_HEADS = 16, 4
C_PATTERNS = ((128, 1), (512, 4), (2048, 16))
C_HEADS, C_KV_HEADS = 8, 2

MEM_HEADS = 4
MEM_HEAD_DIM = D_MODEL // MEM_HEADS

N_EXPERTS = 32
N_EXPERT_GROUPS = 8
EXPERTS_PER_GROUP = 4
D_EXPERT = D_MODEL // 4
N_PAIRS = 6
N_CLASSES = N_EXPERT_GROUPS * N_PAIRS
MOE_BLOCK = 128

LN_EPS = 1e-5
QK_NORM_EPS = 1e-6
DEEPNORM_ALPHA = (2 * DEPTH) ** 0.25

LANES = 128
TOKEN_TILE = 512
BANDED_QUERY_BLOCK = 512
BANDED_SUB_BLOCK = 128
FLASH_CHUNKS_PER_BODY = 4
VMEM_LIMIT_BYTES = 48 * 1024 * 1024

NEG_INF = float("-inf")
LOG2_E = 1.4426950408889634
LN_2 = 0.6931471805599453


def _params(*semantics):
    return pltpu.CompilerParams(dimension_semantics=semantics, vmem_limit_bytes=VMEM_LIMIT_BYTES)


def _layer_norm(z, g, b):
    mu = jnp.mean(z, axis=-1, keepdims=True)
    zc = z - mu
    var = jnp.mean(zc * zc, axis=-1, keepdims=True)
    return zc * lax.rsqrt(var + LN_EPS) * g + b


def _dot(a, b):
    return jnp.dot(a, b, preferred_element_type=F32)


def _dot_nt(a, b):
    return lax.dot_general(a, b, (((1,), (1,)), ((), ())), preferred_element_type=F32)


def _qkv_kernel(x_ref, w_ref, cos_ref, sin_ref, gain_ref, ones_ref, q_ref, k_ref, v_ref, *,
                nq, nk, half, qk_norm, q_scale, v_transposed):
    acc = _dot(x_ref[...].astype(BF16), w_ref[...])
    cos = cos_ref[...]
    sin = sin_ref[...]
    lane = lax.broadcasted_iota(jnp.int32, cos.shape, 1)
    first = (lane % (2 * half)) < half

    mean_sq = {}
    if qk_norm:
        width = ones_ref.shape[0]
        for c2 in range((nq + nk) // width):
            a2 = jnp.square(acc[:, c2 * width:(c2 + 1) * width])
            hi = a2.astype(BF16)
            lo = (a2 - hi.astype(F32)).astype(BF16)
            ms = (_dot(hi, ones_ref[...]) + _dot(lo, ones_ref[...])) * (1.0 / HEAD_DIM)
            for t in range(width // LANES):
                mean_sq[c2 * (width // LANES) + t] = ms[:, t * LANES:(t + 1) * LANES]

    def rotate(c, gain, scale):
        a = acc[:, c * LANES:(c + 1) * LANES]
        if qk_norm:
            a = a * lax.rsqrt(mean_sq[c] + QK_NORM_EPS) * gain
        partner = jnp.where(first, pltpu.roll(a, LANES - half, 1), pltpu.roll(a, half, 1))
        r = a * cos + partner * sin
        if scale != 1.0:
            r = r * scale
        return r.astype(BF16)

    gq = gain_ref[0:1, :]
    gk = gain_ref[1:2, :]
    for c in range(nq // LANES):
        q_ref[:, c * LANES:(c + 1) * LANES] = rotate(c, gq, q_scale)
    for c in range(nk // LANES):
        k_ref[:, c * LANES:(c + 1) * LANES] = rotate(nq // LANES + c, gk, 1.0)
    if v_transposed:
        v_ref[0] = acc[:, nq + nk:].T.astype(BF16)
    else:
        v_ref[...] = acc[:, nq + nk:].astype(BF16)


def _qkv_project(x2, w, cos, sin, gains, ones_bd, *, n, nq, nk, nv, half, qk_norm, q_scale, seq,
                 v_transposed=False):
    d = x2.shape[1]
    tm = TOKEN_TILE
    tiles_per_seq = seq // tm
    kern = functools.partial(_qkv_kernel, nq=nq, nk=nk, half=half, qk_norm=qk_norm, q_scale=q_scale,
                             v_transposed=v_transposed)
    if v_transposed:
        v_spec = pl.BlockSpec((1, nv, tm), lambda i: (i, 0, 0))
        v_shape = jax.ShapeDtypeStruct((n // tm, nv, tm), BF16)
    else:
        v_spec = pl.BlockSpec((tm, nv), lambda i: (i, 0))
        v_shape = jax.ShapeDtypeStruct((n, nv), BF16)
    return pl.pallas_call(
        kern,
        grid=(n // tm,),
        in_specs=[
            pl.BlockSpec((tm, d), lambda i: (i, 0)),
            pl.BlockSpec((d, nq + nk + nv), lambda i: (0, 0)),
            pl.BlockSpec((tm, LANES), lambda i: (i % tiles_per_seq, 0)),
            pl.BlockSpec((tm, LANES), lambda i: (i % tiles_per_seq, 0)),
            pl.BlockSpec((2, LANES), lambda i: (0, 0)),
            pl.BlockSpec(ones_bd.shape, lambda i: (0, 0)),
        ],
        out_specs=[
            pl.BlockSpec((tm, nq), lambda i: (i, 0)),
            pl.BlockSpec((tm, nk), lambda i: (i, 0)),
            v_spec,
        ],
        out_shape=[
            jax.ShapeDtypeStruct((n, nq), BF16),
            jax.ShapeDtypeStruct((n, nk), BF16),
            v_shape,
        ],
        compiler_params=_params("parallel"),
        name="qkv_project",
    )(x2, w, cos, sin, gains, ones_bd)


def _qkv_dilated_kernel(x_ref, w_ref, cos_ref, sin_ref, *refs, dilations, cq, ck, q_scale):
    n_g = len(dilations)
    out_refs, acc_ref = refs[:3 * n_g], refs[3 * n_g]
    acc = _dot(x_ref[...].astype(BF16), w_ref[...])
    for c in range(acc_ref.shape[0]):
        acc_ref[c] = acc[:, c * LANES:(c + 1) * LANES]
    tm = acc_ref.shape[1]
    half = HEAD_DIM // 2
    gw = cq + 2 * ck

    for p, d in enumerate(dilations):
        q_ref, k_ref, v_ref = out_refs[3 * p:3 * p + 3]
        lane = lax.broadcasted_iota(jnp.int32, (tm // d, LANES), 1)
        first = (lane % (2 * half)) < half
        for r in range(d):
            rows = slice(None) if d == 1 else pl.ds(r, tm // d, stride=d)
            cos = cos_ref[rows, :]
            sin = sin_ref[rows, :]

            def put(ref, c, val):
                if d == 1:
                    ref[:, c * LANES:(c + 1) * LANES] = val.astype(BF16)
                else:
                    ref[0, r, :, c * LANES:(c + 1) * LANES] = val.astype(BF16)

            def rotate(a, scale):
                partner = jnp.where(first, pltpu.roll(a, LANES - half, 1), pltpu.roll(a, half, 1))
                return (a * cos + partner * sin) * scale

            for c in range(cq // LANES):
                put(q_ref, c, rotate(acc_ref[(p * gw) // LANES + c, rows, :], q_scale))
            for c in range(ck // LANES):
                put(k_ref, c, rotate(acc_ref[(p * gw + cq) // LANES + c, rows, :], 1.0))
                put(v_ref, c, acc_ref[(p * gw + cq + ck) // LANES + c, rows, :])


def _qkv_project_dilated(x2, w, cos, sin, *, n, seq, dilations, cq, ck, q_scale):
    d_model = x2.shape[1]
    tm = TOKEN_TILE
    tiles = seq // tm
    out_specs, out_shape = [], []
    for d in dilations:
        for cols in (cq, ck, ck):
            if d == 1:
                out_specs.append(pl.BlockSpec((tm, cols), lambda i: (i, 0)))
                out_shape.append(jax.ShapeDtypeStruct((n, cols), BF16))
            else:
                out_specs.append(pl.BlockSpec((1, d, tm // d, cols), lambda i: (i // tiles, 0, i % tiles, 0)))
                out_shape.append(jax.ShapeDtypeStruct((n // seq, d, seq // d, cols), BF16))
    kern = functools.partial(_qkv_dilated_kernel, dilations=tuple(dilations), cq=cq, ck=ck, q_scale=q_scale)
    res = pl.pallas_call(
        kern,
        grid=(n // tm,),
        in_specs=[
            pl.BlockSpec((tm, d_model), lambda i: (i, 0)),
            pl.BlockSpec(w.shape, lambda i: (0, 0)),
            pl.BlockSpec((tm, LANES), lambda i: (i % tiles, 0)),
            pl.BlockSpec((tm, LANES), lambda i: (i % tiles, 0)),
        ],
        out_specs=out_specs,
        out_shape=out_shape,
        scratch_shapes=[pltpu.VMEM((w.shape[1] // LANES, tm, LANES), F32)],
        compiler_params=_params("parallel"),
        name="qkv_project_dilated",
    )(x2, w, cos, sin)
    return [tuple(res[3 * p:3 * p + 3]) for p in range(len(dilations))]


def _banded_kernel(sink_ref, band_ref, q_ref, kp_ref, kc_ref, kn_ref, vp_ref, vc_ref, vn_ref, *out_refs,
                   tq, sub, w, n_q_heads, n_kv_heads, with_lse):
    o_ref = out_refs[0]
    j = pl.program_id(1)
    g_per = n_q_heads // n_kv_heads
    n_sub = tq // sub
    span = tq + 2 * w
    sub_span = sub + 2 * w
    m_lanes = g_per * sub
    key = lax.broadcasted_iota(jnp.int32, (span, 1), 0)
    off_seq = ((key < w) & (j == 0)) | ((key >= w + tq) & (j == pl.num_programs(1) - 1))
    edge = jnp.where(off_seq, NEG_INF, 0.0)
    biases = [band_ref[...] + edge[sb * sub:sb * sub + sub_span] for sb in range(n_sub)]
    lane_row = lax.broadcasted_iota(jnp.int32, (1, m_lanes), 1)
    half_rows = lax.broadcasted_iota(jnp.int32, (2 * HEAD_DIM, m_lanes), 0) >= HEAD_DIM
    ones = jnp.ones((HEAD_DIM, sub_span), BF16)

    vcat = jnp.concatenate([vp_ref[0], vc_ref[0], vn_ref[0]], axis=0).astype(F32)
    vt_all = jnp.concatenate([vcat[:, c * LANES:(c + 1) * LANES].T for c in range(vcat.shape[1] // LANES)],
                             axis=0).astype(BF16)
    kcats = [jnp.concatenate([kp_ref[0, :, kh * HEAD_DIM:(kh + 1) * HEAD_DIM],
                              kc_ref[0, :, kh * HEAD_DIM:(kh + 1) * HEAD_DIM],
                              kn_ref[0, :, kh * HEAD_DIM:(kh + 1) * HEAD_DIM]], axis=0)
             for kh in range(n_kv_heads)]

    def scores(kh, sb):
        r0 = sb * sub
        qs = jnp.concatenate(
            [q_ref[0, r0:r0 + sub, (kh * g_per + g) * HEAD_DIM:(kh * g_per + g + 1) * HEAD_DIM]
             for g in range(g_per)], axis=0)
        return _dot_nt(kcats[kh][r0:r0 + sub_span], qs) + biases[sb]

    tasks = [(kh, sb) for kh in range(n_kv_heads) for sb in range(n_sub)]
    o_parts = [[None] * n_sub for _ in range(n_q_heads)]
    l_parts = [[None] * n_sub for _ in range(n_q_heads)]
    st_next = scores(*tasks[0])
    for t, (kh, sb) in enumerate(tasks):
        r0 = sb * sub
        st = st_next
        if t + 1 < len(tasks):
            st_next = scores(*tasks[t + 1])
        sink = jnp.full((1, m_lanes), sink_ref[kh * g_per], F32)
        for g in range(1, g_per):
            sink = jnp.where(lane_row >= g * sub, sink_ref[kh * g_per + g], sink)
        sink = sink * LOG2_E
        m = jnp.maximum(jnp.max(st, axis=0, keepdims=True), sink)
        p = jnp.exp2(st - m).astype(BF16)
        vext = jnp.concatenate([vt_all[kh * HEAD_DIM:(kh + 1) * HEAD_DIM, r0:r0 + sub_span], ones], axis=0)
        acc = _dot(vext, p)
        acc = jnp.where(half_rows, acc + jnp.exp2(sink - m), acc)
        num, den = acc[:HEAD_DIM], acc[HEAD_DIM:]
        low = jnp.log(den) + m * LN_2 if with_lse else den
        res = jnp.concatenate([num / den, low], axis=0).T
        for g in range(g_per):
            o_parts[kh * g_per + g][sb] = res[g * sub:(g + 1) * sub, :HEAD_DIM]
            l_parts[kh * g_per + g][sb] = res[g * sub:(g + 1) * sub, HEAD_DIM:]
    o_ref[0] = jnp.concatenate([jnp.concatenate(parts, axis=0) for parts in o_parts], axis=-1).astype(o_ref.dtype)
    if with_lse:
        out_refs[1][0] = jnp.concatenate([jnp.concatenate(parts, axis=0) for parts in l_parts], axis=-1)


def _banded_attention(q, k, v, sink, *, batch, seq, n_q_heads, n_kv_heads, window, tq, with_lse):
    qw = n_q_heads * HEAD_DIM
    kw = n_kv_heads * HEAD_DIM
    w = window
    r_blocks = tq // w
    nb_w = seq // w
    q3 = q.reshape(batch, seq, qw)
    k3 = k.reshape(batch, seq, kw)
    v3 = v.reshape(batch, seq, kw)

    def q_map(b, j):
        return (b, j, 0)

    def prev_map(b, j):
        return (b, jnp.maximum(j * r_blocks - 1, 0), 0)

    def next_map(b, j):
        return (b, jnp.minimum((j + 1) * r_blocks, nb_w - 1), 0)

    g_per = n_q_heads // n_kv_heads
    sub = BANDED_SUB_BLOCK
    rel = np.arange(sub + 2 * w)[:, None] - w - (np.arange(g_per * sub) % sub)[None, :]
    band = jnp.asarray(np.where(np.abs(rel) <= w, 0.0, NEG_INF), F32)

    kern = functools.partial(_banded_kernel, tq=tq, sub=sub, w=w, n_q_heads=n_q_heads,
                             n_kv_heads=n_kv_heads, with_lse=with_lse)
    out_specs = [pl.BlockSpec((1, tq, qw), q_map)]
    out_shape = [jax.ShapeDtypeStruct((batch, seq, qw), BF16)]
    if with_lse:
        out_specs.append(pl.BlockSpec((1, tq, qw), q_map))
        out_shape.append(jax.ShapeDtypeStruct((batch, seq, qw), F32))
    res = pl.pallas_call(
        kern,
        grid=(batch, seq // tq),
        in_specs=[
            pl.BlockSpec(memory_space=pltpu.SMEM),
            pl.BlockSpec(band.shape, lambda b, j: (0, 0)),
            pl.BlockSpec((1, tq, qw), q_map),
            pl.BlockSpec((1, w, kw), prev_map),
            pl.BlockSpec((1, tq, kw), q_map),
            pl.BlockSpec((1, w, kw), next_map),
            pl.BlockSpec((1, w, kw), prev_map),
            pl.BlockSpec((1, tq, kw), q_map),
            pl.BlockSpec((1, w, kw), next_map),
        ],
        out_specs=out_specs,
        out_shape=out_shape,
        compiler_params=_params("parallel", "parallel"),
        name="banded_attention",
    )(sink, band, q3, k3, k3, k3, v3, v3, v3)
    return [r.reshape(batch * seq, qw) for r in res]


def _flash_kernel(q_ref, k_ref, vt_ref, o_ref, st_a, st_b, *, tq, tk, tv, seq, n_kv_heads, g_per):
    ones = jnp.ones((HEAD_DIM, tk), BF16)
    outs = [None] * (n_kv_heads * g_per)
    for kh in range(n_kv_heads):
        cs = slice(kh * HEAD_DIM, (kh + 1) * HEAD_DIM)
        qs = jnp.concatenate(
            [q_ref[0, :, (kh * g_per + g) * HEAD_DIM:(kh * g_per + g + 1) * HEAD_DIM] for g in range(g_per)],
            axis=0)

        def scores(c, st_ref):
            start = pl.multiple_of(c * tk, tk)
            st_ref[...] = _dot_nt(k_ref[0, pl.ds(start, tk), cs], qs)

        def update(c, st_ref, m, acc):
            vt = jnp.concatenate([vt_ref[c * (tk // tv) + t, cs, :] for t in range(tk // tv)], axis=-1)
            vext = jnp.concatenate([vt, ones], axis=0)
            st = st_ref[...]
            m_new = jnp.maximum(m, jnp.max(st, axis=0, keepdims=True))
            alpha = jnp.exp2(m - m_new)
            p = jnp.exp2(st - m_new).astype(BF16)
            return m_new, alpha * acc + _dot(vext, p)

        bufs = (st_a, st_b)

        def group(cg, carry, last):
            m, acc = carry
            for t in range(FLASH_CHUNKS_PER_BODY):
                c = cg * FLASH_CHUNKS_PER_BODY + t
                if not (last and t == FLASH_CHUNKS_PER_BODY - 1):
                    scores(c + 1, bufs[(t + 1) % 2])
                m, acc = update(c, bufs[t % 2], m, acc)
            return m, acc

        n_groups = seq // (FLASH_CHUNKS_PER_BODY * tk)
        m0 = jnp.full((1, g_per * tq), NEG_INF, F32)
        acc0 = jnp.zeros((2 * HEAD_DIM, g_per * tq), F32)
        scores(0, st_a)
        carry = lax.fori_loop(0, n_groups - 1, lambda cg, cr: group(cg, cr, False), (m0, acc0))
        _, acc = group(n_groups - 1, carry, True)
        acc = acc.T
        o = acc[:, :HEAD_DIM] / acc[:, HEAD_DIM:]
        for g in range(g_per):
            outs[kh * g_per + g] = o[g * tq:(g + 1) * tq]
    o_ref[0] = jnp.concatenate(outs, axis=-1).astype(o_ref.dtype)


def _flash_attention(q, k, vt, *, batch, seq, n_q_heads, n_kv_heads, tq, tk):
    qw = n_q_heads * HEAD_DIM
    kw = n_kv_heads * HEAD_DIM
    tv = vt.shape[2]
    chunks = seq // tv
    kern = functools.partial(_flash_kernel, tq=tq, tk=tk, tv=tv, seq=seq, n_kv_heads=n_kv_heads,
                             g_per=n_q_heads // n_kv_heads)
    out = pl.pallas_call(
        kern,
        grid=(batch, seq // tq),
        in_specs=[
            pl.BlockSpec((1, tq, qw), lambda b, j: (b, j, 0)),
            pl.BlockSpec((1, seq, kw), lambda b, j: (b, 0, 0)),
            pl.BlockSpec((chunks, kw, tv), lambda b, j: (b, 0, 0)),
        ],
        out_specs=pl.BlockSpec((1, tq, qw), lambda b, j: (b, j, 0)),
        out_shape=jax.ShapeDtypeStruct((batch, seq, qw), BF16),
        scratch_shapes=[pltpu.VMEM((tk, (n_q_heads // n_kv_heads) * tq), F32) for _ in range(2)],
        compiler_params=_params("parallel", "arbitrary"),
        name="dense_attention",
    )(q.reshape(batch, seq, qw), k.reshape(batch, seq, kw), vt)
    return out.reshape(batch * seq, qw)


def _merge_outproj_kernel(*refs, dilations):
    n_g = len(dilations)
    o_refs, l_refs = refs[:n_g], refs[n_g:2 * n_g]
    w_ref, x_ref, g_ref, b_ref, y_ref = refs[2 * n_g:2 * n_g + 5]
    stage = refs[2 * n_g + 5:]
    tm = y_ref.shape[0]

    def token_order(ref, d, buf):
        if d == 1:
            return ref[...].astype(F32)
        for r in range(d):
            rows = ref[0, r].astype(F32)
            for c in range(buf.shape[0]):
                buf[c, pl.ds(r, tm // d, stride=d), :] = rows[:, c * LANES:(c + 1) * LANES]
        return jnp.concatenate([buf[c] for c in range(buf.shape[0])], axis=-1)

    os_, ls_, used = [], [], 0
    for p, d in enumerate(dilations):
        os_.append(token_order(o_refs[p], d, stage[used] if d > 1 else None))
        ls_.append(token_order(l_refs[p], d, stage[used + 1] if d > 1 else None))
        used += 2 if d > 1 else 0
    m = functools.reduce(jnp.maximum, ls_)
    es = [jnp.exp(l - m) for l in ls_]
    o = sum(e * o_p for e, o_p in zip(es, os_)) / sum(es)
    h = _dot(o.astype(BF16), w_ref[...])
    y_ref[...] = _layer_norm(DEEPNORM_ALPHA * x_ref[...] + h, g_ref[...], b_ref[...])


def _merge_outproj_ln(os_, lses, w, x2, g, b, *, n, seq, dilations):
    d_model = x2.shape[1]
    tm = TOKEN_TILE
    tiles = seq // tm
    kdim = w.shape[0]
    row = lambda i: (i, 0)
    fixed = lambda i: (0, 0)
    acts = list(os_) + list(lses)
    kern = functools.partial(_merge_outproj_kernel, dilations=tuple(dilations))
    act_specs = []
    for d in list(dilations) * 2:
        if d == 1:
            act_specs.append(pl.BlockSpec((tm, kdim), row))
        else:
            act_specs.append(pl.BlockSpec((1, d, tm // d, kdim), lambda i: (i // tiles, 0, i % tiles, 0)))
    scratch = [pltpu.VMEM((kdim // LANES, tm, LANES), F32) for d in dilations if d > 1 for _ in range(2)]
    return pl.pallas_call(
        kern,
        grid=(n // tm,),
        in_specs=act_specs + [
            pl.BlockSpec((kdim, d_model), fixed),
            pl.BlockSpec((tm, d_model), row),
            pl.BlockSpec((1, d_model), fixed),
            pl.BlockSpec((1, d_model), fixed),
        ],
        out_specs=pl.BlockSpec((tm, d_model), row),
        out_shape=jax.ShapeDtypeStruct((n, d_model), F32),
        scratch_shapes=scratch,
        compiler_params=_params("parallel"),
        name="outproj_ln",
    )(*acts, w, x2, g, b)


def _matmul_kernel(x_ref, w_ref, o_ref):
    o_ref[...] = _dot(x_ref[...].astype(BF16), w_ref[...]).astype(o_ref.dtype)


def _matmul(x2, w, out_dtype, tm):
    n, kdim = x2.shape
    cols = w.shape[1]
    return pl.pallas_call(
        _matmul_kernel,
        grid=(n // tm,),
        in_specs=[pl.BlockSpec((tm, kdim), lambda i: (i, 0)), pl.BlockSpec((kdim, cols), lambda i: (0, 0))],
        out_specs=pl.BlockSpec((tm, cols), lambda i: (i, 0)),
        out_shape=jax.ShapeDtypeStruct((n, cols), out_dtype),
        compiler_params=_params("parallel"),
        name="matmul",
    )(x2, w)


def _memattn_kernel(*refs, with_outproj):
    if with_outproj:
        o_ref, wmix_ref, gmix_ref, bmix_ref = refs[:4]
        refs = refs[4:]
    x_ref, wq_ref, kv_ref, wo_ref, g_ref, b_ref, rw_ref, rb_ref, y_ref, cls_ref, gate_ref = refs
    x = x_ref[...]
    if with_outproj:
        x = _layer_norm(DEEPNORM_ALPHA * x + _dot(o_ref[...], wmix_ref[...]), gmix_ref[...], bmix_ref[...])
    q = (_dot(x.astype(BF16), wq_ref[...]) * (MEM_HEAD_DIM ** -0.5)).astype(BF16)
    outs = []
    for h in range(MEM_HEADS):
        cs = slice(h * MEM_HEAD_DIM, (h + 1) * MEM_HEAD_DIM)
        kh = kv_ref[0, :, cs]
        vh = kv_ref[0, :, D_MODEL + h * MEM_HEAD_DIM:D_MODEL + (h + 1) * MEM_HEAD_DIM]
        s = _dot_nt(q[:, cs], kh)
        m = jnp.max(s, axis=-1, keepdims=True)
        p = jnp.exp(s - m)
        denom = jnp.sum(p, axis=-1, keepdims=True)
        outs.append(_dot(p.astype(BF16), vh) / denom)
    o = jnp.concatenate(outs, axis=-1).astype(BF16)
    h_out = _dot(o, wo_ref[...])
    y = _layer_norm(DEEPNORM_ALPHA * x + h_out, g_ref[...], b_ref[...])
    y_ref[...] = y
    cls, gates = _route(y, rw_ref, rb_ref)
    cls_ref[0] = cls
    gate_ref[0] = gates


def _memory_attention(x2, kv, wq, wo, g, b, rw_cat, rb_t, *, seq, mixer=None):
    d = x2.shape[1]
    batch, n_mem = kv.shape[0], kv.shape[1]
    tm = TOKEN_TILE
    tiles = seq // tm
    n = batch * seq
    fixed = lambda bi, i: (0, 0)
    row = lambda bi, i: (bi * tiles + i, 0)
    mixer_args, mixer_specs = [], []
    if mixer is not None:
        o, w_mix, g_mix, b_mix = mixer
        mixer_args = [o, w_mix, g_mix, b_mix]
        mixer_specs = [pl.BlockSpec((tm, o.shape[1]), row), pl.BlockSpec(w_mix.shape, fixed),
                       pl.BlockSpec((1, d), fixed), pl.BlockSpec((1, d), fixed)]
    y, cls, gates = pl.pallas_call(
        functools.partial(_memattn_kernel, with_outproj=mixer is not None),
        grid=(batch, tiles),
        in_specs=mixer_specs + [
            pl.BlockSpec((tm, d), row),
            pl.BlockSpec((d, d), fixed),
            pl.BlockSpec((1, n_mem, 2 * d), lambda bi, i: (bi, 0, 0)),
            pl.BlockSpec((d, d), fixed),
            pl.BlockSpec((1, d), fixed),
            pl.BlockSpec((1, d), fixed),
            pl.BlockSpec((d, 2 * LANES), fixed),
            pl.BlockSpec((N_EXPERTS, tm), fixed),
        ],
        out_specs=[
            pl.BlockSpec((tm, d), lambda bi, i: (bi * tiles + i, 0)),
            pl.BlockSpec((1, 1, tm), lambda bi, i: (bi * tiles + i, 0, 0)),
            pl.BlockSpec((1, 2, tm), lambda bi, i: (bi * tiles + i, 0, 0)),
        ],
        out_shape=[
            jax.ShapeDtypeStruct((n, d), F32),
            jax.ShapeDtypeStruct((n // tm, 1, tm), jnp.int32),
            jax.ShapeDtypeStruct((n // tm, 2, tm), F32),
        ],
        compiler_params=_params("parallel", "parallel"),
        name="memory_attention",
    )(*mixer_args, x2, wq, kv, wo, g, b, rw_cat, rb_t)
    return y, cls.reshape(n), gates[:, 0, :].reshape(n), gates[:, 1, :].reshape(n)


def _route(x, w_ref, bias_ref):
    xh = x.astype(BF16)
    xl = (x - xh.astype(F32)).astype(BF16)
    both = _dot(xh, w_ref[...])
    logits = both[:, :LANES] + both[:, LANES:] + _dot(xl, w_ref[:, :LANES])
    lt = logits.T[:N_EXPERTS]
    scores = 1.0 / (1.0 + jnp.exp(-lt))
    biased = scores + bias_ref[...]
    ng = N_EXPERT_GROUPS
    a = [biased[l * ng:(l + 1) * ng] for l in range(EXPERTS_PER_GROUP)]
    sc = [scores[l * ng:(l + 1) * ng] for l in range(EXPERTS_PER_GROUP)]
    gs = None
    for i in range(EXPERTS_PER_GROUP):
        for k in range(i + 1, EXPERTS_PER_GROUP):
            pair = a[i] + a[k]
            gs = pair if gs is None else jnp.maximum(gs, pair)
    gidx = lax.broadcasted_iota(jnp.int32, gs.shape, 0).astype(F32)
    gmax = jnp.max(gs, axis=0, keepdims=True)
    grp_f = jnp.min(jnp.where(gs == gmax, gidx, float(ng)), axis=0, keepdims=True)
    sel = gidx == grp_f
    grp = grp_f.astype(jnp.int32)
    v = [jnp.sum(jnp.where(sel, a[l], 0.0), axis=0, keepdims=True) for l in range(EXPERTS_PER_GROUP)]
    sv = [jnp.sum(jnp.where(sel, sc[l], 0.0), axis=0, keepdims=True) for l in range(EXPERTS_PER_GROUP)]

    def first_argmax(vals, excluded):
        best = None
        for l, val in enumerate(vals):
            cand = val if excluded is None else jnp.where(excluded == l, NEG_INF, val)
            best = cand if best is None else jnp.maximum(best, cand)
        idx = jnp.full(best.shape, EXPERTS_PER_GROUP - 1, jnp.int32)
        for l in range(EXPERTS_PER_GROUP - 1, -1, -1):
            hit = vals[l] == best
            if excluded is not None:
                hit = hit & (excluded != l)
            idx = jnp.where(hit, l, idx)
        return idx

    i1 = first_argmax(v, None)
    i2 = first_argmax(v, i1)

    def pick(vals, idx):
        out = vals[0]
        for l in range(1, EXPERTS_PER_GROUP):
            out = jnp.where(idx == l, vals[l], out)
        return out

    s1 = pick(sv, i1)
    s2 = pick(sv, i2)
    tot = s1 + s2
    lo = jnp.minimum(i1, i2)
    hi = jnp.maximum(i1, i2)
    first_is_lo = i1 < i2
    g_lo = jnp.where(first_is_lo, s1, s2) / tot
    g_hi = jnp.where(first_is_lo, s2, s1) / tot
    pair_idx = jnp.right_shift(lo * (7 - lo), 1) + (hi - lo - 1)
    return grp * N_PAIRS + pair_idx, jnp.concatenate([g_lo, g_hi], axis=0)


def _moe_kernel(src_ref, dst_ref, lo_ref, hi_ref, nblk_ref,
                x_hbm, gate_ref, gu_lo_ref, gu_hi_ref, dn_lo_ref, dn_hi_ref, g_ref, b_ref,
                y_hbm, xbuf, ybuf, gsem, ssem, *, n_tokens):
    blk = pl.program_id(0)
    n_used = nblk_ref[0]
    rows = MOE_BLOCK

    def gather_copy(block, i, slot):
        row = src_ref[block * rows + i]
        return pltpu.make_async_copy(x_hbm.at[pl.ds(row, 1)], xbuf.at[slot, pl.ds(i, 1)], gsem.at[slot])

    def scatter_copy(block, i, slot):
        row = dst_ref[block * rows + i]
        return pltpu.make_async_copy(ybuf.at[slot, pl.ds(i, 1)], y_hbm.at[pl.ds(row, 1)], ssem.at[slot])

    def start_all(copy, block, slot):
        for i in range(rows):
            copy(block, i, slot).start()

    def wait_all(buf, sem, slot):
        pltpu.make_async_copy(buf.at[slot], buf.at[slot], sem.at[slot]).wait()

    @pl.when(blk < n_used)
    def _():
        slot = blk % 2
        nxt = jnp.minimum(blk + 1, n_used - 1)

        @pl.when(blk == 0)
        def _():
            start_all(gather_copy, 0, 0)
            ybuf[1] = jnp.zeros(ybuf.shape[1:], ybuf.dtype)
            for part in range(2):
                spare = pltpu.make_async_copy(ybuf.at[1], y_hbm.at[pl.ds(n_tokens + part * rows, rows)], ssem.at[1])
                spare.start()
                spare.wait()

        @pl.when(blk >= 2)
        def _():
            wait_all(ybuf, ssem, slot)

        wait_all(xbuf, gsem, slot)
        start_all(gather_copy, nxt, 1 - slot)

        x = xbuf[slot]
        xb = x.astype(BF16)

        def expert(gu_ref, dn_ref):
            gu = _dot(xb, gu_ref[0])
            gate_act = gu[:, :D_EXPERT]
            act = gate_act / (1.0 + jnp.exp(-gate_act)) * gu[:, D_EXPERT:]
            return _dot(act.astype(BF16), dn_ref[0])

        r_i = lax.broadcasted_iota(jnp.int32, (rows, rows), 0)
        c_i = lax.broadcasted_iota(jnp.int32, (rows, rows), 1)
        diag = r_i == c_i
        gates = gate_ref[0]
        g_lo = jnp.sum(jnp.where(diag, gates[0:1, :], 0.0), axis=1, keepdims=True)
        g_hi = jnp.sum(jnp.where(diag, gates[1:2, :], 0.0), axis=1, keepdims=True)
        h = g_lo * expert(gu_lo_ref, dn_lo_ref) + g_hi * expert(gu_hi_ref, dn_hi_ref)
        ybuf[slot] = _layer_norm(DEEPNORM_ALPHA * x + h, g_ref[...], b_ref[...])
        start_all(scatter_copy, blk, slot)

        @pl.when(blk == n_used - 1)
        def _():
            wait_all(xbuf, gsem, 1 - slot)

            @pl.when(blk >= 1)
            def _():
                wait_all(ybuf, ssem, 1 - slot)
            wait_all(ybuf, ssem, slot)


def _moe_experts(x2, slot_src, slot_dst, blk_lo, blk_hi, n_used, slot_gates, w_gu, w_dn, g, b, *, n):
    d = x2.shape[1]
    n_blocks = blk_lo.shape[0]
    rows = MOE_BLOCK
    grid_spec = pltpu.PrefetchScalarGridSpec(
        num_scalar_prefetch=5,
        grid=(n_blocks,),
        in_specs=[
            pl.BlockSpec(memory_space=pl.ANY),
            pl.BlockSpec((1, 2, rows), lambda i, *_: (i, 0, 0)),
            pl.BlockSpec((1, d, 2 * D_EXPERT), lambda i, src, dst, lo, hi, nb: (lo[i], 0, 0)),
            pl.BlockSpec((1, d, 2 * D_EXPERT), lambda i, src, dst, lo, hi, nb: (hi[i], 0, 0)),
            pl.BlockSpec((1, D_EXPERT, d), lambda i, src, dst, lo, hi, nb: (lo[i], 0, 0)),
            pl.BlockSpec((1, D_EXPERT, d), lambda i, src, dst, lo, hi, nb: (hi[i], 0, 0)),
            pl.BlockSpec((1, d), lambda i, *_: (0, 0)),
            pl.BlockSpec((1, d), lambda i, *_: (0, 0)),
        ],
        out_specs=pl.BlockSpec(memory_space=pl.ANY),
        scratch_shapes=[
            pltpu.VMEM((2, rows, d), F32),
            pltpu.VMEM((2, rows, d), F32),
            pltpu.SemaphoreType.DMA((2,)),
            pltpu.SemaphoreType.DMA((2,)),
        ],
    )
    return pl.pallas_call(
        functools.partial(_moe_kernel, n_tokens=n),
        grid_spec=grid_spec,
        out_shape=jax.ShapeDtypeStruct((n + 2 * rows, d), F32),
        compiler_params=_params("arbitrary"),
        name="moe_experts",
    )(slot_src, slot_dst, blk_lo, blk_hi, n_used, x2, slot_gates, w_gu, w_gu, w_dn, w_dn, g, b)


_PAIR_LO = np.array([0, 0, 0, 1, 1, 2], np.int32)
_PAIR_HI = np.array([1, 2, 3, 2, 3, 3], np.int32)


def _dispatch_tables(cls, g_lo, g_hi):
    n = cls.shape[0]
    rows = MOE_BLOCK
    n_blocks = n // rows + N_CLASSES
    i32 = jnp.int32
    cls_ids = jnp.arange(N_CLASSES, dtype=i32)
    _, order, glo_s, ghi_s = lax.sort((cls, jnp.arange(n, dtype=i32), g_lo, g_hi), num_keys=1)
    counts = jnp.sum((cls[:, None] == cls_ids[None, :]).astype(i32), axis=0)
    cls_blocks = (counts + rows - 1) // rows
    blk_end = jnp.cumsum(cls_blocks)
    blk_begin = blk_end - cls_blocks
    start = jnp.cumsum(counts) - counts
    n_used = blk_end[-1]
    blk = jnp.arange(n_blocks, dtype=i32)
    blk_eff = jnp.minimum(blk, n_used - 1)
    blk_cls = jnp.sum((blk_end[None, :] <= blk_eff[:, None]).astype(i32), axis=1)
    pick = blk_cls[:, None] == cls_ids[None, :]
    table = lambda t: jnp.sum(jnp.where(pick, t[None, :], 0), axis=1)
    blk_rank = (blk_eff - table(blk_begin)) * rows
    blk_nvalid = jnp.where(blk < n_used, jnp.clip(table(counts) - blk_rank, 0, rows), 0)
    row = jnp.arange(rows, dtype=i32)
    valid = row[None, :] < blk_nvalid[:, None]
    src = jnp.clip((table(start) + blk_rank)[:, None] + row[None, :], 0, n - 1)
    tok = order[src]
    spare = n + (blk % 2)[:, None] * rows + row[None, :]
    slot_src = jnp.where(valid, tok, 0).reshape(n_blocks * rows)
    slot_dst = jnp.where(valid, tok, spare).reshape(n_blocks * rows)
    slot_gates = jnp.stack([jnp.where(valid, glo_s[src], 0.0), jnp.where(valid, ghi_s[src], 0.0)], axis=1)
    grp = blk_cls // N_PAIRS
    pair = blk_cls % N_PAIRS
    pair_ids = jnp.arange(N_PAIRS, dtype=i32)
    pair_pick = pair[:, None] == pair_ids[None, :]
    blk_lo = grp * EXPERTS_PER_GROUP + jnp.sum(jnp.where(pair_pick, jnp.asarray(_PAIR_LO)[None, :], 0), axis=1)
    blk_hi = grp * EXPERTS_PER_GROUP + jnp.sum(jnp.where(pair_pick, jnp.asarray(_PAIR_HI)[None, :], 0), axis=1)
    return (slot_src.astype(i32), slot_dst.astype(i32), blk_lo.astype(i32), blk_hi.astype(i32),
            n_used.reshape(1).astype(i32), slot_gates)


def _tile_heads(t64):
    return jnp.concatenate([t64] * (LANES // HEAD_DIM), axis=-1)


def _rope_tables(seq):
    t = jnp.arange(seq, dtype=jnp.int32)

    def tables(pos, dim):
        inv_freq = ROPE_THETA ** (-jnp.arange(0, dim, 2, dtype=F32) / dim)
        ang = pos.astype(F32)[:, None] * inv_freq[None, :]
        return jnp.cos(ang), jnp.sin(ang)

    c1, s1 = tables(t, HEAD_DIM)
    cos_1d = _tile_heads(jnp.concatenate([c1, c1], axis=-1))
    sin_1d = _tile_heads(jnp.concatenate([-s1, s1], axis=-1))
    cr, sr = tables(t // GRID_W, HEAD_DIM // 2)
    cc, sc = tables(t % GRID_W, HEAD_DIM // 2)
    cos_ax = _tile_heads(jnp.concatenate([cr, cr, cc, cc], axis=-1))
    sin_ax = _tile_heads(jnp.concatenate([-sr, sr, -sc, sc], axis=-1))
    return (cos_1d, sin_1d), (cos_ax, sin_ax)


def _block_diag_ones():
    i = np.arange(2 * LANES)
    return jnp.asarray((i[:, None] // HEAD_DIM) == (i[None, :] // HEAD_DIM), BF16)


def kernel(x_prompt, x_sample, mem_prompt, mem_sample, a_w_in, a_sink, a_w_out, b_w_in, b_q_norm, b_k_norm,
           b_w_out, c_w_in, c_w_out, m_w_q, m_w_kv, m_w_out, ln_g, ln_b, router_w, router_bias, e_w_gu, e_w_down):
    seq = x_prompt.shape[1]
    assert x_sample.shape[1] == seq
    x = jnp.concatenate([x_prompt, x_sample], axis=0)
    mem = jnp.concatenate([mem_prompt, mem_sample], axis=0)
    batch = x.shape[0]
    n = batch * seq
    n_mem = mem.shape[1]
    x2 = x.reshape(n, D_MODEL)

    rope_1d, rope_ax = _rope_tables(seq)
    ones_bd = _block_diag_ones()
    unit_gains = jnp.ones((2, LANES), F32)
    scale = HEAD_DIM ** -0.5

    rw = router_w.astype(F32).reshape(D_MODEL, N_EXPERT_GROUPS, EXPERTS_PER_GROUP).transpose(0, 2, 1)
    rw = jnp.pad(rw.reshape(D_MODEL, N_EXPERTS), ((0, 0), (0, LANES - N_EXPERTS)))
    rw_hi = rw.astype(BF16)
    rw_cat = jnp.concatenate([rw_hi, (rw - rw_hi.astype(F32)).astype(BF16)], axis=1)
    rb = router_bias.astype(F32).reshape(N_EXPERT_GROUPS, EXPERTS_PER_GROUP).T.reshape(N_EXPERTS, 1)
    rb_t = jnp.broadcast_to(rb, (N_EXPERTS, TOKEN_TILE))

    cq, ck = C_HEADS * HEAD_DIM, C_KV_HEADS * HEAD_DIM

    for i in range(DEPTH):
        kind, j = i % N_MIXERS, i // N_MIXERS
        lg = lambda s_: ln_g[i, s_].astype(F32).reshape(1, D_MODEL)
        lb = lambda s_: ln_b[i, s_].astype(F32).reshape(1, D_MODEL)
        if kind == 0:
            nq, nk = A_HEADS * HEAD_DIM, A_KV_HEADS * HEAD_DIM
            q, k, v = _qkv_project(x2, a_w_in[j].astype(BF16), rope_1d[0], rope_1d[1], unit_gains, ones_bd, n=n,
                                   nq=nq, nk=nk, nv=nk, half=HEAD_DIM // 2, qk_norm=False,
                                   q_scale=scale * LOG2_E, seq=seq)
            o, = _banded_attention(q, k, v, a_sink[j].astype(F32), batch=batch, seq=seq, n_q_heads=A_HEADS,
                                   n_kv_heads=A_KV_HEADS, window=A_WINDOW, tq=BANDED_QUERY_BLOCK, with_lse=False)
            mixer = (o, a_w_out[j].astype(BF16), lg(0), lb(0))
        elif kind == 1:
            nq, nk = B_HEADS * HEAD_DIM, B_KV_HEADS * HEAD_DIM
            gains = jnp.stack([_tile_heads(b_q_norm[j].astype(F32)), _tile_heads(b_k_norm[j].astype(F32))], axis=0)
            q, k, v = _qkv_project(x2, b_w_in[j].astype(BF16), rope_ax[0], rope_ax[1], gains, ones_bd, n=n,
                                   nq=nq, nk=nk, nv=nk, half=HEAD_DIM // 4, qk_norm=True,
                                   q_scale=scale * LOG2_E, seq=seq, v_transposed=True)
            o = _flash_attention(q, k, v, batch=batch, seq=seq, n_q_heads=B_HEADS, n_kv_heads=B_KV_HEADS,
                                 tq=256, tk=1024)
            mixer = (o, b_w_out[j].astype(BF16), lg(0), lb(0))
        else:
            dilations = [dil for _, dil in C_PATTERNS]
            qkv = _qkv_project_dilated(x2, c_w_in[j].astype(BF16), rope_1d[0], rope_1d[1], n=n, seq=seq,
                                       dilations=dilations, cq=cq, ck=ck, q_scale=scale * LOG2_E)
            no_sink = jnp.full((C_HEADS,), NEG_INF, F32)
            os_, lses = [], []
            for (window, dil), (q, k, v) in zip(C_PATTERNS, qkv):
                half_w = (window // 2) // dil
                o, lse = _banded_attention(q.reshape(n, cq), k.reshape(n, ck), v.reshape(n, ck), no_sink,
                                           batch=batch * dil, seq=seq // dil, n_q_heads=C_HEADS,
                                           n_kv_heads=C_KV_HEADS, window=half_w, tq=BANDED_QUERY_BLOCK,
                                           with_lse=True)
                shape = (n, cq) if dil == 1 else (batch, dil, seq // dil, cq)
                os_.append(o.reshape(shape))
                lses.append(lse.reshape(shape))
            x2 = _merge_outproj_ln(os_, lses, c_w_out[j].astype(BF16), x2, lg(0), lb(0), n=n, seq=seq,
                                   dilations=dilations)
            mixer = None

        kv = _matmul(mem.reshape(batch * n_mem, D_MODEL), m_w_kv[i].astype(BF16), BF16, tm=n_mem)
        x2, cls, g_lo, g_hi = _memory_attention(x2, kv.reshape(batch, n_mem, 2 * D_MODEL), m_w_q[i].astype(BF16),
                                                m_w_out[i].astype(BF16), lg(1), lb(1), rw_cat, rb_t, seq=seq,
                                                mixer=mixer)
        slot_src, slot_dst, blk_lo, blk_hi, n_used, slot_gates = _dispatch_tables(cls, g_lo, g_hi)
        x2 = _moe_experts(x2, slot_src, slot_dst, blk_lo, blk_hi, n_used, slot_gates,
                          e_w_gu[i].astype(BF16), e_w_down[i].astype(BF16), lg(2), lb(2), n=n)

    n_prompt = x_prompt.shape[0] * seq
    return (x2[:n_prompt].reshape(x_prompt.shape), x2[n_prompt:n].reshape(x_sample.shape))
```

```python
import functools

import jax
import jax.numpy as jnp
import numpy as np
from jax import lax
from jax.experimental import pallas as pl
from jax.experimental.pallas import tpu as pltpu

F32 = jnp.float32
BF16 = jnp.bfloat16

D_MODEL = 1024
DEPTH = 4
N_MIXERS = 3
HEAD_DIM = 64
ROPE_THETA = 10000.0
GRID_W = 64

A_HEADS, A_KV_HEADS, A_WINDOW = 16, 4, 128
B_HEADS, B_KV_HEADS = 16, 4
C_PATTERNS = ((128, 1), (512, 4), (2048, 16))
C_HEADS, C_KV_HEADS = 8, 2

MEM_HEADS = 4
MEM_HEAD_DIM = D_MODEL // MEM_HEADS

N_EXPERTS = 32
N_EXPERT_GROUPS = 8
EXPERTS_PER_GROUP = 4
D_EXPERT = D_MODEL // 4
N_PAIRS = 6
N_CLASSES = N_EXPERT_GROUPS * N_PAIRS
MOE_BLOCK = 128
GATHER_BUFFERS = 3

LN_EPS = 1e-5
QK_NORM_EPS = 1e-6
DEEPNORM_ALPHA = (2 * DEPTH) ** 0.25

LANES = 128
TOKEN_TILE = 512
BANDED_QUERY_BLOCK = 512
BANDED_SUB_BLOCK = 128
FLASH_CHUNKS_PER_BODY = 4
VMEM_LIMIT_BYTES = 48 * 1024 * 1024

NEG_INF = float("-inf")
LOG2_E = 1.4426950408889634
LN_2 = 0.6931471805599453


def _params(*semantics):
    return pltpu.CompilerParams(dimension_semantics=semantics, vmem_limit_bytes=VMEM_LIMIT_BYTES)


def _layer_norm(z, g, b):
    mu = jnp.mean(z, axis=-1, keepdims=True)
    zc = z - mu
    var = jnp.mean(zc * zc, axis=-1, keepdims=True)
    return zc * lax.rsqrt(var + LN_EPS) * g + b


def _dot(a, b):
    return jnp.dot(a, b, preferred_element_type=F32)


def _dot_nt(a, b):
    return lax.dot_general(a, b, (((1,), (1,)), ((), ())), preferred_element_type=F32)


def _qkv_kernel(x_ref, w_ref, cos_ref, sin_ref, gain_ref, ones_ref, q_ref, k_ref, v_ref, *,
                nq, nk, half, qk_norm, q_scale, v_transposed):
    acc = _dot(x_ref[...].astype(BF16), w_ref[...])
    cos = cos_ref[...]
    sin = sin_ref[...]
    lane = lax.broadcasted_iota(jnp.int32, cos.shape, 1)
    first = (lane % (2 * half)) < half

    mean_sq = {}
    if qk_norm:
        width = ones_ref.shape[0]
        for c2 in range((nq + nk) // width):
            a2 = jnp.square(acc[:, c2 * width:(c2 + 1) * width])
            hi = a2.astype(BF16)
            lo = (a2 - hi.astype(F32)).astype(BF16)
            ms = (_dot(hi, ones_ref[...]) + _dot(lo, ones_ref[...])) * (1.0 / HEAD_DIM)
            for t in range(width // LANES):
                mean_sq[c2 * (width // LANES) + t] = ms[:, t * LANES:(t + 1) * LANES]

    def rotate(c, gain, scale):
        a = acc[:, c * LANES:(c + 1) * LANES]
        if qk_norm:
            a = a * lax.rsqrt(mean_sq[c] + QK_NORM_EPS) * gain
        partner = jnp.where(first, pltpu.roll(a, LANES - half, 1), pltpu.roll(a, half, 1))
        r = a * cos + partner * sin
        if scale != 1.0:
            r = r * scale
        return r.astype(BF16)

    gq = gain_ref[0:1, :]
    gk = gain_ref[1:2, :]
    for c in range(nq // LANES):
        q_ref[:, c * LANES:(c + 1) * LANES] = rotate(c, gq, q_scale)
    for c in range(nk // LANES):
        k_ref[:, c * LANES:(c + 1) * LANES] = rotate(nq // LANES + c, gk, 1.0)
    if v_transposed:
        v_ref[0] = acc[:, nq + nk:].T.astype(BF16)
    else:
        v_ref[...] = acc[:, nq + nk:].astype(BF16)


def _qkv_project(x2, w, cos, sin, gains, ones_bd, *, n, nq, nk, nv, half, qk_norm, q_scale, seq,
                 v_transposed=False):
    d = x2.shape[1]
    tm = TOKEN_TILE
    tiles_per_seq = seq // tm
    kern = functools.partial(_qkv_kernel, nq=nq, nk=nk, half=half, qk_norm=qk_norm, q_scale=q_scale,
                             v_transposed=v_transposed)
    if v_transposed:
        v_spec = pl.BlockSpec((1, nv, tm), lambda i: (i, 0, 0))
        v_shape = jax.ShapeDtypeStruct((n // tm, nv, tm), BF16)
    else:
        v_spec = pl.BlockSpec((tm, nv), lambda i: (i, 0))
        v_shape = jax.ShapeDtypeStruct((n, nv), BF16)
    return pl.pallas_call(
        kern,
        grid=(n // tm,),
        in_specs=[
            pl.BlockSpec((tm, d), lambda i: (i, 0)),
            pl.BlockSpec((d, nq + nk + nv), lambda i: (0, 0)),
            pl.BlockSpec((tm, LANES), lambda i: (i % tiles_per_seq, 0)),
            pl.BlockSpec((tm, LANES), lambda i: (i % tiles_per_seq, 0)),
            pl.BlockSpec((2, LANES), lambda i: (0, 0)),
            pl.BlockSpec(ones_bd.shape, lambda i: (0, 0)),
        ],
        out_specs=[
            pl.BlockSpec((tm, nq), lambda i: (i, 0)),
            pl.BlockSpec((tm, nk), lambda i: (i, 0)),
            v_spec,
        ],
        out_shape=[
            jax.ShapeDtypeStruct((n, nq), BF16),
            jax.ShapeDtypeStruct((n, nk), BF16),
            v_shape,
        ],
        compiler_params=_params("parallel"),
        name="qkv_project",
    )(x2, w, cos, sin, gains, ones_bd)


def _qkv_dilated_kernel(x_ref, w_ref, cos_ref, sin_ref, *refs, dilations, cq, ck, q_scale):
    n_g = len(dilations)
    out_refs, acc_ref = refs[:3 * n_g], refs[3 * n_g]
    acc = _dot(x_ref[...].astype(BF16), w_ref[...])
    for c in range(acc_ref.shape[0]):
        acc_ref[c] = acc[:, c * LANES:(c + 1) * LANES]
    tm = acc_ref.shape[1]
    half = HEAD_DIM // 2
    gw = cq + 2 * ck

    for p, d in enumerate(dilations):
        q_ref, k_ref, v_ref = out_refs[3 * p:3 * p + 3]
        lane = lax.broadcasted_iota(jnp.int32, (tm // d, LANES), 1)
        first = (lane % (2 * half)) < half
        for r in range(d):
            rows = slice(None) if d == 1 else pl.ds(r, tm // d, stride=d)
            cos = cos_ref[rows, :]
            sin = sin_ref[rows, :]

            def put(ref, c, val):
                if d == 1:
                    ref[:, c * LANES:(c + 1) * LANES] = val.astype(BF16)
                else:
                    ref[0, r, :, c * LANES:(c + 1) * LANES] = val.astype(BF16)

            def rotate(a, scale):
                partner = jnp.where(first, pltpu.roll(a, LANES - half, 1), pltpu.roll(a, half, 1))
                return (a * cos + partner * sin) * scale

            for c in range(cq // LANES):
                put(q_ref, c, rotate(acc_ref[(p * gw) // LANES + c, rows, :], q_scale))
            for c in range(ck // LANES):
                put(k_ref, c, rotate(acc_ref[(p * gw + cq) // LANES + c, rows, :], 1.0))
                put(v_ref, c, acc_ref[(p * gw + cq + ck) // LANES + c, rows, :])


def _qkv_project_dilated(x2, w, cos, sin, *, n, seq, dilations, cq, ck, q_scale):
    d_model = x2.shape[1]
    tm = TOKEN_TILE
    tiles = seq // tm
    out_specs, out_shape = [], []
    for d in dilations:
        for cols in (cq, ck, ck):
            if d == 1:
                out_specs.append(pl.BlockSpec((tm, cols), lambda i: (i, 0)))
                out_shape.append(jax.ShapeDtypeStruct((n, cols), BF16))
            else:
                out_specs.append(pl.BlockSpec((1, d, tm // d, cols), lambda i: (i // tiles, 0, i % tiles, 0)))
                out_shape.append(jax.ShapeDtypeStruct((n // seq, d, seq // d, cols), BF16))
    kern = functools.partial(_qkv_dilated_kernel, dilations=tuple(dilations), cq=cq, ck=ck, q_scale=q_scale)
    res = pl.pallas_call(
        kern,
        grid=(n // tm,),
        in_specs=[
            pl.BlockSpec((tm, d_model), lambda i: (i, 0)),
            pl.BlockSpec(w.shape, lambda i: (0, 0)),
            pl.BlockSpec((tm, LANES), lambda i: (i % tiles, 0)),
            pl.BlockSpec((tm, LANES), lambda i: (i % tiles, 0)),
        ],
        out_specs=out_specs,
        out_shape=out_shape,
        scratch_shapes=[pltpu.VMEM((w.shape[1] // LANES, tm, LANES), F32)],
        compiler_params=_params("parallel"),
        name="qkv_project_dilated",
    )(x2, w, cos, sin)
    return [tuple(res[3 * p:3 * p + 3]) for p in range(len(dilations))]


def _banded_kernel(sink_ref, band_ref, q_ref, kp_ref, kc_ref, kn_ref, vp_ref, vc_ref, vn_ref, *out_refs,
                   tq, sub, w, n_q_heads, n_kv_heads, with_lse):
    o_ref = out_refs[0]
    j = pl.program_id(1)
    g_per = n_q_heads // n_kv_heads
    n_sub = tq // sub
    span = tq + 2 * w
    sub_span = sub + 2 * w
    m_lanes = g_per * sub
    key = lax.broadcasted_iota(jnp.int32, (span, 1), 0)
    off_seq = ((key < w) & (j == 0)) | ((key >= w + tq) & (j == pl.num_programs(1) - 1))
    edge = jnp.where(off_seq, NEG_INF, 0.0)
    biases = [band_ref[...] + edge[sb * sub:sb * sub + sub_span] for sb in range(n_sub)]
    lane_row = lax.broadcasted_iota(jnp.int32, (1, m_lanes), 1)
    half_rows = lax.broadcasted_iota(jnp.int32, (2 * HEAD_DIM, m_lanes), 0) >= HEAD_DIM
    ones = jnp.ones((HEAD_DIM, sub_span), BF16)

    vcat = jnp.concatenate([vp_ref[0], vc_ref[0], vn_ref[0]], axis=0).astype(F32)
    vt_all = jnp.concatenate([vcat[:, c * LANES:(c + 1) * LANES].T for c in range(vcat.shape[1] // LANES)],
                             axis=0).astype(BF16)
    kcats = [jnp.concatenate([kp_ref[0, :, kh * HEAD_DIM:(kh + 1) * HEAD_DIM],
                              kc_ref[0, :, kh * HEAD_DIM:(kh + 1) * HEAD_DIM],
                              kn_ref[0, :, kh * HEAD_DIM:(kh + 1) * HEAD_DIM]], axis=0)
             for kh in range(n_kv_heads)]

    def scores(kh, sb):
        r0 = sb * sub
        qs = jnp.concatenate(
            [q_ref[0, r0:r0 + sub, (kh * g_per + g) * HEAD_DIM:(kh * g_per + g + 1) * HEAD_DIM]
             for g in range(g_per)], axis=0)
        return _dot_nt(kcats[kh][r0:r0 + sub_span], qs) + biases[sb]

    tasks = [(kh, sb) for kh in range(n_kv_heads) for sb in range(n_sub)]
    o_parts = [[None] * n_sub for _ in range(n_q_heads)]
    l_parts = [[None] * n_sub for _ in range(n_q_heads)]
    st_next = scores(*tasks[0])
    for t, (kh, sb) in enumerate(tasks):
        r0 = sb * sub
        st = st_next
        if t + 1 < len(tasks):
            st_next = scores(*tasks[t + 1])
        sink = jnp.full((1, m_lanes), sink_ref[kh * g_per], F32)
        for g in range(1, g_per):
            sink = jnp.where(lane_row >= g * sub, sink_ref[kh * g_per + g], sink)
        sink = sink * LOG2_E
        m = jnp.maximum(jnp.max(st, axis=0, keepdims=True), sink)
        p = jnp.exp2(st - m).astype(BF16)
        vext = jnp.concatenate([vt_all[kh * HEAD_DIM:(kh + 1) * HEAD_DIM, r0:r0 + sub_span], ones], axis=0)
        acc = _dot(vext, p)
        acc = jnp.where(half_rows, acc + jnp.exp2(sink - m), acc)
        num, den = acc[:HEAD_DIM], acc[HEAD_DIM:]
        low = jnp.log(den) + m * LN_2 if with_lse else den
        res = jnp.concatenate([num / den, low], axis=0).T
        for g in range(g_per):
            o_parts[kh * g_per + g][sb] = res[g * sub:(g + 1) * sub, :HEAD_DIM]
            l_parts[kh * g_per + g][sb] = res[g * sub:(g + 1) * sub, HEAD_DIM:]
    o_ref[0] = jnp.concatenate([jnp.concatenate(parts, axis=0) for parts in o_parts], axis=-1).astype(o_ref.dtype)
    if with_lse:
        out_refs[1][0] = jnp.concatenate([jnp.concatenate(parts, axis=0) for parts in l_parts], axis=-1)


def _banded_attention(q, k, v, sink, *, batch, seq, n_q_heads, n_kv_heads, window, tq, with_lse):
    qw = n_q_heads * HEAD_DIM
    kw = n_kv_heads * HEAD_DIM
    w = window
    r_blocks = tq // w
    nb_w = seq // w
    q3 = q.reshape(batch, seq, qw)
    k3 = k.reshape(batch, seq, kw)
    v3 = v.reshape(batch, seq, kw)

    def q_map(b, j):
        return (b, j, 0)

    def prev_map(b, j):
        return (b, jnp.maximum(j * r_blocks - 1, 0), 0)

    def next_map(b, j):
        return (b, jnp.minimum((j + 1) * r_blocks, nb_w - 1), 0)

    g_per = n_q_heads // n_kv_heads
    sub = BANDED_SUB_BLOCK
    rel = np.arange(sub + 2 * w)[:, None] - w - (np.arange(g_per * sub) % sub)[None, :]
    band = jnp.asarray(np.where(np.abs(rel) <= w, 0.0, NEG_INF), F32)

    kern = functools.partial(_banded_kernel, tq=tq, sub=sub, w=w, n_q_heads=n_q_heads,
                             n_kv_heads=n_kv_heads, with_lse=with_lse)
    out_specs = [pl.BlockSpec((1, tq, qw), q_map)]
    out_shape = [jax.ShapeDtypeStruct((batch, seq, qw), BF16)]
    if with_lse:
        out_specs.append(pl.BlockSpec((1, tq, qw), q_map))
        out_shape.append(jax.ShapeDtypeStruct((batch, seq, qw), F32))
    res = pl.pallas_call(
        kern,
        grid=(batch, seq // tq),
        in_specs=[
            pl.BlockSpec(memory_space=pltpu.SMEM),
            pl.BlockSpec(band.shape, lambda b, j: (0, 0)),
            pl.BlockSpec((1, tq, qw), q_map),
            pl.BlockSpec((1, w, kw), prev_map),
            pl.BlockSpec((1, tq, kw), q_map),
            pl.BlockSpec((1, w, kw), next_map),
            pl.BlockSpec((1, w, kw), prev_map),
            pl.BlockSpec((1, tq, kw), q_map),
            pl.BlockSpec((1, w, kw), next_map),
        ],
        out_specs=out_specs,
        out_shape=out_shape,
        compiler_params=_params("parallel", "parallel"),
        name="banded_attention",
    )(sink, band, q3, k3, k3, k3, v3, v3, v3)
    return [r.reshape(batch * seq, qw) for r in res]


def _flash_kernel(q_ref, k_ref, vt_ref, o_ref, st_a, st_b, *, tq, tk, tv, seq, n_kv_heads, g_per):
    ones = jnp.ones((HEAD_DIM, tk), BF16)
    outs = [None] * (n_kv_heads * g_per)
    for kh in range(n_kv_heads):
        cs = slice(kh * HEAD_DIM, (kh + 1) * HEAD_DIM)
        qs = jnp.concatenate(
            [q_ref[0, :, (kh * g_per + g) * HEAD_DIM:(kh * g_per + g + 1) * HEAD_DIM] for g in range(g_per)],
            axis=0)

        def scores(c, st_ref):
            start = pl.multiple_of(c * tk, tk)
            st_ref[...] = _dot_nt(k_ref[0, pl.ds(start, tk), cs], qs)

        def update(c, st_ref, m, acc):
            vt = jnp.concatenate([vt_ref[c * (tk // tv) + t, cs, :] for t in range(tk // tv)], axis=-1)
            vext = jnp.concatenate([vt, ones], axis=0)
            st = st_ref[...]
            m_new = jnp.maximum(m, jnp.max(st, axis=0, keepdims=True))
            alpha = jnp.exp2(m - m_new)
            p = jnp.exp2(st - m_new).astype(BF16)
            return m_new, alpha * acc + _dot(vext, p)

        bufs = (st_a, st_b)

        def group(cg, carry, last):
            m, acc = carry
            for t in range(FLASH_CHUNKS_PER_BODY):
                c = cg * FLASH_CHUNKS_PER_BODY + t
                if not (last and t == FLASH_CHUNKS_PER_BODY - 1):
                    scores(c + 1, bufs[(t + 1) % 2])
                m, acc = update(c, bufs[t % 2], m, acc)
            return m, acc

        n_groups = seq // (FLASH_CHUNKS_PER_BODY * tk)
        m0 = jnp.full((1, g_per * tq), NEG_INF, F32)
        acc0 = jnp.zeros((2 * HEAD_DIM, g_per * tq), F32)
        scores(0, st_a)
        carry = lax.fori_loop(0, n_groups - 1, lambda cg, cr: group(cg, cr, False), (m0, acc0))
        _, acc = group(n_groups - 1, carry, True)
        acc = acc.T
        o = acc[:, :HEAD_DIM] / acc[:, HEAD_DIM:]
        for g in range(g_per):
            outs[kh * g_per + g] = o[g * tq:(g + 1) * tq]
    o_ref[0] = jnp.concatenate(outs, axis=-1).astype(o_ref.dtype)


def _flash_attention(q, k, vt, *, batch, seq, n_q_heads, n_kv_heads, tq, tk):
    qw = n_q_heads * HEAD_DIM
    kw = n_kv_heads * HEAD_DIM
    tv = vt.shape[2]
    chunks = seq // tv
    kern = functools.partial(_flash_kernel, tq=tq, tk=tk, tv=tv, seq=seq, n_kv_heads=n_kv_heads,
                             g_per=n_q_heads // n_kv_heads)
    out = pl.pallas_call(
        kern,
        grid=(batch, seq // tq),
        in_specs=[
            pl.BlockSpec((1, tq, qw), lambda b, j: (b, j, 0)),
            pl.BlockSpec((1, seq, kw), lambda b, j: (b, 0, 0)),
            pl.BlockSpec((chunks, kw, tv), lambda b, j: (b, 0, 0)),
        ],
        out_specs=pl.BlockSpec((1, tq, qw), lambda b, j: (b, j, 0)),
        out_shape=jax.ShapeDtypeStruct((batch, seq, qw), BF16),
        scratch_shapes=[pltpu.VMEM((tk, (n_q_heads // n_kv_heads) * tq), F32) for _ in range(2)],
        compiler_params=_params("parallel", "arbitrary"),
        name="dense_attention",
    )(q.reshape(batch, seq, qw), k.reshape(batch, seq, kw), vt)
    return out.reshape(batch * seq, qw)


def _merge_outproj_kernel(*refs, dilations):
    n_g = len(dilations)
    o_refs, l_refs = refs[:n_g], refs[n_g:2 * n_g]
    w_ref, x_ref, g_ref, b_ref, y_ref = refs[2 * n_g:2 * n_g + 5]
    stage = refs[2 * n_g + 5:]
    tm = y_ref.shape[0]

    def token_order(ref, d, buf):
        if d == 1:
            return ref[...].astype(F32)
        for r in range(d):
            rows = ref[0, r].astype(F32)
            for c in range(buf.shape[0]):
                buf[c, pl.ds(r, tm // d, stride=d), :] = rows[:, c * LANES:(c + 1) * LANES]
        return jnp.concatenate([buf[c] for c in range(buf.shape[0])], axis=-1)

    os_, ls_, used = [], [], 0
    for p, d in enumerate(dilations):
        os_.append(token_order(o_refs[p], d, stage[used] if d > 1 else None))
        ls_.append(token_order(l_refs[p], d, stage[used + 1] if d > 1 else None))
        used += 2 if d > 1 else 0
    m = functools.reduce(jnp.maximum, ls_)
    es = [jnp.exp(l - m) for l in ls_]
    o = sum(e * o_p for e, o_p in zip(es, os_)) / sum(es)
    h = _dot(o.astype(BF16), w_ref[...])
    y_ref[...] = _layer_norm(DEEPNORM_ALPHA * x_ref[...] + h, g_ref[...], b_ref[...])


def _merge_outproj_ln(os_, lses, w, x2, g, b, *, n, seq, dilations):
    d_model = x2.shape[1]
    tm = TOKEN_TILE
    tiles = seq // tm
    kdim = w.shape[0]
    row = lambda i: (i, 0)
    fixed = lambda i: (0, 0)
    acts = list(os_) + list(lses)
    kern = functools.partial(_merge_outproj_kernel, dilations=tuple(dilations))
    act_specs = []
    for d in list(dilations) * 2:
        if d == 1:
            act_specs.append(pl.BlockSpec((tm, kdim), row))
        else:
            act_specs.append(pl.BlockSpec((1, d, tm // d, kdim), lambda i: (i // tiles, 0, i % tiles, 0)))
    scratch = [pltpu.VMEM((kdim // LANES, tm, LANES), F32) for d in dilations if d > 1 for _ in range(2)]
    return pl.pallas_call(
        kern,
        grid=(n // tm,),
        in_specs=act_specs + [
            pl.BlockSpec((kdim, d_model), fixed),
            pl.BlockSpec((tm, d_model), row),
            pl.BlockSpec((1, d_model), fixed),
            pl.BlockSpec((1, d_model), fixed),
        ],
        out_specs=pl.BlockSpec((tm, d_model), row),
        out_shape=jax.ShapeDtypeStruct((n, d_model), F32),
        scratch_shapes=scratch,
        compiler_params=_params("parallel"),
        name="outproj_ln",
    )(*acts, w, x2, g, b)


def _matmul_kernel(x_ref, w_ref, o_ref):
    o_ref[...] = _dot(x_ref[...].astype(BF16), w_ref[...]).astype(o_ref.dtype)


def _matmul(x2, w, out_dtype, tm):
    n, kdim = x2.shape
    cols = w.shape[1]
    return pl.pallas_call(
        _matmul_kernel,
        grid=(n // tm,),
        in_specs=[pl.BlockSpec((tm, kdim), lambda i: (i, 0)), pl.BlockSpec((kdim, cols), lambda i: (0, 0))],
        out_specs=pl.BlockSpec((tm, cols), lambda i: (i, 0)),
        out_shape=jax.ShapeDtypeStruct((n, cols), out_dtype),
        compiler_params=_params("parallel"),
        name="matmul",
    )(x2, w)


def _memattn_kernel(*refs, with_outproj):
    if with_outproj:
        o_ref, wmix_ref, gmix_ref, bmix_ref = refs[:4]
        refs = refs[4:]
    x_ref, wq_ref, kv_ref, wo_ref, g_ref, b_ref, rw_ref, rb_ref, y_ref, cls_ref, gate_ref = refs
    x = x_ref[...]
    if with_outproj:
        x = _layer_norm(DEEPNORM_ALPHA * x + _dot(o_ref[...], wmix_ref[...]), gmix_ref[...], bmix_ref[...])
    q = (_dot(x.astype(BF16), wq_ref[...]) * (MEM_HEAD_DIM ** -0.5)).astype(BF16)
    outs = []
    for h in range(MEM_HEADS):
        cs = slice(h * MEM_HEAD_DIM, (h + 1) * MEM_HEAD_DIM)
        kh = kv_ref[0, :, cs]
        vh = kv_ref[0, :, D_MODEL + h * MEM_HEAD_DIM:D_MODEL + (h + 1) * MEM_HEAD_DIM]
        s = _dot_nt(q[:, cs], kh)
        m = jnp.max(s, axis=-1, keepdims=True)
        p = jnp.exp(s - m)
        denom = jnp.sum(p, axis=-1, keepdims=True)
        outs.append(_dot(p.astype(BF16), vh) / denom)
    o = jnp.concatenate(outs, axis=-1).astype(BF16)
    h_out = _dot(o, wo_ref[...])
    y = _layer_norm(DEEPNORM_ALPHA * x + h_out, g_ref[...], b_ref[...])
    y_ref[...] = y
    cls, gates = _route(y, rw_ref, rb_ref)
    cls_ref[0] = cls
    gate_ref[0] = gates


def _memory_attention(x2, kv, wq, wo, g, b, rw_cat, rb_t, *, seq, mixer=None):
    d = x2.shape[1]
    batch, n_mem = kv.shape[0], kv.shape[1]
    tm = TOKEN_TILE
    tiles = seq // tm
    n = batch * seq
    fixed = lambda bi, i: (0, 0)
    row = lambda bi, i: (bi * tiles + i, 0)
    mixer_args, mixer_specs = [], []
    if mixer is not None:
        o, w_mix, g_mix, b_mix = mixer
        mixer_args = [o, w_mix, g_mix, b_mix]
        mixer_specs = [pl.BlockSpec((tm, o.shape[1]), row), pl.BlockSpec(w_mix.shape, fixed),
                       pl.BlockSpec((1, d), fixed), pl.BlockSpec((1, d), fixed)]
    y, cls, gates = pl.pallas_call(
        functools.partial(_memattn_kernel, with_outproj=mixer is not None),
        grid=(batch, tiles),
        in_specs=mixer_specs + [
            pl.BlockSpec((tm, d), row),
            pl.BlockSpec((d, d), fixed),
            pl.BlockSpec((1, n_mem, 2 * d), lambda bi, i: (bi, 0, 0)),
            pl.BlockSpec((d, d), fixed),
            pl.BlockSpec((1, d), fixed),
            pl.BlockSpec((1, d), fixed),
            pl.BlockSpec((d, 2 * LANES), fixed),
            pl.BlockSpec((N_EXPERTS, tm), fixed),
        ],
        out_specs=[
            pl.BlockSpec((tm, d), lambda bi, i: (bi * tiles + i, 0)),
            pl.BlockSpec((1, 1, tm), lambda bi, i: (bi * tiles + i, 0, 0)),
            pl.BlockSpec((1, 2, tm), lambda bi, i: (bi * tiles + i, 0, 0)),
        ],
        out_shape=[
            jax.ShapeDtypeStruct((n, d), F32),
            jax.ShapeDtypeStruct((n // tm, 1, tm), jnp.int32),
            jax.ShapeDtypeStruct((n // tm, 2, tm), F32),
        ],
        compiler_params=_params("parallel", "parallel"),
        name="memory_attention",
    )(*mixer_args, x2, wq, kv, wo, g, b, rw_cat, rb_t)
    return y, cls.reshape(n), gates[:, 0, :].reshape(n), gates[:, 1, :].reshape(n)


def _route(x, w_ref, bias_ref):
    xh = x.astype(BF16)
    xl = (x - xh.astype(F32)).astype(BF16)
    both = _dot(xh, w_ref[...])
    logits = both[:, :LANES] + both[:, LANES:] + _dot(xl, w_ref[:, :LANES])
    lt = logits.T[:N_EXPERTS]
    scores = 1.0 / (1.0 + jnp.exp(-lt))
    biased = scores + bias_ref[...]
    ng = N_EXPERT_GROUPS
    a = [biased[l * ng:(l + 1) * ng] for l in range(EXPERTS_PER_GROUP)]
    sc = [scores[l * ng:(l + 1) * ng] for l in range(EXPERTS_PER_GROUP)]
    gs = None
    for i in range(EXPERTS_PER_GROUP):
        for k in range(i + 1, EXPERTS_PER_GROUP):
            pair = a[i] + a[k]
            gs = pair if gs is None else jnp.maximum(gs, pair)
    gidx = lax.broadcasted_iota(jnp.int32, gs.shape, 0).astype(F32)
    gmax = jnp.max(gs, axis=0, keepdims=True)
    grp_f = jnp.min(jnp.where(gs == gmax, gidx, float(ng)), axis=0, keepdims=True)
    sel = gidx == grp_f
    grp = grp_f.astype(jnp.int32)
    v = [jnp.sum(jnp.where(sel, a[l], 0.0), axis=0, keepdims=True) for l in range(EXPERTS_PER_GROUP)]
    sv = [jnp.sum(jnp.where(sel, sc[l], 0.0), axis=0, keepdims=True) for l in range(EXPERTS_PER_GROUP)]

    def first_argmax(vals, excluded):
        best = None
        for l, val in enumerate(vals):
            cand = val if excluded is None else jnp.where(excluded == l, NEG_INF, val)
            best = cand if best is None else jnp.maximum(best, cand)
        idx = jnp.full(best.shape, EXPERTS_PER_GROUP - 1, jnp.int32)
        for l in range(EXPERTS_PER_GROUP - 1, -1, -1):
            hit = vals[l] == best
            if excluded is not None:
                hit = hit & (excluded != l)
            idx = jnp.where(hit, l, idx)
        return idx

    i1 = first_argmax(v, None)
    i2 = first_argmax(v, i1)

    def pick(vals, idx):
        out = vals[0]
        for l in range(1, EXPERTS_PER_GROUP):
            out = jnp.where(idx == l, vals[l], out)
        return out

    s1 = pick(sv, i1)
    s2 = pick(sv, i2)
    tot = s1 + s2
    lo = jnp.minimum(i1, i2)
    hi = jnp.maximum(i1, i2)
    first_is_lo = i1 < i2
    g_lo = jnp.where(first_is_lo, s1, s2) / tot
    g_hi = jnp.where(first_is_lo, s2, s1) / tot
    pair_idx = jnp.right_shift(lo * (7 - lo), 1) + (hi - lo - 1)
    return grp * N_PAIRS + pair_idx, jnp.concatenate([g_lo, g_hi], axis=0)


def _moe_kernel(src_ref, dst_ref, lo_ref, hi_ref, nblk_ref,
                x_hbm, gate_ref, gu_lo_ref, gu_hi_ref, dn_lo_ref, dn_hi_ref, g_ref, b_ref,
                y_hbm, xbuf, ybuf, gsem, ssem, *, n_tokens):
    blk = pl.program_id(0)
    n_used = nblk_ref[0]
    rows = MOE_BLOCK

    def gather_copy(block, i, slot):
        row = src_ref[block * rows + i]
        return pltpu.make_async_copy(x_hbm.at[pl.ds(row, 1)], xbuf.at[slot, pl.ds(i, 1)], gsem.at[slot])

    def scatter_copy(block, i, slot):
        row = dst_ref[block * rows + i]
        return pltpu.make_async_copy(ybuf.at[slot, pl.ds(i, 1)], y_hbm.at[pl.ds(row, 1)], ssem.at[slot])

    def start_all(copy, block, slot):
        for i in range(rows):
            copy(block, i, slot).start()

    def wait_all(buf, sem, slot):
        pltpu.make_async_copy(buf.at[slot], buf.at[slot], sem.at[slot]).wait()

    @pl.when(blk < n_used)
    def _():
        slot = blk % 2
        xs = blk % GATHER_BUFFERS
        ahead = jnp.minimum(blk + 2, n_used - 1)
        xs_ahead = (blk + 2) % GATHER_BUFFERS

        @pl.when(blk == 0)
        def _():
            start_all(gather_copy, 0, 0)
            start_all(gather_copy, jnp.minimum(1, n_used - 1), 1)
            ybuf[1] = jnp.zeros(ybuf.shape[1:], ybuf.dtype)
            for part in range(2):
                spare = pltpu.make_async_copy(ybuf.at[1], y_hbm.at[pl.ds(n_tokens + part * rows, rows)], ssem.at[1])
                spare.start()
                spare.wait()

        @pl.when(blk >= 2)
        def _():
            wait_all(ybuf, ssem, slot)

        wait_all(xbuf, gsem, xs)
        start_all(gather_copy, ahead, xs_ahead)

        x = xbuf[xs]
        xb = x.astype(BF16)

        def expert(gu_ref, dn_ref):
            gu = _dot(xb, gu_ref[0])
            gate_act = gu[:, :D_EXPERT]
            act = gate_act / (1.0 + jnp.exp(-gate_act)) * gu[:, D_EXPERT:]
            return _dot(act.astype(BF16), dn_ref[0])

        r_i = lax.broadcasted_iota(jnp.int32, (rows, rows), 0)
        c_i = lax.broadcasted_iota(jnp.int32, (rows, rows), 1)
        diag = r_i == c_i
        gates = gate_ref[0]
        g_lo = jnp.sum(jnp.where(diag, gates[0:1, :], 0.0), axis=1, keepdims=True)
        g_hi = jnp.sum(jnp.where(diag, gates[1:2, :], 0.0), axis=1, keepdims=True)
        h = g_lo * expert(gu_lo_ref, dn_lo_ref) + g_hi * expert(gu_hi_ref, dn_hi_ref)
        ybuf[slot] = _layer_norm(DEEPNORM_ALPHA * x + h, g_ref[...], b_ref[...])
        start_all(scatter_copy, blk, slot)

        @pl.when(blk == n_used - 1)
        def _():
            wait_all(xbuf, gsem, (blk + 1) % GATHER_BUFFERS)
            wait_all(xbuf, gsem, xs_ahead)

            @pl.when(blk >= 1)
            def _():
                wait_all(ybuf, ssem, 1 - slot)
            wait_all(ybuf, ssem, slot)


def _moe_experts(x2, slot_src, slot_dst, blk_lo, blk_hi, n_used, slot_gates, w_gu, w_dn, g, b, *, n):
    d = x2.shape[1]
    n_blocks = blk_lo.shape[0]
    rows = MOE_BLOCK
    grid_spec = pltpu.PrefetchScalarGridSpec(
        num_scalar_prefetch=5,
        grid=(n_blocks,),
        in_specs=[
            pl.BlockSpec(memory_space=pl.ANY),
            pl.BlockSpec((1, 2, rows), lambda i, *_: (i, 0, 0)),
            pl.BlockSpec((1, d, 2 * D_EXPERT), lambda i, src, dst, lo, hi, nb: (lo[i], 0, 0)),
            pl.BlockSpec((1, d, 2 * D_EXPERT), lambda i, src, dst, lo, hi, nb: (hi[i], 0, 0)),
            pl.BlockSpec((1, D_EXPERT, d), lambda i, src, dst, lo, hi, nb: (lo[i], 0, 0)),
            pl.BlockSpec((1, D_EXPERT, d), lambda i, src, dst, lo, hi, nb: (hi[i], 0, 0)),
            pl.BlockSpec((1, d), lambda i, *_: (0, 0)),
            pl.BlockSpec((1, d), lambda i, *_: (0, 0)),
        ],
        out_specs=pl.BlockSpec(memory_space=pl.ANY),
        scratch_shapes=[
            pltpu.VMEM((GATHER_BUFFERS, rows, d), F32),
            pltpu.VMEM((2, rows, d), F32),
            pltpu.SemaphoreType.DMA((GATHER_BUFFERS,)),
            pltpu.SemaphoreType.DMA((2,)),
        ],
    )
    return pl.pallas_call(
        functools.partial(_moe_kernel, n_tokens=n),
        grid_spec=grid_spec,
        out_shape=jax.ShapeDtypeStruct((n + 2 * rows, d), F32),
        compiler_params=_params("arbitrary"),
        name="moe_experts",
    )(slot_src, slot_dst, blk_lo, blk_hi, n_used, x2, slot_gates, w_gu, w_gu, w_dn, w_dn, g, b)


_PAIR_LO = np.array([0, 0, 0, 1, 1, 2], np.int32)
_PAIR_HI = np.array([1, 2, 3, 2, 3, 3], np.int32)


def _dispatch_tables(cls, g_lo, g_hi):
    n = cls.shape[0]
    rows = MOE_BLOCK
    n_blocks = n // rows + N_CLASSES
    i32 = jnp.int32
    cls_ids = jnp.arange(N_CLASSES, dtype=i32)
    _, order, glo_s, ghi_s = lax.sort((cls, jnp.arange(n, dtype=i32), g_lo, g_hi), num_keys=1)
    counts = jnp.sum((cls[:, None] == cls_ids[None, :]).astype(i32), axis=0)
    cls_blocks = (counts + rows - 1) // rows
    blk_end = jnp.cumsum(cls_blocks)
    blk_begin = blk_end - cls_blocks
    start = jnp.cumsum(counts) - counts
    n_used = blk_end[-1]
    blk = jnp.arange(n_blocks, dtype=i32)
    blk_eff = jnp.minimum(blk, n_used - 1)
    blk_cls = jnp.sum((blk_end[None, :] <= blk_eff[:, None]).astype(i32), axis=1)
    pick = blk_cls[:, None] == cls_ids[None, :]
    table = lambda t: jnp.sum(jnp.where(pick, t[None, :], 0), axis=1)
    blk_rank = (blk_eff - table(blk_begin)) * rows
    blk_nvalid = jnp.where(blk < n_used, jnp.clip(table(counts) - blk_rank, 0, rows), 0)
    row = jnp.arange(rows, dtype=i32)
    valid = row[None, :] < blk_nvalid[:, None]
    src = jnp.clip((table(start) + blk_rank)[:, None] + row[None, :], 0, n - 1)
    tok = order[src]
    spare = n + (blk % 2)[:, None] * rows + row[None, :]
    slot_src = jnp.where(valid, tok, 0).reshape(n_blocks * rows)
    slot_dst = jnp.where(valid, tok, spare).reshape(n_blocks * rows)
    slot_gates = jnp.stack([jnp.where(valid, glo_s[src], 0.0), jnp.where(valid, ghi_s[src], 0.0)], axis=1)
    grp = blk_cls // N_PAIRS
    pair = blk_cls % N_PAIRS
    pair_ids = jnp.arange(N_PAIRS, dtype=i32)
    pair_pick = pair[:, None] == pair_ids[None, :]
    blk_lo = grp * EXPERTS_PER_GROUP + jnp.sum(jnp.where(pair_pick, jnp.asarray(_PAIR_LO)[None, :], 0), axis=1)
    blk_hi = grp * EXPERTS_PER_GROUP + jnp.sum(jnp.where(pair_pick, jnp.asarray(_PAIR_HI)[None, :], 0), axis=1)
    return (slot_src.astype(i32), slot_dst.astype(i32), blk_lo.astype(i32), blk_hi.astype(i32),
            n_used.reshape(1).astype(i32), slot_gates)


def _tile_heads(t64):
    return jnp.concatenate([t64] * (LANES // HEAD_DIM), axis=-1)


def _rope_tables(seq):
    t = jnp.arange(seq, dtype=jnp.int32)

    def tables(pos, dim):
        inv_freq = ROPE_THETA ** (-jnp.arange(0, dim, 2, dtype=F32) / dim)
        ang = pos.astype(F32)[:, None] * inv_freq[None, :]
        return jnp.cos(ang), jnp.sin(ang)

    c1, s1 = tables(t, HEAD_DIM)
    cos_1d = _tile_heads(jnp.concatenate([c1, c1], axis=-1))
    sin_1d = _tile_heads(jnp.concatenate([-s1, s1], axis=-1))
    cr, sr = tables(t // GRID_W, HEAD_DIM // 2)
    cc, sc = tables(t % GRID_W, HEAD_DIM // 2)
    cos_ax = _tile_heads(jnp.concatenate([cr, cr, cc, cc], axis=-1))
    sin_ax = _tile_heads(jnp.concatenate([-sr, sr, -sc, sc], axis=-1))
    return (cos_1d, sin_1d), (cos_ax, sin_ax)


def _block_diag_ones():
    i = np.arange(2 * LANES)
    return jnp.asarray((i[:, None] // HEAD_DIM) == (i[None, :] // HEAD_DIM), BF16)


def kernel(x_prompt, x_sample, mem_prompt, mem_sample, a_w_in, a_sink, a_w_out, b_w_in, b_q_norm, b_k_norm,
           b_w_out, c_w_in, c_w_out, m_w_q, m_w_kv, m_w_out, ln_g, ln_b, router_w, router_bias, e_w_gu, e_w_down):
    seq = x_prompt.shape[1]
    assert x_sample.shape[1] == seq
    x = jnp.concatenate([x_prompt, x_sample], axis=0)
    mem = jnp.concatenate([mem_prompt, mem_sample], axis=0)
    batch = x.shape[0]
    n = batch * seq
    n_mem = mem.shape[1]
    x2 = x.reshape(n, D_MODEL)

    rope_1d, rope_ax = _rope_tables(seq)
    ones_bd = _block_diag_ones()
    unit_gains = jnp.ones((2, LANES), F32)
    scale = HEAD_DIM ** -0.5

    rw = router_w.astype(F32).reshape(D_MODEL, N_EXPERT_GROUPS, EXPERTS_PER_GROUP).transpose(0, 2, 1)
    rw = jnp.pad(rw.reshape(D_MODEL, N_EXPERTS), ((0, 0), (0, LANES - N_EXPERTS)))
    rw_hi = rw.astype(BF16)
    rw_cat = jnp.concatenate([rw_hi, (rw - rw_hi.astype(F32)).astype(BF16)], axis=1)
    rb = router_bias.astype(F32).reshape(N_EXPERT_GROUPS, EXPERTS_PER_GROUP).T.reshape(N_EXPERTS, 1)
    rb_t = jnp.broadcast_to(rb, (N_EXPERTS, TOKEN_TILE))

    cq, ck = C_HEADS * HEAD_DIM, C_KV_HEADS * HEAD_DIM

    for i in range(DEPTH):
        kind, j = i % N_MIXERS, i // N_MIXERS
        lg = lambda s_: ln_g[i, s_].astype(F32).reshape(1, D_MODEL)
        lb = lambda s_: ln_b[i, s_].astype(F32).reshape(1, D_MODEL)
        if kind == 0:
            nq, nk = A_HEADS * HEAD_DIM, A_KV_HEADS * HEAD_DIM
            q, k, v = _qkv_project(x2, a_w_in[j].astype(BF16), rope_1d[0], rope_1d[1], unit_gains, ones_bd, n=n,
                                   nq=nq, nk=nk, nv=nk, half=HEAD_DIM // 2, qk_norm=False,
                                   q_scale=scale * LOG2_E, seq=seq)
            o, = _banded_attention(q, k, v, a_sink[j].astype(F32), batch=batch, seq=seq, n_q_heads=A_HEADS,
                                   n_kv_heads=A_KV_HEADS, window=A_WINDOW, tq=BANDED_QUERY_BLOCK, with_lse=False)
            mixer = (o, a_w_out[j].astype(BF16), lg(0), lb(0))
        elif kind == 1:
            nq, nk = B_HEADS * HEAD_DIM, B_KV_HEADS * HEAD_DIM
            gains = jnp.stack([_tile_heads(b_q_norm[j].astype(F32)), _tile_heads(b_k_norm[j].astype(F32))], axis=0)
            q, k, v = _qkv_project(x2, b_w_in[j].astype(BF16), rope_ax[0], rope_ax[1], gains, ones_bd, n=n,
                                   nq=nq, nk=nk, nv=nk, half=HEAD_DIM // 4, qk_norm=True,
                                   q_scale=scale * LOG2_E, seq=seq, v_transposed=True)
            o = _flash_attention(q, k, v, batch=batch, seq=seq, n_q_heads=B_HEADS, n_kv_heads=B_KV_HEADS,
                                 tq=256, tk=1024)
            mixer = (o, b_w_out[j].astype(BF16), lg(0), lb(0))
        else:
            dilations = [dil for _, dil in C_PATTERNS]
            qkv = _qkv_project_dilated(x2, c_w_in[j].astype(BF16), rope_1d[0], rope_1d[1], n=n, seq=seq,
                                       dilations=dilations, cq=cq, ck=ck, q_scale=scale * LOG2_E)
            no_sink = jnp.full((C_HEADS,), NEG_INF, F32)
            os_, lses = [], []
            for (window, dil), (q, k, v) in zip(C_PATTERNS, qkv):
                half_w = (window // 2) // dil
                o, lse = _banded_attention(q.reshape(n, cq), k.reshape(n, ck), v.reshape(n, ck), no_sink,
                                           batch=batch * dil, seq=seq // dil, n_q_heads=C_HEADS,
                                           n_kv_heads=C_KV_HEADS, window=half_w, tq=BANDED_QUERY_BLOCK,
                                           with_lse=True)
                shape = (n, cq) if dil == 1 else (batch, dil, seq // dil, cq)
                os_.append(o.reshape(shape))
                lses.append(lse.reshape(shape))
            x2 = _merge_outproj_ln(os_, lses, c_w_out[j].astype(BF16), x2, lg(0), lb(0), n=n, seq=seq,
                                   dilations=dilations)
            mixer = None

        kv = _matmul(mem.reshape(batch * n_mem, D_MODEL), m_w_kv[i].astype(BF16), BF16, tm=n_mem)
        x2, cls, g_lo, g_hi = _memory_attention(x2, kv.reshape(batch, n_mem, 2 * D_MODEL), m_w_q[i].astype(BF16),
                                                m_w_out[i].astype(BF16), lg(1), lb(1), rw_cat, rb_t, seq=seq,
                                                mixer=mixer)
        slot_src, slot_dst, blk_lo, blk_hi, n_used, slot_gates = _dispatch_tables(cls, g_lo, g_hi)
        x2 = _moe_experts(x2, slot_src, slot_dst, blk_lo, blk_hi, n_used, slot_gates,
                          e_w_gu[i].astype(BF16), e_w_down[i].astype(BF16), lg(2), lb(2), n=n)

    n_prompt = x_prompt.shape[0] * seq
    return (x2[:n_prompt].reshape(x_prompt.shape), x2[n_prompt:n].reshape(x_sample.shape))
```
